```python
import math
import jax, jax.numpy as jnp
from jax import lax
import numpy as np

D_MODEL = 1024
BATCH = 4
SEQ = 8192
DEPTH = 2
DEC_BATCH = 32
DEC_SEQ = 1
PAST_LEN = 16384
PAGE_SIZE = 128

S_GROUP = 16
S_GROUPS = 16
S_STATE = 64
S_WIDTH = S_GROUPS * S_GROUP
DT_MIN = 1e-3
DT_MAX = 1e-1
R_HEAD = 64
R_HEADS = 4
R_WIDTH = R_HEADS * R_HEAD
W_LORA = 32
A_LORA = 32
G_LORA = 64
R_IN = 3 * R_WIDTH + W_LORA + A_LORA + G_LORA
GN_EPS = 64e-5
A_HEAD = 64
A_HEADS = 8
A_KV_HEADS = 4
A_QW = A_HEADS * A_HEAD
A_KVW = A_KV_HEADS * A_HEAD
ROT_DIM = A_HEAD // 4
ROPE_THETA = 500000.0
MOBA_BLOCK = 256
MOBA_TOPK = 3
Q_CHUNK = 32
N_BRANCH = 3
O_SSM = N_BRANCH * D_MODEL
O_RWKV = O_SSM + S_WIDTH
O_Q = O_RWKV + R_IN
O_K = O_Q + A_QW
O_V = O_K + A_KVW
N_IN = O_V + A_KVW
D_FF = 2816
CONV_W = 3
ALPHA = (2 * DEPTH) ** 0.25
BETA = (8 * DEPTH) ** -0.25
LN_EPS = 1e-5

kernel_name = 'hybrid_s5_rwkv7_moba_convffn_step'


def layer_norm(x, g, b):
    xf = x.astype(jnp.float32)
    mu = jnp.mean(xf, -1, keepdims=True)
    var = jnp.mean(jnp.square(xf - mu), -1, keepdims=True)
    return ((xf - mu) * lax.rsqrt(var + LN_EPS) * g + b).astype(x.dtype)


def rotary(x, pos):
    half = ROT_DIM // 2
    inv = ROPE_THETA ** (-jnp.arange(half, dtype=jnp.float32) / half)
    ang = pos.astype(jnp.float32)[:, None] * inv[None, :]
    cos = jnp.cos(ang)[None, :, None, :]
    sin = jnp.sin(ang)[None, :, None, :]
    xr = x[..., :ROT_DIM].astype(jnp.float32)
    x1, x2 = xr[..., :half], xr[..., half:]
    rot = jnp.concatenate([x1 * cos - x2 * sin, x2 * cos + x1 * sin], -1).astype(x.dtype)
    return jnp.concatenate([rot, x[..., ROT_DIM:]], -1)


def s5_branch(u, a_re, a_im, log_dt, b_re, b_im, c_re, c_im, d, w_glu, b_glu, s0_re, s0_im):
    f32 = jnp.float32
    nb, L, _ = u.shape
    ug = u.reshape(nb, L, S_GROUPS, S_GROUP).astype(f32)
    dt = jnp.exp(log_dt.astype(f32))[:, None]
    ar, ai = a_re.astype(f32), a_im.astype(f32)
    mag = jnp.exp(ar * dt)
    abar_re, abar_im = mag * jnp.cos(ai * dt), mag * jnp.sin(ai * dt)
    den = ar * ar + ai * ai
    em_re = abar_re - 1.0
    coef_re = (em_re * ar + abar_im * ai) / den
    coef_im = (abar_im * ar - em_re * ai) / den
    bu_re = jnp.einsum('blgc,gpc->blgp', ug, b_re.astype(f32))
    bu_im = jnp.einsum('blgc,gpc->blgp', ug, b_im.astype(f32))
    x_re = coef_re * bu_re - coef_im * bu_im
    x_im = coef_re * bu_im + coef_im * bu_re
    s0_re, s0_im = s0_re.astype(f32), s0_im.astype(f32)
    x_re = x_re.at[:, 0].add(abar_re * s0_re - abar_im * s0_im)
    x_im = x_im.at[:, 0].add(abar_re * s0_im + abar_im * s0_re)
    shp = x_re.shape
    elems = (jnp.broadcast_to(abar_re, shp), jnp.broadcast_to(abar_im, shp), x_re, x_im)

    def combine(e1, e2):
        a1r, a1i, b1r, b1i = e1
        a2r, a2i, b2r, b2i = e2
        return (a2r * a1r - a2i * a1i, a2r * a1i + a2i * a1r,
                a2r * b1r - a2i * b1i + b2r, a2r * b1i + a2i * b1r + b2i)

    _, _, s_re, s_im = lax.associative_scan(combine, elems, axis=1)
    y = (jnp.einsum('blgp,gcp->blgc', s_re, c_re.astype(f32))
         - jnp.einsum('blgp,gcp->blgc', s_im, c_im.astype(f32)) + d * ug)
    z = jax.nn.gelu(y.reshape(nb, L, S_WIDTH))
    out = z * jax.nn.sigmoid(z @ w_glu.astype(f32) + b_glu)
    return out.astype(u.dtype), s_re[:, -1], s_im[:, -1]


def rwkv7_branch(c, shift0, s0, mu, w0, w2, a0, a2, g2, k_k, k_a, r_k, lnx_g, lnx_b):
    f32 = jnp.float32
    nb, L, _ = c.shape
    prev = jnp.concatenate([shift0[:, None].astype(c.dtype), c[:, :-1]], axis=1)
    cf = (c + (prev - c) * mu).astype(f32)
    o = 3 * R_WIDTH
    r = cf[..., :R_WIDTH]
    k = cf[..., R_WIDTH:2 * R_WIDTH]
    v = cf[..., 2 * R_WIDTH:o]
    wl = cf[..., o:o + W_LORA]
    al = cf[..., o + W_LORA:o + W_LORA + A_LORA]
    gl = cf[..., o + W_LORA + A_LORA:]
    w_log = -jax.nn.softplus(-(w0 + jnp.tanh(wl) @ w2)) - 0.5
    decay = jnp.exp(-jnp.exp(w_log))
    a = jax.nn.sigmoid(a0 + al @ a2)
    g = jax.nn.sigmoid(gl) @ g2

    def heads(t):
        return t.reshape(nb, L, R_HEADS, R_HEAD)

    kk = heads(k * k_k)
    kk = kk * lax.rsqrt(jnp.maximum(jnp.sum(kk * kk, -1, keepdims=True), 1e-24))
    k = k * (1.0 + (a - 1.0) * k_a)
    rh, wh, kh, vh, ah = heads(r), heads(decay), heads(k), heads(v), heads(a)
    a_vec = -kk
    b_vec = kk * ah

    def step(S, inp):
        r_t, w_t, k_t, v_t, a_t, b_t = inp
        sa = jnp.einsum('bhij,bhj->bhi', S, a_t)
        S = (S * w_t[:, :, None, :] + sa[..., None] * b_t[:, :, None, :]
             + v_t[..., None] * k_t[:, :, None, :])
        return S, jnp.einsum('bhij,bhj->bhi', S, r_t)

    xs = tuple(jnp.moveaxis(t, 1, 0) for t in (rh, wh, kh, vh, a_vec, b_vec))
    s_fin, y = lax.scan(step, s0.astype(f32), xs)
    y = jnp.moveaxis(y, 0, 1)
    m = jnp.mean(y, -1, keepdims=True)
    var = jnp.mean(jnp.square(y - m), -1, keepdims=True)
    y = ((y - m) * lax.rsqrt(var + GN_EPS)).reshape(nb, L, R_WIDTH) * lnx_g + lnx_b
    bonus = jnp.sum(rh * kh * r_k, -1, keepdims=True) * vh
    y = y + bonus.reshape(nb, L, R_WIDTH)
    return (y * g).astype(c.dtype), c[:, -1], s_fin


def moba_attend(q, q_pos, kb, vb, k_means, n_topk):
    f32 = jnp.float32
    nb, H, Q, dh = q.shape
    NB = kb.shape[1]
    kv_of_h = jnp.arange(H) // (H // A_KV_HEADS)
    qf = q.astype(f32)
    own = q_pos // MOBA_BLOCK
    gate = jnp.einsum('bhqd,bnhd->bhqn', qf, k_means[:, :, kv_of_h])
    past = jnp.arange(NB)[None, :] < own[:, None]
    gate = jnp.where(past, gate, -jnp.inf)
    _, top_idx = lax.top_k(gate, n_topk)
    own_idx = jnp.broadcast_to(own[None, None, :, None], (nb, H, Q, 1)).astype(top_idx.dtype)
    idx = jnp.concatenate([top_idx, own_idx], -1)
    rank = jnp.arange(n_topk + 1)
    blk_ok = jnp.where(rank[None, :] < n_topk, rank[None, :] < own[:, None], True)
    bi = jnp.arange(nb)[:, None, None, None]
    hi = kv_of_h[None, :, None, None]
    kg = kb[bi, idx, :, hi].astype(f32)
    vg = vb[bi, idx, :, hi].astype(f32)
    key_pos = idx[..., None] * MOBA_BLOCK + jnp.arange(MOBA_BLOCK)
    mask = blk_ok[None, None, :, :, None] & (key_pos <= q_pos[None, None, :, None, None])
    s = jnp.einsum('bhqd,bhqnsd->bhqns', qf, kg) * (dh ** -0.5)
    s = jnp.where(mask, s, -jnp.inf)
    p = jax.nn.softmax(s.reshape(nb, H, Q, -1), axis=-1).reshape(s.shape)
    o = jnp.einsum('bhqns,bhqnsd->bhqd', p, vg)
    return o.astype(q.dtype)


def moba_branch(q, pos, k_parts, v_parts):
    nb, L, H, dh = q.shape
    T = sum(t.shape[1] for t in k_parts)
    NB = -(-T // MOBA_BLOCK)
    pad = NB * MOBA_BLOCK - T

    def to_blocks(parts):
        dt = parts[0].dtype
        z = jnp.zeros((nb, pad, A_KV_HEADS, dh), dt)
        full = jnp.concatenate([t.astype(dt) for t in parts] + [z], axis=1)
        return full.reshape(nb, NB, MOBA_BLOCK, A_KV_HEADS, dh)

    kb = to_blocks(k_parts)
    vb = to_blocks(v_parts)
    k_means = jnp.mean(kb, axis=2, dtype=jnp.float32)
    n_topk = min(MOBA_TOPK, NB)
    qc = Q_CHUNK if L % Q_CHUNK == 0 else L
    nc = L // qc
    q_chunks = q.reshape(nb, nc, qc, H, dh).transpose(1, 0, 3, 2, 4)
    pos_chunks = pos.reshape(nc, qc)
    o = lax.map(lambda a: moba_attend(a[0], a[1], kb, vb, k_means, n_topk), (q_chunks, pos_chunks))
    return o.transpose(1, 0, 3, 2, 4).reshape(nb, L, H * dh)


def mixer_sublayer(x, pos, l, P, s_re0, s_im0, rwkv0, shift0, k_parts, v_parts):
    nb, L, _ = x.shape
    h = jnp.einsum('bld,dn->bln', x, P['w_in'][l])
    gates = jax.nn.sigmoid(h[..., :O_SSM].astype(jnp.float32)).reshape(nb, L, N_BRANCH, D_MODEL)
    u = h[..., O_SSM:O_RWKV]
    c = h[..., O_RWKV:O_Q]
    q = rotary(h[..., O_Q:O_K].reshape(nb, L, A_HEADS, A_HEAD), pos)
    k = rotary(h[..., O_K:O_V].reshape(nb, L, A_KV_HEADS, A_HEAD), pos)
    v = h[..., O_V:].reshape(nb, L, A_KV_HEADS, A_HEAD)
    y_s, s_re, s_im = s5_branch(u, P['ssm_a_re'][l], P['ssm_a_im'][l], P['ssm_log_dt'][l],
                                P['ssm_b_re'][l], P['ssm_b_im'][l], P['ssm_c_re'][l], P['ssm_c_im'][l],
                                P['ssm_d'][l], P['ssm_w_glu'][l], P['ssm_b_glu'][l], s_re0, s_im0)
    y_r, shift, rwkv = rwkv7_branch(c, shift0, rwkv0, P['rwkv_mu'][l], P['rwkv_w0'][l], P['rwkv_w2'][l],
                                    P['rwkv_a0'][l], P['rwkv_a2'][l], P['rwkv_g2'][l], P['rwkv_k_k'][l],
                                    P['rwkv_k_a'][l], P['rwkv_r_k'][l], P['rwkv_lnx_g'][l], P['rwkv_lnx_b'][l])
    y_a = moba_branch(q, pos, k_parts + [k], v_parts + [v])
    merged = (gates[:, :, 0] * (y_s @ P['proj_ssm'][l])
              + gates[:, :, 1] * (y_r @ P['proj_rwkv'][l])
              + gates[:, :, 2] * (y_a @ P['proj_attn'][l]))
    out = merged.astype(x.dtype) @ P['w_o'][l]
    x = layer_norm(ALPHA * x + out, P['ln1_g'][l], P['ln1_b'][l])
    return x, (k, v, s_re, s_im, rwkv, shift)


def ffn_sublayer(x, l, P, conv0):
    L = x.shape[1]
    up = x @ P['ffn_w_up'][l]
    ext = jnp.concatenate([conv0.astype(up.dtype), up], axis=1)
    w = P['ffn_conv_w'][l]
    cv = P['ffn_conv_b'][l] + w[0] * ext[:, 0:L]
    for j in range(1, CONV_W):
        cv = cv + w[j] * ext[:, j:j + L]
    hmid = jax.nn.gelu(cv[..., :D_FF]) * cv[..., D_FF:]
    out = hmid @ P['ffn_w_down'][l]
    x = layer_norm(ALPHA * x + out, P['ln2_g'][l], P['ln2_b'][l])
    return x, ext[:, -(CONV_W - 1):]


def trunk(x, pos, P, ssm_re0, ssm_im0, rwkv0, shift0, conv0, cache_k, cache_v, page_table):
    f32 = jnp.float32
    nb = x.shape[0]
    x = layer_norm(x, P['ln_in_g'], P['ln_in_b'])
    ks, vs, sres, sims, rws, shs, cvs = [], [], [], [], [], [], []
    for l in range(DEPTH):
        if cache_k is None:
            s_re0 = jnp.zeros((nb, S_GROUPS, S_STATE), f32)
            s_im0 = jnp.zeros((nb, S_GROUPS, S_STATE), f32)
            r0 = jnp.zeros((nb, R_HEADS, R_HEAD, R_HEAD), f32)
            sh0 = jnp.zeros((nb, R_IN), x.dtype)
            c0 = jnp.zeros((nb, CONV_W - 1, 2 * D_FF), x.dtype)
            k_parts, v_parts = [], []
        else:
            s_re0, s_im0, r0, sh0, c0 = ssm_re0[l], ssm_im0[l], rwkv0[l], shift0[l], conv0[l]
            k_parts = [cache_k[l, page_table].reshape(nb, -1, A_KV_HEADS, A_HEAD)]
            v_parts = [cache_v[l, page_table].reshape(nb, -1, A_KV_HEADS, A_HEAD)]
        x, (k, v, s_re, s_im, rw, sh) = mixer_sublayer(x, pos, l, P, s_re0, s_im0, r0, sh0, k_parts, v_parts)
        x, cv = ffn_sublayer(x, l, P, c0)
        ks.append(k); vs.append(v); sres.append(s_re); sims.append(s_im)
        rws.append(rw); shs.append(sh); cvs.append(cv)
    return x, (jnp.stack(ks), jnp.stack(vs), jnp.stack(sres), jnp.stack(sims),
               jnp.stack(rws), jnp.stack(shs), jnp.stack(cvs))


def setup_inputs(seed: int = 0) -> dict:
    key = jax.random.key(seed)
    keys = iter(jax.random.split(key, 64))
    f32 = jnp.float32

    def nrm(shape, scale=1.0):
        return jax.random.normal(next(keys), shape, f32) * scale

    n_pages = PAST_LEN // PAGE_SIZE
    n_used = DEC_BATCH * n_pages
    n_pool = n_used + max(1, n_used // 4)
    perm = jax.random.permutation(next(keys), n_pool)
    page_table = perm[:n_used].reshape(DEC_BATCH, n_pages).astype(jnp.int32)
    inp = {}
    inp['x_prompt'] = nrm((BATCH, SEQ, D_MODEL))
    inp['x_sample'] = nrm((DEC_BATCH, DEC_SEQ, D_MODEL))
    inp['cache_k'] = nrm((DEPTH, n_pool, PAGE_SIZE, A_KV_HEADS, A_HEAD))
    inp['cache_v'] = nrm((DEPTH, n_pool, PAGE_SIZE, A_KV_HEADS, A_HEAD))
    inp['page_table'] = page_table
    inp['state_ssm_re'] = nrm((DEPTH, DEC_BATCH, S_GROUPS, S_STATE), 0.1)
    inp['state_ssm_im'] = nrm((DEPTH, DEC_BATCH, S_GROUPS, S_STATE), 0.1)
    inp['state_rwkv'] = nrm((DEPTH, DEC_BATCH, R_HEADS, R_HEAD, R_HEAD), 0.3)
    inp['state_rwkv_shift'] = nrm((DEPTH, DEC_BATCH, R_IN))
    inp['state_conv'] = nrm((DEPTH, DEC_BATCH, CONV_W - 1, 2 * D_FF))
    inp['ln_in_g'] = 1.0 + nrm((D_MODEL,), 0.02)
    inp['ln_in_b'] = nrm((D_MODEL,), 0.02)
    inp['w_in'] = nrm((DEPTH, D_MODEL, N_IN), D_MODEL ** -0.5)
    inp['ssm_a_re'] = -0.5 + nrm((DEPTH, S_GROUPS, S_STATE), 0.01)
    inp['ssm_a_im'] = math.pi * jnp.arange(S_STATE, dtype=f32) + nrm((DEPTH, S_GROUPS, S_STATE), 0.01)
    inp['ssm_log_dt'] = jax.random.uniform(next(keys), (DEPTH, S_GROUPS), f32, math.log(DT_MIN), math.log(DT_MAX))
    inp['ssm_b_re'] = nrm((DEPTH, S_GROUPS, S_STATE, S_GROUP), S_GROUP ** -0.5)
    inp['ssm_b_im'] = nrm((DEPTH, S_GROUPS, S_STATE, S_GROUP), S_GROUP ** -0.5)
    inp['ssm_c_re'] = nrm((DEPTH, S_GROUPS, S_GROUP, S_STATE), S_STATE ** -0.5)
    inp['ssm_c_im'] = nrm((DEPTH, S_GROUPS, S_GROUP, S_STATE), S_STATE ** -0.5)
    inp['ssm_d'] = nrm((DEPTH, S_GROUPS, S_GROUP))
    inp['ssm_w_glu'] = nrm((DEPTH, S_WIDTH, S_WIDTH), S_WIDTH ** -0.5)
    inp['ssm_b_glu'] = nrm((DEPTH, S_WIDTH), 0.02)
    inp['rwkv_mu'] = jax.random.uniform(next(keys), (DEPTH, R_IN), f32)
    inp['rwkv_w0'] = jax.random.uniform(next(keys), (DEPTH, R_WIDTH), f32, -6.0, -1.0)
    inp['rwkv_w2'] = nrm((DEPTH, W_LORA, R_WIDTH), 0.1 * W_LORA ** -0.5)
    inp['rwkv_a0'] = nrm((DEPTH, R_WIDTH), 0.1)
    inp['rwkv_a2'] = nrm((DEPTH, A_LORA, R_WIDTH), 0.1 * A_LORA ** -0.5)
    inp['rwkv_g2'] = nrm((DEPTH, G_LORA, R_WIDTH), G_LORA ** -0.5)
    inp['rwkv_k_k'] = 0.85 + nrm((DEPTH, R_WIDTH), 0.02)
    inp['rwkv_k_a'] = 1.0 + nrm((DEPTH, R_WIDTH), 0.02)
    inp['rwkv_r_k'] = -0.04 + nrm((DEPTH, R_HEADS, R_HEAD), 0.02)
    inp['rwkv_lnx_g'] = 1.0 + nrm((DEPTH, R_WIDTH), 0.02)
    inp['rwkv_lnx_b'] = nrm((DEPTH, R_WIDTH), 0.02)
    inp['proj_ssm'] = nrm((DEPTH, S_WIDTH, D_MODEL), S_WIDTH ** -0.5)
    inp['proj_rwkv'] = nrm((DEPTH, R_WIDTH, D_MODEL), R_WIDTH ** -0.5)
    inp['proj_attn'] = nrm((DEPTH, A_QW, D_MODEL), A_QW ** -0.5)
    inp['w_o'] = nrm((DEPTH, D_MODEL, D_MODEL), BETA * D_MODEL ** -0.5)
    inp['ln1_g'] = 1.0 + nrm((DEPTH, D_MODEL), 0.02)
    inp['ln1_b'] = nrm((DEPTH, D_MODEL), 0.02)
    inp['ffn_w_up'] = nrm((DEPTH, D_MODEL, 2 * D_FF), D_MODEL ** -0.5)
    inp['ffn_conv_w'] = nrm((DEPTH, CONV_W, 2 * D_FF), CONV_W ** -0.5)
    inp['ffn_conv_b'] = nrm((DEPTH, 2 * D_FF), 0.02)
    inp['ffn_w_down'] = nrm((DEPTH, D_FF, D_MODEL), BETA * D_FF ** -0.5)
    inp['ln2_g'] = 1.0 + nrm((DEPTH, D_MODEL), 0.02)
    inp['ln2_b'] = nrm((DEPTH, D_MODEL), 0.02)
    return inp


def reference(x_prompt, x_sample, cache_k, cache_v, page_table, state_ssm_re, state_ssm_im,
              state_rwkv, state_rwkv_shift, state_conv, ln_in_g, ln_in_b, w_in,
              ssm_a_re, ssm_a_im, ssm_log_dt, ssm_b_re, ssm_b_im, ssm_c_re, ssm_c_im, ssm_d,
              ssm_w_glu, ssm_b_glu, rwkv_mu, rwkv_w0, rwkv_w2, rwkv_a0, rwkv_a2, rwkv_g2,
              rwkv_k_k, rwkv_k_a, rwkv_r_k, rwkv_lnx_g, rwkv_lnx_b, proj_ssm, proj_rwkv, proj_attn,
              w_o, ln1_g, ln1_b, ffn_w_up, ffn_conv_w, ffn_conv_b, ffn_w_down, ln2_g, ln2_b):
    P = dict(ln_in_g=ln_in_g, ln_in_b=ln_in_b, w_in=w_in,
             ssm_a_re=ssm_a_re, ssm_a_im=ssm_a_im, ssm_log_dt=ssm_log_dt,
             ssm_b_re=ssm_b_re, ssm_b_im=ssm_b_im, ssm_c_re=ssm_c_re, ssm_c_im=ssm_c_im,
             ssm_d=ssm_d, ssm_w_glu=ssm_w_glu, ssm_b_glu=ssm_b_glu,
             rwkv_mu=rwkv_mu, rwkv_w0=rwkv_w0, rwkv_w2=rwkv_w2, rwkv_a0=rwkv_a0, rwkv_a2=rwkv_a2,
             rwkv_g2=rwkv_g2, rwkv_k_k=rwkv_k_k, rwkv_k_a=rwkv_k_a, rwkv_r_k=rwkv_r_k,
             rwkv_lnx_g=rwkv_lnx_g, rwkv_lnx_b=rwkv_lnx_b,
             proj_ssm=proj_ssm, proj_rwkv=proj_rwkv, proj_attn=proj_attn, w_o=w_o,
             ln1_g=ln1_g, ln1_b=ln1_b, ffn_w_up=ffn_w_up, ffn_conv_w=ffn_conv_w,
             ffn_conv_b=ffn_conv_b, ffn_w_down=ffn_w_down, ln2_g=ln2_g, ln2_b=ln2_b)
    pos_p = jnp.arange(x_prompt.shape[1], dtype=jnp.int32)
    y_prompt, (k_p, v_p, sre_p, sim_p, rw_p, sh_p, cv_p) = trunk(
        x_prompt, pos_p, P, None, None, None, None, None, None, None, None)
    past_len = page_table.shape[1] * PAGE_SIZE
    pos_s = past_len + jnp.arange(x_sample.shape[1], dtype=jnp.int32)
    y_sample, (k_s, v_s, sre_s, sim_s, rw_s, sh_s, cv_s) = trunk(
        x_sample, pos_s, P, state_ssm_re, state_ssm_im, state_rwkv, state_rwkv_shift, state_conv,
        cache_k, cache_v, page_table)
    return (y_prompt, y_sample, k_p, v_p, k_s, v_s, sre_p, sim_p, sre_s, sim_s,
            rw_p, rw_s, sh_p, sh_s, cv_p, cv_s)
```

```python
import functools
import math

import jax
import jax.numpy as jnp
from jax import lax
from jax.experimental import pallas as pl
from jax.experimental.pallas import tpu as pltpu

F32 = jnp.float32
BF16 = jnp.bfloat16

D_MODEL = 1024
PAGE_SIZE = 128
S_GROUP = 16
S_GROUPS = 16
S_STATE = 64
S_WIDTH = S_GROUPS * S_GROUP
S_LANES = S_GROUPS * S_STATE
R_HEAD = 64
R_HEADS = 4
R_WIDTH = R_HEADS * R_HEAD
W_LORA = 32
A_LORA = 32
G_LORA = 64
N_LORA = W_LORA + A_LORA + G_LORA
R_IN = 3 * R_WIDTH + N_LORA
GN_EPS = 64e-5
A_HEAD = 64
A_HEADS = 8
A_KV_HEADS = 4
A_GROUP = A_HEADS // A_KV_HEADS
A_QW = A_HEADS * A_HEAD
A_KVW = A_KV_HEADS * A_HEAD
ROT_DIM = A_HEAD // 4
ROPE_THETA = 500000.0
MOBA_BLOCK = 256
MOBA_TOPK = 3
N_BRANCH = 3
O_SSM = N_BRANCH * D_MODEL
N_REST = S_WIDTH + R_IN + A_QW + 2 * A_KVW
D_FF = 2816
CONV_W = 3
DEPTH = 2
ALPHA = (2 * DEPTH) ** 0.25
LN_EPS = 1e-5
ATT_SCALE = A_HEAD ** -0.5

VMEM_LIMIT_BYTES = 56 * 1024 * 1024
LANES = 128
SUBLANES = 8

RWKV_CHUNK = 64
NEG_INF = float("-inf")

_NN = (((1,), (0,)), ((), ()))
_NT = (((1,), (1,)), ((), ()))
_TN = (((0,), (0,)), ((), ()))


def _cparams(*sem):
    return pltpu.CompilerParams(dimension_semantics=sem or None, vmem_limit_bytes=VMEM_LIMIT_BYTES)


def _bdot(a, b, dims=_NN):
    return lax.dot_general(a.astype(BF16), b.astype(BF16), dims, preferred_element_type=F32)


def _dot3(a, b, dims=_NN):
    ah = a.astype(BF16)
    al = (a - ah.astype(F32)).astype(BF16)
    bh = b.astype(BF16)
    bl = (b - bh.astype(F32)).astype(BF16)
    f = lambda x, y: lax.dot_general(x, y, dims, preferred_element_type=F32)
    return f(ah, bh) + f(ah, bl) + f(al, bh)


def _ln(x, g, b):
    mu = jnp.mean(x, -1, keepdims=True)
    xc = x - mu
    var = jnp.mean(xc * xc, -1, keepdims=True)
    return xc * lax.rsqrt(var + LN_EPS) * g + b


def _head_sum(x, head):
    lane_head = lax.broadcasted_iota(jnp.int32, (1, x.shape[1]), 1) // head
    out = jnp.zeros_like(x)
    for h in range(x.shape[1] // head):
        m = lane_head == h
        s = jnp.sum(jnp.where(m, x, 0.0), -1, keepdims=True)
        out = jnp.where(m, s, out)
    return out


def _full(shape):
    n = len(shape)
    return pl.BlockSpec(shape, lambda *_: (0,) * n)


def _in_proj_kernel(pre_ln, x_ref, g_ref, b_ref, w_ref, cos_ref, sa_ref, sb_ref,
                    u_ref, c_ref, q_ref, k_ref, v_ref, *xn_ref):
    x = x_ref[...]
    if pre_ln:
        x = _ln(x, g_ref[...], b_ref[...])
        xn_ref[0][...] = x
    h = _bdot(x, w_ref[...])
    u_ref[...] = h[:, :S_WIDTH]
    c_ref[...] = h[:, S_WIDTH:S_WIDTH + R_IN]
    cos, sa, sb = cos_ref[...], sa_ref[...], sb_ref[...]
    base = S_WIDTH + R_IN
    nq = A_QW // LANES
    for j in range((A_QW + A_KVW) // LANES):
        ch = h[:, base + LANES * j: base + LANES * (j + 1)]
        rot = (ch * cos + pltpu.roll(ch, LANES - ROT_DIM // 2, 1) * sa
               + pltpu.roll(ch, ROT_DIM // 2, 1) * sb)
        if j < nq:
            q_ref[:, LANES * j:LANES * (j + 1)] = rot
        else:
            k_ref[:, LANES * (j - nq):LANES * (j - nq + 1)] = rot
    v_ref[...] = h[:, base + A_QW + A_KVW:]


def _in_proj(x, ln_g, ln_b, w, rope, tm, pos_tiles, pre_ln):
    m = x.shape[0]
    row = lambda width: pl.BlockSpec((tm, width), lambda i: (i, 0))
    tab = pl.BlockSpec((tm, LANES), lambda i: (i % pos_tiles, 0))
    widths = [S_WIDTH, R_IN, A_QW, A_KVW, A_KVW] + ([D_MODEL] if pre_ln else [])
    return pl.pallas_call(
        functools.partial(_in_proj_kernel, pre_ln),
        grid=(m // tm,),
        in_specs=[row(D_MODEL), _full((1, D_MODEL)), _full((1, D_MODEL)), _full((D_MODEL, N_REST)),
                  tab, tab, tab],
        out_specs=[row(wd) for wd in widths],
        out_shape=[jax.ShapeDtypeStruct((m, wd), F32) for wd in widths],
        compiler_params=_cparams("parallel"),
        name="in_proj",
    )(x, ln_g, ln_b, w, *rope)


def _rope_tables(pos):
    half = ROT_DIM // 2
    inv = ROPE_THETA ** (-jnp.arange(half, dtype=F32) / half)
    ang = pos.astype(F32)[:, None] * inv[None, :]
    cos, sin = jnp.cos(ang), jnp.sin(ang)
    n = pos.shape[0]
    pad = jnp.zeros((n, A_HEAD - ROT_DIM), F32)
    zero = jnp.zeros((n, half), F32)
    cos_h = jnp.concatenate([cos, cos, pad + 1.0], -1)
    sa_h = jnp.concatenate([-sin, zero, pad], -1)
    sb_h = jnp.concatenate([zero, sin, pad], -1)
    rep = LANES // A_HEAD
    return tuple(jnp.tile(t, (1, rep)) for t in (cos_h, sa_h, sb_h))


def _s5_params(a_re, a_im, log_dt, b_re, b_im, c_re, c_im):
    dt = jnp.exp(log_dt)[:, None]
    lam_re, lam_im = a_re * dt, a_im * dt

    def power(k):
        mag = jnp.exp(lam_re * k)
        return (mag * jnp.cos(lam_im * k)).reshape(-1), (mag * jnp.sin(lam_im * k)).reshape(-1)

    abar_re, abar_im = power(1.0)
    den = (a_re * a_re + a_im * a_im).reshape(-1)
    ar, ai = a_re.reshape(-1), a_im.reshape(-1)
    em_re = abar_re - 1.0
    coef = jnp.stack([(em_re * ar + abar_im * ai) / den, (abar_im * ar - em_re * ai) / den])
    pw = jnp.stack([jnp.stack(x) for x in zip(*[power(float(k)) for k in range(1, SUBLANES + 1)])])
    dbl = jnp.stack([jnp.stack(x) for x in zip(*[power(float(k)) for k in (1, 2, 4)])])
    eye = jnp.eye(S_GROUPS, dtype=F32)
    wb = jnp.concatenate([jnp.einsum("gpc,gh->gchp", b, eye).reshape(S_WIDTH, S_LANES)
                          for b in (b_re, b_im)], axis=1)
    wc = jnp.concatenate([jnp.einsum("gcp,gh->gphc", c, eye).reshape(S_LANES, S_WIDTH)
                          for c in (c_re, -c_im)], axis=0)
    return dict(coef=coef, pw=pw, dbl=dbl, abar=jnp.stack([abar_re, abar_im]),
                wb=wb.astype(BF16), wc=wc.astype(BF16))


def _s5_glu(s_re, s_im, u, wc_ref, d_ref, wg_ref, bg_ref):
    s = jnp.concatenate([s_re, s_im], axis=1)
    y = _bdot(s, wc_ref[...]) + d_ref[...] * u
    z = jax.nn.gelu(y)
    return z * jax.nn.sigmoid(_bdot(z, wg_ref[...]) + bg_ref[...])


def _s5_kernel(u_ref, wb_ref, wc_ref, coef_ref, pw_ref, dbl_ref, d_ref, wg_ref, bg_ref, s0_ref,
               y_ref, sfin_ref, s_scr, carry_scr):
    j = pl.program_id(1)
    t = u_ref.shape[0]

    @pl.when(j == 0)
    def _():
        carry_scr[...] = s0_ref[0]

    u = u_ref[...]
    bu = _bdot(u, wb_ref[...])
    b_re, b_im = bu[:, :S_LANES], bu[:, S_LANES:]
    c_re, c_im = coef_ref[0:1, :], coef_ref[1:2, :]
    s_scr[0] = c_re * b_re - c_im * b_im
    s_scr[1] = c_re * b_im + c_im * b_re
    row = lax.broadcasted_iota(jnp.int32, (SUBLANES, S_LANES), 0)

    def body(g, carry):
        cr, ci = carry
        off = pl.multiple_of(g * SUBLANES, SUBLANES)
        sr = s_scr[0, pl.ds(off, SUBLANES), :]
        si = s_scr[1, pl.ds(off, SUBLANES), :]
        for n, d in enumerate((1, 2, 4)):
            ar, ai = dbl_ref[0, n:n + 1, :], dbl_ref[1, n:n + 1, :]
            pr = jnp.where(row >= d, pltpu.roll(sr, d, 0), 0.0)
            pi = jnp.where(row >= d, pltpu.roll(si, d, 0), 0.0)
            sr, si = sr + ar * pr - ai * pi, si + ar * pi + ai * pr
        p_re, p_im = pw_ref[0], pw_ref[1]
        sr, si = sr + p_re * cr - p_im * ci, si + p_re * ci + p_im * cr
        s_scr[0, pl.ds(off, SUBLANES), :] = sr
        s_scr[1, pl.ds(off, SUBLANES), :] = si
        return sr[SUBLANES - 1:SUBLANES, :], si[SUBLANES - 1:SUBLANES, :]

    cr, ci = lax.fori_loop(0, t // SUBLANES, body, (carry_scr[0:1, :], carry_scr[1:2, :]), unroll=2)
    carry_scr[0:1, :] = cr
    carry_scr[1:2, :] = ci
    y_ref[...] = _s5_glu(s_scr[0], s_scr[1], u, wc_ref, d_ref, wg_ref, bg_ref)

    @pl.when(j == pl.num_programs(1) - 1)
    def _():
        sfin_ref[0] = carry_scr[...]


def _s5_prompt(u, sp, d, w_glu, b_glu, s0, nb, seq, t):
    nt = seq // t
    return pl.pallas_call(
        _s5_kernel,
        grid=(nb, nt),
        in_specs=[pl.BlockSpec((t, S_WIDTH), lambda b, j: (b * nt + j, 0)),
                  _full((S_WIDTH, 2 * S_LANES)), _full((2 * S_LANES, S_WIDTH)), _full((2, S_LANES)),
                  _full((2, SUBLANES, S_LANES)), _full((2, 3, S_LANES)), _full((1, S_WIDTH)),
                  _full((S_WIDTH, S_WIDTH)), _full((1, S_WIDTH)),
                  pl.BlockSpec((1, 2, S_LANES), lambda b, j: (b, 0, 0))],
        out_specs=[pl.BlockSpec((t, S_WIDTH), lambda b, j: (b * nt + j, 0)),
                   pl.BlockSpec((1, 2, S_LANES), lambda b, j: (b, 0, 0))],
        out_shape=[jax.ShapeDtypeStruct((nb * seq, S_WIDTH), F32),
                   jax.ShapeDtypeStruct((nb, 2, S_LANES), F32)],
        scratch_shapes=[pltpu.VMEM((2, t, S_LANES), F32), pltpu.VMEM((2, S_LANES), F32)],
        compiler_params=_cparams("parallel", "arbitrary"),
        name="s5_scan",
    )(u, sp["wb"], sp["wc"], sp["coef"], sp["pw"], sp["dbl"], d, w_glu, b_glu, s0)


def _s5_step_kernel(u_ref, wb_ref, wc_ref, coef_ref, abar_ref, d_ref, wg_ref, bg_ref, s0_ref,
                    y_ref, s_ref):
    u = u_ref[...]
    bu = _bdot(u, wb_ref[...])
    b_re, b_im = bu[:, :S_LANES], bu[:, S_LANES:]
    c_re, c_im = coef_ref[0:1, :], coef_ref[1:2, :]
    a_re, a_im = abar_ref[0:1, :], abar_ref[1:2, :]
    s0_re, s0_im = s0_ref[:, :S_LANES], s0_ref[:, S_LANES:]
    s_re = c_re * b_re - c_im * b_im + (a_re * s0_re - a_im * s0_im)
    s_im = c_re * b_im + c_im * b_re + (a_re * s0_im + a_im * s0_re)
    s_ref[:, :S_LANES] = s_re
    s_ref[:, S_LANES:] = s_im
    y_ref[...] = _s5_glu(s_re, s_im, u, wc_ref, d_ref, wg_ref, bg_ref)


def _s5_step(u, sp, d, w_glu, b_glu, s0):
    n = u.shape[0]
    return pl.pallas_call(
        _s5_step_kernel,
        out_shape=[jax.ShapeDtypeStruct((n, S_WIDTH), F32), jax.ShapeDtypeStruct((n, 2 * S_LANES), F32)],
        compiler_params=_cparams(),
        name="s5_step",
    )(u, sp["wb"], sp["wc"], sp["coef"], sp["abar"], d, w_glu, b_glu, s0)


def _rwkv_pre_kernel(decode, c_ref, prev_ref, sh0_ref, mu_ref, w0_ref, w2_ref, a0_ref, a2_ref, g2_ref,
                     kk_ref, ka_ref, rk_ref,
                     r_o, lw_o, k_o, v_o, a_o, b_o, g_o, bonus_o):
    c = c_ref[...]
    if decode:
        prev = sh0_ref[...]
    else:
        j = pl.program_id(1)
        last = jnp.where(j == 0, sh0_ref[0], prev_ref[SUBLANES - 1:SUBLANES, :])
        row = lax.broadcasted_iota(jnp.int32, c.shape, 0)
        prev = jnp.where(row == 0, last, pltpu.roll(c, 1, 0))
    cf = c + (prev - c) * mu_ref[...]
    r = cf[:, :R_WIDTH]
    k = cf[:, R_WIDTH:2 * R_WIDTH]
    v = cf[:, 2 * R_WIDTH:3 * R_WIDTH]
    lora = cf[:, 3 * R_WIDTH:]
    w_log = -jax.nn.softplus(-(w0_ref[...] + _bdot(jnp.tanh(lora), w2_ref[...]))) - 0.5
    a = jax.nn.sigmoid(a0_ref[...] + _bdot(lora, a2_ref[...]))
    g = _bdot(jax.nn.sigmoid(lora), g2_ref[...])
    kk = k * kk_ref[...]
    kk = kk * lax.rsqrt(jnp.maximum(_head_sum(kk * kk, R_HEAD), 1e-24))
    k = k * (1.0 + (a - 1.0) * ka_ref[...])
    r_o[...] = r
    lw_o[...] = -jnp.exp(w_log)
    k_o[...] = k
    v_o[...] = v
    a_o[...] = -kk
    b_o[...] = kk * a
    g_o[...] = g
    bonus_o[...] = _head_sum(r * k * rk_ref[...], R_HEAD) * v


def _rwkv_weights(mu, w0, w2, a0, a2, g2, k_k, k_a, r_k):
    z = lambda n: jnp.zeros((n, R_WIDTH), F32)
    w2p = jnp.concatenate([w2, z(A_LORA + G_LORA)], 0).astype(BF16)
    a2p = jnp.concatenate([z(W_LORA), a2, z(G_LORA)], 0).astype(BF16)
    g2p = jnp.concatenate([z(W_LORA + A_LORA), g2], 0).astype(BF16)
    row = lambda x: x.reshape(1, -1)
    return [row(mu), row(w0), w2p, row(a0), a2p, g2p, row(k_k), row(k_a), row(r_k)]


def _rwkv_pre(c, sh0, wts, nb, seq, t, decode):
    m = c.shape[0]
    outs = [jax.ShapeDtypeStruct((m, R_WIDTH), F32)] * 8
    wspecs = [_full(w.shape) for w in wts]
    if decode:
        return pl.pallas_call(
            functools.partial(_rwkv_pre_kernel, True),
            grid=(1,),
            in_specs=[_full((m, R_IN)), _full((SUBLANES, R_IN)), _full((m, R_IN))] + wspecs,
            out_specs=[_full((m, R_WIDTH))] * 8,
            out_shape=outs,
            compiler_params=_cparams("arbitrary"),
            name="rwkv_pre_step",
        )(c, c[:SUBLANES], sh0, *wts)
    nt = seq // t
    per8 = t // SUBLANES
    return pl.pallas_call(
        functools.partial(_rwkv_pre_kernel, False),
        grid=(nb, nt),
        in_specs=[pl.BlockSpec((t, R_IN), lambda b, j: (b * nt + j, 0)),
                  pl.BlockSpec((SUBLANES, R_IN), lambda b, j: (jnp.maximum((b * nt + j) * per8 - 1, 0), 0)),
                  pl.BlockSpec((1, 1, R_IN), lambda b, j: (b, 0, 0))] + wspecs,
        out_specs=[pl.BlockSpec((t, R_WIDTH), lambda b, j: (b * nt + j, 0))] * 8,
        out_shape=outs,
        compiler_params=_cparams("parallel", "parallel"),
        name="rwkv_pre",
    )(c, c, sh0.reshape(nb, 1, R_IN), *wts)


def _rwkv_chunk_kernel(r_ref, lw_ref, k_ref, v_ref, a_ref, b_ref, y1_ref, y2_ref, g_ref, f_ref):
    t = r_ref.shape[0]
    lw = lw_ref[...]
    row = lax.broadcasted_iota(jnp.int32, lw.shape, 0)
    cw = lw
    d = 1
    while d < t:
        cw = cw + jnp.where(row >= d, pltpu.roll(cw, d, 0), 0.0)
        d *= 2
    cw_end = cw[t - 1:t, :]
    e_neg = jnp.exp(-cw)
    e_rem = jnp.exp(cw_end - cw)
    a_t = a_ref[...] * jnp.exp(cw - lw)
    r_t = r_ref[...] * jnp.exp(cw)
    b_t = b_ref[...] * e_neg
    k_t = k_ref[...] * e_neg
    b_e = b_ref[...] * e_rem
    k_e = k_ref[...] * e_rem
    e_end = jnp.exp(cw_end)
    v = v_ref[...]
    ri = lax.broadcasted_iota(jnp.int32, (t, t), 0)
    ci = lax.broadcasted_iota(jnp.int32, (t, t), 1)
    hi = lax.broadcasted_iota(jnp.int32, (R_HEAD, R_HEAD), 0)
    hj = lax.broadcasted_iota(jnp.int32, (R_HEAD, R_HEAD), 1)
    for h in range(R_HEADS):
        sl = slice(h * R_HEAD, (h + 1) * R_HEAD)
        ah, rh, bh, kh, vh = a_t[:, sl], r_t[:, sl], b_t[:, sl], k_t[:, sl], v[:, sl]
        a_ab = jnp.where(ri > ci, _dot3(ah, bh, _NT), 0.0)
        a_ak = jnp.where(ri > ci, _dot3(ah, kh, _NT), 0.0)
        a_rb = jnp.where(ri >= ci, _dot3(rh, bh, _NT), 0.0)
        a_rk = jnp.where(ri >= ci, _dot3(rh, kh, _NT), 0.0)
        inv = jnp.where(ri == ci, 1.0, 0.0) + a_ab
        apow = a_ab
        n = 2
        while n < t:
            apow = _dot3(apow, apow)
            inv = inv + _dot3(apow, inv)
            n *= 2
        w1 = _dot3(inv, ah)
        w2 = _dot3(inv, _dot3(a_ak, vh))
        y1_ref[0, h] = rh + _dot3(a_rb, w1)
        y2_ref[0, h] = _dot3(a_rb, w2) + _dot3(a_rk, vh)
        be_t = b_e[:, sl].T
        ke_t = k_e[:, sl].T
        g_ref[0, h] = jnp.where(hi == hj, e_end[:, sl], 0.0) + _dot3(be_t, w1)
        f_ref[0, h] = _dot3(be_t, w2) + _dot3(ke_t, vh)


def _rwkv_chunks(r, lw, k, v, a, b, t):
    m = r.shape[0]
    nc = m // t
    spec = pl.BlockSpec((t, R_WIDTH), lambda i: (i, 0))
    yspec = pl.BlockSpec((1, R_HEADS, t, R_HEAD), lambda i: (i, 0, 0, 0))
    gspec = pl.BlockSpec((1, R_HEADS, R_HEAD, R_HEAD), lambda i: (i, 0, 0, 0))
    return pl.pallas_call(
        _rwkv_chunk_kernel,
        grid=(nc,),
        in_specs=[spec] * 6,
        out_specs=[yspec, yspec, gspec, gspec],
        out_shape=[jax.ShapeDtypeStruct((nc, R_HEADS, t, R_HEAD), F32)] * 2
        + [jax.ShapeDtypeStruct((nc, R_HEADS, R_HEAD, R_HEAD), F32)] * 2,
        compiler_params=_cparams("parallel"),
        name="rwkv_chunk",
    )(r, lw, k, v, a, b)


def _rwkv_post(y, bonus, g, lng, lnb):
    m = _head_sum(y, R_HEAD) * (1.0 / R_HEAD)
    yc = y - m
    var = _head_sum(yc * yc, R_HEAD) * (1.0 / R_HEAD)
    return (yc * lax.rsqrt(var + GN_EPS) * lng + lnb + bonus) * g


def _rwkv_scan_kernel(y1_ref, y2_ref, g_ref, f_ref, bonus_ref, gate_ref, lng_ref, lnb_ref, h0_ref,
                      y_ref, hfin_ref, h_scr, y_scr):
    j = pl.program_id(1)

    @pl.when(j == 0)
    def _():
        h_scr[...] = h0_ref[0]

    for h in range(R_HEADS):
        state = h_scr[h]
        y_scr[:, h * R_HEAD:(h + 1) * R_HEAD] = _dot3(y1_ref[0, h], state) + y2_ref[0, h]
        h_scr[h] = _dot3(g_ref[0, h], state) + f_ref[0, h]
    y_ref[...] = _rwkv_post(y_scr[...], bonus_ref[...], gate_ref[...], lng_ref[...], lnb_ref[...])

    @pl.when(j == pl.num_programs(1) - 1)
    def _():
        hfin_ref[0] = h_scr[...]


def _rwkv_scan(y1, y2, g, f, bonus, gate, lng, lnb, h0, nb, seq, t):
    nc = seq // t
    yspec = pl.BlockSpec((1, R_HEADS, t, R_HEAD), lambda b, j: (b * nc + j, 0, 0, 0))
    gspec = pl.BlockSpec((1, R_HEADS, R_HEAD, R_HEAD), lambda b, j: (b * nc + j, 0, 0, 0))
    row = pl.BlockSpec((t, R_WIDTH), lambda b, j: (b * nc + j, 0))
    hspec = pl.BlockSpec((1, R_HEADS, R_HEAD, R_HEAD), lambda b, j: (b, 0, 0, 0))
    return pl.pallas_call(
        _rwkv_scan_kernel,
        grid=(nb, nc),
        in_specs=[yspec, yspec, gspec, gspec, row, row, _full((1, R_WIDTH)), _full((1, R_WIDTH)), hspec],
        out_specs=[row, hspec],
        out_shape=[jax.ShapeDtypeStruct((nb * seq, R_WIDTH), F32),
                   jax.ShapeDtypeStruct((nb, R_HEADS, R_HEAD, R_HEAD), F32)],
        scratch_shapes=[pltpu.VMEM((R_HEADS, R_HEAD, R_HEAD), F32), pltpu.VMEM((t, R_WIDTH), F32)],
        compiler_params=_cparams("parallel", "arbitrary"),
        name="rwkv_scan",
    )(y1, y2, g, f, bonus, gate, lng, lnb, h0)


RWKV_STEP_ROWS = 8


def _rwkv_step_kernel(r_ref, lw_ref, k_ref, v_ref, a_ref, b_ref, s_ref, y_ref, so_ref):
    ii = lax.broadcasted_iota(jnp.int32, (R_HEAD, R_HEAD), 0)
    jj = lax.broadcasted_iota(jnp.int32, (R_HEAD, R_HEAD), 1)
    eye = ii == jj
    for n in range(RWKV_STEP_ROWS):
        one = lambda ref: ref[n:n + 1, :]
        s = s_ref[n]
        sa = jnp.sum(s * one(a_ref), -1, keepdims=True)
        v_col = jnp.sum(jnp.where(eye, one(v_ref), 0.0), -1, keepdims=True)
        s_new = s * jnp.exp(one(lw_ref)) + sa * one(b_ref) + v_col * one(k_ref)
        y_col = jnp.sum(s_new * one(r_ref), -1, keepdims=True)
        so_ref[n] = s_new
        y_ref[n:n + 1, :] = jnp.sum(jnp.where(eye, y_col, 0.0), 0, keepdims=True)


def _rwkv_step(vecs, s0):
    n = s0.shape[0]
    vspec = pl.BlockSpec((RWKV_STEP_ROWS, R_HEAD), lambda i: (i, 0))
    sspec = pl.BlockSpec((RWKV_STEP_ROWS, R_HEAD, R_HEAD), lambda i: (i, 0, 0))
    return pl.pallas_call(
        _rwkv_step_kernel,
        grid=(n // RWKV_STEP_ROWS,),
        in_specs=[vspec] * 6 + [sspec],
        out_specs=[vspec, sspec],
        out_shape=[jax.ShapeDtypeStruct((n, R_HEAD), F32), jax.ShapeDtypeStruct((n, R_HEAD, R_HEAD), F32)],
        compiler_params=_cparams("parallel"),
        name="rwkv_step",
    )(*vecs, s0)


def _rwkv_post_kernel(y_ref, bonus_ref, gate_ref, lng_ref, lnb_ref, o_ref):
    o_ref[...] = _rwkv_post(y_ref[...], bonus_ref[...], gate_ref[...], lng_ref[...], lnb_ref[...])


def _rwkv_post_call(y, bonus, gate, lng, lnb):
    return pl.pallas_call(
        _rwkv_post_kernel,
        out_shape=jax.ShapeDtypeStruct(y.shape, F32),
        compiler_params=_cparams(),
        name="rwkv_post_step",
    )(y, bonus, gate, lng, lnb)


def _topk_mask(gate, col, n_blocks):
    sel = jnp.zeros(gate.shape, jnp.bool_)
    for _ in range(MOBA_TOPK):
        m = jnp.max(gate, -1, keepdims=True)
        idx = jnp.min(jnp.where(gate == m, col, n_blocks), -1, keepdims=True)
        pick = (col == idx) & (m > NEG_INF)
        sel = sel | pick
        gate = jnp.where(pick, NEG_INF, gate)
    return sel


def _moba_kernel(q_ref, k_ref, v_ref, o_ref, km_scr):
    qi = pl.program_id(2)
    nblk = km_scr.shape[0]
    rows = A_GROUP * MOBA_BLOCK

    @pl.when(qi == 0)
    def _():
        km_scr[...] = jnp.mean(k_ref[0, 0].reshape(nblk, MOBA_BLOCK, A_HEAD), axis=1)

    qb = q_ref[0, 0].reshape(rows, A_HEAD).astype(BF16)
    col = lax.broadcasted_iota(jnp.int32, (rows, nblk), 1)
    gate = jnp.where(col < qi, _bdot(qb, km_scr[...], _NT), NEG_INF)
    bias = jnp.where(_topk_mask(gate, col, nblk), 0.0, NEG_INF)

    def block(n):
        off = pl.multiple_of(n * MOBA_BLOCK, MOBA_BLOCK)
        kb = k_ref[0, 0, pl.ds(off, MOBA_BLOCK), :]
        vb = v_ref[0, 0, pl.ds(off, MOBA_BLOCK), :]
        return _bdot(qb, kb, _NT) * ATT_SCALE, vb.astype(BF16)

    s, vb = block(qi)
    rq = lax.broadcasted_iota(jnp.int32, s.shape, 0) % MOBA_BLOCK
    ck = lax.broadcasted_iota(jnp.int32, s.shape, 1)
    s = jnp.where(ck <= rq, s, NEG_INF)
    m0 = jnp.max(s, -1, keepdims=True)
    p = jnp.exp(s - m0)
    l0 = jnp.sum(p, -1, keepdims=True)
    acc0 = _bdot(p, vb)

    def body(n, carry):
        m, l, acc = carry
        s, vb = block(n)
        s = s + jnp.max(jnp.where(col == n, bias, NEG_INF), -1, keepdims=True)
        m_new = jnp.maximum(m, jnp.max(s, -1, keepdims=True))
        alpha = jnp.exp(m - m_new)
        p = jnp.exp(s - m_new)
        return m_new, alpha * l + jnp.sum(p, -1, keepdims=True), alpha * acc + _bdot(p, vb)

    _, l, acc = lax.fori_loop(0, qi, body, (m0, l0, acc0))
    o_ref[0, 0] = (acc / l).reshape(A_GROUP, MOBA_BLOCK, A_HEAD)


def _moba_prompt(q, k, v, nb, seq):
    nblk = seq // MOBA_BLOCK
    qh = q.reshape(nb, seq, A_KV_HEADS, A_GROUP, A_HEAD).transpose(0, 2, 3, 1, 4)
    kh = k.reshape(nb, seq, A_KV_HEADS, A_HEAD).transpose(0, 2, 1, 3)
    vh = v.reshape(nb, seq, A_KV_HEADS, A_HEAD).transpose(0, 2, 1, 3)
    qspec = pl.BlockSpec((1, 1, A_GROUP, MOBA_BLOCK, A_HEAD), lambda b, h, i: (b, h, 0, i, 0))
    kspec = pl.BlockSpec((1, 1, seq, A_HEAD), lambda b, h, i: (b, h, 0, 0))
    o = pl.pallas_call(
        _moba_kernel,
        grid=(nb, A_KV_HEADS, nblk),
        in_specs=[qspec, kspec, kspec],
        out_specs=qspec,
        out_shape=jax.ShapeDtypeStruct(qh.shape, F32),
        scratch_shapes=[pltpu.VMEM((nblk, A_HEAD), F32)],
        compiler_params=_cparams("parallel", "parallel", "arbitrary"),
        name="moba_prompt",
    )(qh, kh, vh)
    return o.transpose(0, 3, 1, 2, 4).reshape(nb * seq, A_QW)


def _kmeans_kernel(n_in, pt_ref, *refs):
    pages, o_ref = refs[:n_in], refs[n_in]
    per_blk = MOBA_BLOCK // PAGE_SIZE
    for i in range(n_in // per_blk):
        tot = sum(jnp.sum(pages[per_blk * i + p][0], axis=0, keepdims=True) for p in range(per_blk))
        o_ref[0, i:i + 1, :] = tot * (1.0 / MOBA_BLOCK)


def _decode_kmeans(cache, page_table, layer, n_pool):
    nb, n_pages = page_table.shape
    n_in = min(16, n_pages)
    steps = n_pages // n_in
    per_blk = MOBA_BLOCK // PAGE_SIZE
    pages = cache.reshape(-1, PAGE_SIZE, A_KVW)

    def page_spec(i):
        return pl.BlockSpec((1, PAGE_SIZE, A_KVW),
                            lambda b, j, pt: (layer * n_pool + pt[b * n_pages + j * n_in + i], 0, 0))

    return pl.pallas_call(
        functools.partial(_kmeans_kernel, n_in),
        grid_spec=pltpu.PrefetchScalarGridSpec(
            num_scalar_prefetch=1,
            grid=(nb, steps),
            in_specs=[page_spec(i) for i in range(n_in)],
            out_specs=pl.BlockSpec((1, n_in // per_blk, A_KVW), lambda b, j, pt: (b, j, 0)),
        ),
        out_shape=jax.ShapeDtypeStruct((nb, n_pages // per_blk, A_KVW), F32),
        compiler_params=_cparams("parallel", "arbitrary"),
        name="decode_kmeans",
    )(page_table.reshape(-1), *([pages] * n_in))


def _decode_select_kernel(q_ref, km_ref, o_ref):
    nblk = km_ref.shape[1]
    gate = _bdot(q_ref[0], km_ref[0], _NT)
    col = lax.broadcasted_iota(jnp.int32, gate.shape, 1)
    lane = lax.broadcasted_iota(jnp.int32, (A_HEADS, LANES), 1)
    out = jnp.zeros((A_HEADS, LANES), jnp.int32)
    for r in range(MOBA_TOPK):
        m = jnp.max(gate, -1, keepdims=True)
        idx = jnp.min(jnp.where(gate == m, col, nblk), -1, keepdims=True)
        out = jnp.where(lane == r, idx, out)
        gate = jnp.where(col == idx, NEG_INF, gate)
    o_ref[0] = out


def _decode_select(q_exp, kmeans):
    nb, nblk, _ = kmeans.shape
    return pl.pallas_call(
        _decode_select_kernel,
        grid=(nb,),
        in_specs=[pl.BlockSpec((1, A_HEADS, A_KVW), lambda b: (b, 0, 0)),
                  pl.BlockSpec((1, nblk, A_KVW), lambda b: (b, 0, 0))],
        out_specs=pl.BlockSpec((1, A_HEADS, LANES), lambda b: (b, 0, 0)),
        out_shape=jax.ShapeDtypeStruct((nb, A_HEADS, LANES), jnp.int32),
        compiler_params=_cparams("parallel"),
        name="decode_select",
    )(q_exp, kmeans)


DEC_PAGES = MOBA_TOPK * (MOBA_BLOCK // PAGE_SIZE)


def _decode_attn_kernel(pg_ref, q_ref, kn_ref, vn_ref, *refs):
    k_pages, v_pages, o_ref = refs[:DEC_PAGES], refs[DEC_PAGES:2 * DEC_PAGES], refs[2 * DEC_PAGES]
    h = pl.program_id(1)
    q = q_ref[0, pl.ds(h, 1), :].astype(BF16)
    rb = lambda x: x.astype(BF16).astype(F32)
    s_self = jnp.sum(rb(q) * rb(kn_ref[0]), -1, keepdims=True) * ATT_SCALE
    scores = [_bdot(q, kp[0], _NT) * ATT_SCALE for kp in k_pages]
    m = s_self
    for s in scores:
        m = jnp.maximum(m, jnp.max(s, -1, keepdims=True))
    p_self = jnp.exp(s_self - m)
    l = p_self
    acc = rb(p_self) * rb(vn_ref[0])
    for s, vp in zip(scores, v_pages):
        p = jnp.exp(s - m)
        l = l + jnp.sum(p, -1, keepdims=True)
        acc = acc + _bdot(p, vp[0])
    o_ref[0, pl.ds(h, 1), :] = acc / l


def _decode_attn(q_exp, k_new, v_new, cache_k, cache_v, page_ids, layer, n_pool):
    nb = q_exp.shape[0]
    kp = cache_k.reshape(-1, PAGE_SIZE, A_KVW)
    vp = cache_v.reshape(-1, PAGE_SIZE, A_KVW)

    def page_spec(i):
        return pl.BlockSpec(
            (1, PAGE_SIZE, A_KVW),
            lambda b, h, pg: (layer * n_pool + pg[(b * A_HEADS + h) * DEC_PAGES + i], 0, 0))

    bspec = pl.BlockSpec((1, A_HEADS, A_KVW), lambda b, h, pg: (b, 0, 0))
    nspec = pl.BlockSpec((1, 1, A_KVW), lambda b, h, pg: (b, 0, 0))
    return pl.pallas_call(
        _decode_attn_kernel,
        grid_spec=pltpu.PrefetchScalarGridSpec(
            num_scalar_prefetch=1,
            grid=(nb, A_HEADS),
            in_specs=[bspec, nspec, nspec] + [page_spec(i) for i in range(DEC_PAGES)] * 2,
            out_specs=bspec,
        ),
        out_shape=jax.ShapeDtypeStruct((nb, A_HEADS, A_KVW), F32),
        compiler_params=_cparams("parallel", "arbitrary"),
        name="decode_attn",
    )(page_ids.reshape(-1), q_exp, k_new.reshape(nb, 1, A_KVW), v_new.reshape(nb, 1, A_KVW),
      *([kp] * DEC_PAGES), *([vp] * DEC_PAGES))


def _moba_decode(q, k_new, v_new, cache_k, cache_v, page_table, layer):
    nb = q.shape[0]
    n_pool = cache_k.shape[1]
    onehot = (jnp.arange(A_KV_HEADS)[None, :] == (jnp.arange(A_HEADS) // A_GROUP)[:, None]).astype(F32)
    q_exp = (q.reshape(nb, A_HEADS, 1, A_HEAD) * onehot[None, :, :, None]).reshape(nb, A_HEADS, A_KVW)
    kmeans = _decode_kmeans(cache_k, page_table, layer, n_pool)
    blk = _decode_select(q_exp, kmeans)[:, :, :MOBA_TOPK]
    per_blk = MOBA_BLOCK // PAGE_SIZE
    pages_of = (blk[..., None] * per_blk + jnp.arange(per_blk)).reshape(nb, A_HEADS * DEC_PAGES)
    page_ids = jnp.take_along_axis(page_table, pages_of, axis=1)
    o = _decode_attn(q_exp, k_new, v_new, cache_k, cache_v, page_ids, layer, n_pool)
    o = o.reshape(nb, A_HEADS, A_KV_HEADS, A_HEAD)
    return jnp.einsum("bhkd,hk->bhd", o, onehot).reshape(nb, A_QW)


def _merge_kernel(x_ref, ys_ref, yr_ref, ya_ref, wg_ref, ps_ref, pr_ref, pa_ref, wo_ref, g_ref, b_ref, o_ref):
    x = x_ref[...]
    xb = x.astype(BF16)
    merged = None
    for i, (y_ref, p_ref) in enumerate(((ys_ref, ps_ref), (yr_ref, pr_ref), (ya_ref, pa_ref))):
        gate = jax.nn.sigmoid(_bdot(xb, wg_ref[:, i * D_MODEL:(i + 1) * D_MODEL]))
        term = gate * _bdot(y_ref[...], p_ref[...])
        merged = term if merged is None else merged + term
    o_ref[...] = _ln(ALPHA * x + _bdot(merged, wo_ref[...]), g_ref[...], b_ref[...])


def _merge(x, ys, yr, ya, wg, ps, pr, pa, wo, g, b, tm):
    m = x.shape[0]
    row = lambda width: pl.BlockSpec((tm, width), lambda i: (i, 0))
    return pl.pallas_call(
        _merge_kernel,
        grid=(m // tm,),
        in_specs=[row(D_MODEL), row(S_WIDTH), row(R_WIDTH), row(A_QW),
                  _full(wg.shape), _full(ps.shape), _full(pr.shape), _full(pa.shape), _full(wo.shape),
                  _full((1, D_MODEL)), _full((1, D_MODEL))],
        out_specs=row(D_MODEL),
        out_shape=jax.ShapeDtypeStruct((m, D_MODEL), F32),
        compiler_params=_cparams("parallel"),
        name="merge",
    )(x, ys, yr, ya, wg, ps, pr, pa, wo, g, b)


FFN_COLS = D_FF // 2


def _ffn_kernel(tiles_per_seq, x_ref, halo_ref, wup_ref, cw_ref, cb_ref, wdn_ref, g_ref, b_ref, o_ref):
    i = pl.program_id(0)
    x = x_ref[...]
    tm = x.shape[0]
    xe = jnp.concatenate([halo_ref[...], x], axis=0).astype(BF16)
    row = lax.broadcasted_iota(jnp.int32, (tm + SUBLANES, 1), 0)
    keep = jnp.logical_or(row >= SUBLANES, i % tiles_per_seq != 0)
    acc = jnp.zeros((tm, D_MODEL), F32)
    for c in range(D_FF // FFN_COLS):
        halves = []
        for off in (c * FFN_COLS, D_FF + c * FFN_COLS):
            up = jnp.where(keep, _bdot(xe, wup_ref[:, off:off + FFN_COLS]), 0.0)
            cv = cb_ref[:, off:off + FFN_COLS]
            for j in range(CONV_W):
                lo = SUBLANES - (CONV_W - 1) + j
                cv = cv + cw_ref[j:j + 1, off:off + FFN_COLS] * up[lo:lo + tm, :]
            halves.append(cv)
        hmid = jax.nn.gelu(halves[0]) * halves[1]
        acc = acc + _bdot(hmid, wdn_ref[c * FFN_COLS:(c + 1) * FFN_COLS, :])
    o_ref[...] = _ln(ALPHA * x + acc, g_ref[...], b_ref[...])


def _ffn(x, wup, cw, cb, wdn, g, b, seq, tm):
    m = x.shape[0]
    per8 = tm // SUBLANES
    return pl.pallas_call(
        functools.partial(_ffn_kernel, seq // tm),
        grid=(m // tm,),
        in_specs=[pl.BlockSpec((tm, D_MODEL), lambda i: (i, 0)),
                  pl.BlockSpec((SUBLANES, D_MODEL), lambda i: (jnp.maximum(i * per8 - 1, 0), 0)),
                  _full(wup.shape), _full(cw.shape), _full(cb.shape), _full(wdn.shape),
                  _full((1, D_MODEL)), _full((1, D_MODEL))],
        out_specs=pl.BlockSpec((tm, D_MODEL), lambda i: (i, 0)),
        out_shape=jax.ShapeDtypeStruct((m, D_MODEL), F32),
        compiler_params=_cparams("parallel"),
        name="ffn",
    )(x, x, wup, cw, cb, wdn, g, b)


def _ffn_step_kernel(x_ref, c0a_ref, c0b_ref, wup_ref, cw_ref, cb_ref, wdn_ref, g_ref, b_ref, o_ref, up_ref):
    x = x_ref[...]
    up = _bdot(x, wup_ref[...])
    up_ref[...] = up
    cv = cb_ref[...] + cw_ref[0:1, :] * c0a_ref[...] + cw_ref[1:2, :] * c0b_ref[...] + cw_ref[2:3, :] * up
    hmid = jax.nn.gelu(cv[:, :D_FF]) * cv[:, D_FF:]
    o_ref[...] = _ln(ALPHA * x + _bdot(hmid, wdn_ref[...]), g_ref[...], b_ref[...])


def _ffn_step(x, c0a, c0b, wup, cw, cb, wdn, g, b):
    n = x.shape[0]
    return pl.pallas_call(
        _ffn_step_kernel,
        out_shape=[jax.ShapeDtypeStruct((n, D_MODEL), F32), jax.ShapeDtypeStruct((n, 2 * D_FF), F32)],
        compiler_params=_cparams(),
        name="ffn_step",
    )(x, c0a, c0b, wup, cw, cb, wdn, g, b)


def _up_rows_kernel(x_ref, w_ref, o_ref):
    o_ref[...] = _bdot(x_ref[...], w_ref[...])


def _up_rows(x, w):
    return pl.pallas_call(
        _up_rows_kernel,
        out_shape=jax.ShapeDtypeStruct((x.shape[0], w.shape[1]), F32),
        compiler_params=_cparams(),
        name="ffn_up_tail",
    )(x, w)


def _layer_params(l, P):
    bf = lambda x: x.astype(BF16)
    row = lambda x: x.reshape(1, -1)
    w_in = P["w_in"][l]
    return dict(
        w_gate=bf(w_in[:, :O_SSM]), w_rest=bf(w_in[:, O_SSM:]),
        s5=_s5_params(P["ssm_a_re"][l], P["ssm_a_im"][l], P["ssm_log_dt"][l], P["ssm_b_re"][l],
                      P["ssm_b_im"][l], P["ssm_c_re"][l], P["ssm_c_im"][l]),
        ssm_d=row(P["ssm_d"][l]), w_glu=bf(P["ssm_w_glu"][l]), b_glu=row(P["ssm_b_glu"][l]),
        rwkv=_rwkv_weights(P["rwkv_mu"][l], P["rwkv_w0"][l], P["rwkv_w2"][l], P["rwkv_a0"][l],
                           P["rwkv_a2"][l], P["rwkv_g2"][l], P["rwkv_k_k"][l], P["rwkv_k_a"][l],
                           P["rwkv_r_k"][l]),
        lnx_g=row(P["rwkv_lnx_g"][l]), lnx_b=row(P["rwkv_lnx_b"][l]),
        proj_ssm=bf(P["proj_ssm"][l]), proj_rwkv=bf(P["proj_rwkv"][l]), proj_attn=bf(P["proj_attn"][l]),
        w_o=bf(P["w_o"][l]), ln1_g=row(P["ln1_g"][l]), ln1_b=row(P["ln1_b"][l]),
        w_up=bf(P["ffn_w_up"][l]), conv_w=P["ffn_conv_w"][l], conv_b=row(P["ffn_conv_b"][l]),
        w_down=bf(P["ffn_w_down"][l]), ln2_g=row(P["ln2_g"][l]), ln2_b=row(P["ln2_b"][l]),
    )


def _pick_tile(n, pref):
    t = min(pref, n)
    while n % t:
        t //= 2
    return t


def _prompt_trunk(x_prompt, P, layers):
    nb, seq, _ = x_prompt.shape
    m = nb * seq
    x = x_prompt.reshape(m, D_MODEL)
    tm = _pick_tile(seq, 512)
    rope = _rope_tables(jnp.arange(seq, dtype=jnp.int32))
    ln_g, ln_b = P["ln_in_g"].reshape(1, -1), P["ln_in_b"].reshape(1, -1)
    outs = []
    for l, lp in enumerate(layers):
        res = _in_proj(x, ln_g, ln_b, lp["w_rest"], rope, tm, seq // tm, pre_ln=(l == 0))
        u, c, q, k, v = res[:5]
        if l == 0:
            x = res[5]
        y_s, s_fin = _s5_prompt(u, lp["s5"], lp["ssm_d"], lp["w_glu"], lp["b_glu"],
                                jnp.zeros((nb, 2, S_LANES), F32), nb, seq, _pick_tile(seq, 256))
        r, lw, k2, vv, a, b, gate, bonus = _rwkv_pre(c, jnp.zeros((nb, R_IN), F32), lp["rwkv"], nb, seq,
                                                     _pick_tile(seq, 256), decode=False)
        y1, y2, g, f = _rwkv_chunks(r, lw, k2, vv, a, b, RWKV_CHUNK)
        y_r, h_fin = _rwkv_scan(y1, y2, g, f, bonus, gate, lp["lnx_g"], lp["lnx_b"],
                                jnp.zeros((nb, R_HEADS, R_HEAD, R_HEAD), F32), nb, seq, RWKV_CHUNK)
        y_a = _moba_prompt(q, k, v, nb, seq)
        x = _merge(x, y_s, y_r, y_a, lp["w_gate"], lp["proj_ssm"], lp["proj_rwkv"], lp["proj_attn"],
                   lp["w_o"], lp["ln1_g"], lp["ln1_b"], _pick_tile(seq, 256))
        tail = x.reshape(nb, seq, D_MODEL)[:, seq - (CONV_W - 1):].reshape(nb * (CONV_W - 1), D_MODEL)
        conv = _up_rows(tail, lp["w_up"]).reshape(nb, CONV_W - 1, 2 * D_FF)
        x = _ffn(x, lp["w_up"], lp["conv_w"], lp["conv_b"], lp["w_down"], lp["ln2_g"], lp["ln2_b"],
                 seq, _pick_tile(seq, 256))
        outs.append(dict(
            k=k.reshape(nb, seq, A_KV_HEADS, A_HEAD), v=v.reshape(nb, seq, A_KV_HEADS, A_HEAD),
            s_re=s_fin[:, 0].reshape(nb, S_GROUPS, S_STATE), s_im=s_fin[:, 1].reshape(nb, S_GROUPS, S_STATE),
            rwkv=jnp.swapaxes(h_fin, -1, -2), shift=c.reshape(nb, seq, R_IN)[:, -1], conv=conv))
    return x.reshape(nb, seq, D_MODEL), outs


def _sample_trunk(x_sample, P, layers, cache_k, cache_v, page_table, st_re, st_im, st_rwkv, st_shift, st_conv):
    nb = x_sample.shape[0]
    x = x_sample.reshape(nb, D_MODEL)
    past_len = page_table.shape[1] * PAGE_SIZE
    rope = _rope_tables(jnp.full((nb,), past_len, jnp.int32))
    ln_g, ln_b = P["ln_in_g"].reshape(1, -1), P["ln_in_b"].reshape(1, -1)
    outs = []
    for l, lp in enumerate(layers):
        res = _in_proj(x, ln_g, ln_b, lp["w_rest"], rope, nb, 1, pre_ln=(l == 0))
        u, c, q, k, v = res[:5]
        if l == 0:
            x = res[5]
        s0 = jnp.concatenate([st_re[l].reshape(nb, S_LANES), st_im[l].reshape(nb, S_LANES)], axis=1)
        y_s, s_new = _s5_step(u, lp["s5"], lp["ssm_d"], lp["w_glu"], lp["b_glu"], s0)
        r, lw, k2, vv, a, b, gate, bonus = _rwkv_pre(c, st_shift[l], lp["rwkv"], nb, 1, nb, decode=True)
        per_head = lambda t: t.reshape(nb * R_HEADS, R_HEAD)
        y_rows, s_rwkv = _rwkv_step([per_head(t) for t in (r, lw, k2, vv, a, b)],
                                    st_rwkv[l].reshape(nb * R_HEADS, R_HEAD, R_HEAD))
        y_r = _rwkv_post_call(y_rows.reshape(nb, R_WIDTH), bonus, gate, lp["lnx_g"], lp["lnx_b"])
        y_a = _moba_decode(q, k, v, cache_k, cache_v, page_table, l)
        x = _merge(x, y_s, y_r, y_a, lp["w_gate"], lp["proj_ssm"], lp["proj_rwkv"], lp["proj_attn"],
                   lp["w_o"], lp["ln1_g"], lp["ln1_b"], nb)
        x, up = _ffn_step(x, st_conv[l][:, 0], st_conv[l][:, 1], lp["w_up"], lp["conv_w"], lp["conv_b"],
                          lp["w_down"], lp["ln2_g"], lp["ln2_b"])
        outs.append(dict(
            k=k.reshape(nb, 1, A_KV_HEADS, A_HEAD), v=v.reshape(nb, 1, A_KV_HEADS, A_HEAD),
            s_re=s_new[:, :S_LANES].reshape(nb, S_GROUPS, S_STATE),
            s_im=s_new[:, S_LANES:].reshape(nb, S_GROUPS, S_STATE),
            rwkv=s_rwkv.reshape(nb, R_HEADS, R_HEAD, R_HEAD), shift=c,
            conv=jnp.stack([st_conv[l][:, 1], up], axis=1)))
    return x.reshape(nb, 1, D_MODEL), outs


def kernel(x_prompt, x_sample, cache_k, cache_v, page_table, state_ssm_re, state_ssm_im, state_rwkv, state_rwkv_shift, state_conv, ln_in_g, ln_in_b, w_in, ssm_a_re, ssm_a_im, ssm_log_dt, ssm_b_re, ssm_b_im, ssm_c_re, ssm_c_im, ssm_d, ssm_w_glu, ssm_b_glu, rwkv_mu, rwkv_w0, rwkv_w2, rwkv_a0, rwkv_a2, rwkv_g2, rwkv_k_k, rwkv_k_a, rwkv_r_k, rwkv_lnx_g, rwkv_lnx_b, proj_ssm, proj_rwkv, proj_attn, w_o, ln1_g, ln1_b, ffn_w_up, ffn_conv_w, ffn_conv_b, ffn_w_down, ln2_g, ln2_b):
    P = dict(ln_in_g=ln_in_g, ln_in_b=ln_in_b, w_in=w_in,
             ssm_a_re=ssm_a_re, ssm_a_im=ssm_a_im, ssm_log_dt=ssm_log_dt,
             ssm_b_re=ssm_b_re, ssm_b_im=ssm_b_im, ssm_c_re=ssm_c_re, ssm_c_im=ssm_c_im,
             ssm_d=ssm_d, ssm_w_glu=ssm_w_glu, ssm_b_glu=ssm_b_glu,
             rwkv_mu=rwkv_mu, rwkv_w0=rwkv_w0, rwkv_w2=rwkv_w2, rwkv_a0=rwkv_a0, rwkv_a2=rwkv_a2,
             rwkv_g2=rwkv_g2, rwkv_k_k=rwkv_k_k, rwkv_k_a=rwkv_k_a, rwkv_r_k=rwkv_r_k,
             rwkv_lnx_g=rwkv_lnx_g, rwkv_lnx_b=rwkv_lnx_b,
             proj_ssm=proj_ssm, proj_rwkv=proj_rwkv, proj_attn=proj_attn, w_o=w_o,
             ln1_g=ln1_g, ln1_b=ln1_b, ffn_w_up=ffn_w_up, ffn_conv_w=ffn_conv_w,
             ffn_conv_b=ffn_conv_b, ffn_w_down=ffn_w_down, ln2_g=ln2_g, ln2_b=ln2_b)
    layers = [_layer_params(l, P) for l in range(w_in.shape[0])]
    y_p, op = _prompt_trunk(x_prompt, P, layers)
    y_s, os_ = _sample_trunk(x_sample, P, layers, cache_k, cache_v, page_table, state_ssm_re, state_ssm_im,
                             state_rwkv, state_rwkv_shift, state_conv)
    st = lambda outs, key: jnp.stack([o[key] for o in outs])
    return (y_p, y_s, st(op, "k"), st(op, "v"), st(os_, "k"), st(os_, "v"),
            st(op, "s_re"), st(op, "s_im"), st(os_, "s_re"), st(os_, "s_im"),
            st(op, "rwkv"), st(os_, "rwkv"), st(op, "shift"), st(os_, "shift"),
            st(op, "conv"), st(os_, "conv"))
```

```python
import functools
import math

import jax
import jax.numpy as jnp
from jax import lax
from jax.experimental import pallas as pl
from jax.experimental.pallas import tpu as pltpu

F32 = jnp.float32
BF16 = jnp.bfloat16

D_MODEL = 1024
PAGE_SIZE = 128
S_GROUP = 16
S_GROUPS = 16
S_STATE = 64
S_WIDTH = S_GROUPS * S_GROUP
S_LANES = S_GROUPS * S_STATE
R_HEAD = 64
R_HEADS = 4
R_WIDTH = R_HEADS * R_HEAD
W_LORA = 32
A_LORA = 32
G_LORA = 64
N_LORA = W_LORA + A_LORA + G_LORA
R_IN = 3 * R_WIDTH + N_LORA
GN_EPS = 64e-5
A_HEAD = 64
A_HEADS = 8
A_KV_HEADS = 4
A_GROUP = A_HEADS // A_KV_HEADS
A_QW = A_HEADS * A_HEAD
A_KVW = A_KV_HEADS * A_HEAD
ROT_DIM = A_HEAD // 4
ROPE_THETA = 500000.0
MOBA_BLOCK = 256
MOBA_TOPK = 3
N_BRANCH = 3
O_SSM = N_BRANCH * D_MODEL
N_REST = S_WIDTH + R_IN + A_QW + 2 * A_KVW
D_FF = 2816
CONV_W = 3
DEPTH = 2
ALPHA = (2 * DEPTH) ** 0.25
LN_EPS = 1e-5
ATT_SCALE = A_HEAD ** -0.5

VMEM_LIMIT_BYTES = 56 * 1024 * 1024
LANES = 128
SUBLANES = 8

RWKV_CHUNK = 64
NEG_INF = float("-inf")

_NN = (((1,), (0,)), ((), ()))
_NT = (((1,), (1,)), ((), ()))
_TN = (((0,), (0,)), ((), ()))


def _cparams(*sem):
    return pltpu.CompilerParams(dimension_semantics=sem or None, vmem_limit_bytes=VMEM_LIMIT_BYTES)


def _bdot(a, b, dims=_NN):
    return lax.dot_general(a.astype(BF16), b.astype(BF16), dims, preferred_element_type=F32)


def _dot3(a, b, dims=_NN):
    ah = a.astype(BF16)
    al = (a - ah.astype(F32)).astype(BF16)
    bh = b.astype(BF16)
    bl = (b - bh.astype(F32)).astype(BF16)
    f = lambda x, y: lax.dot_general(x, y, dims, preferred_element_type=F32)
    return f(ah, bh) + f(ah, bl) + f(al, bh)


def _ln(x, g, b):
    mu = jnp.mean(x, -1, keepdims=True)
    xc = x - mu
    var = jnp.mean(xc * xc, -1, keepdims=True)
    return xc * lax.rsqrt(var + LN_EPS) * g + b


def _head_sum(x, head):
    lane_head = lax.broadcasted_iota(jnp.int32, (1, x.shape[1]), 1) // head
    out = jnp.zeros_like(x)
    for h in range(x.shape[1] // head):
        m = lane_head == h
        s = jnp.sum(jnp.where(m, x, 0.0), -1, keepdims=True)
        out = jnp.where(m, s, out)
    return out


def _full(shape):
    n = len(shape)
    return pl.BlockSpec(shape, lambda *_: (0,) * n)


def _in_proj_kernel(pre_ln, attn_layouts, x_ref, g_ref, b_ref, w_ref, cos_ref, sa_ref, sb_ref,
                    u_ref, c_ref, k_ref, v_ref, *rest):
    x = x_ref[...]
    if pre_ln:
        x = _ln(x, g_ref[...], b_ref[...])
        rest[-1][...] = x
    h = _bdot(x, w_ref[...])
    u_ref[...] = h[:, :S_WIDTH]
    c_ref[...] = h[:, S_WIDTH:S_WIDTH + R_IN]
    cos, sa, sb = cos_ref[...], sa_ref[...], sb_ref[...]
    base = S_WIDTH + R_IN
    nq = A_QW // LANES
    per = LANES // A_HEAD
    for j in range((A_QW + A_KVW) // LANES):
        ch = h[:, base + LANES * j: base + LANES * (j + 1)]
        rot = (ch * cos + pltpu.roll(ch, LANES - ROT_DIM // 2, 1) * sa
               + pltpu.roll(ch, ROT_DIM // 2, 1) * sb)
        if j < nq:
            if attn_layouts:
                rest[0][LANES * j:LANES * (j + 1), :] = rot.T
            else:
                rest[0][:, LANES * j:LANES * (j + 1)] = rot
        else:
            k_ref[:, LANES * (j - nq):LANES * (j - nq + 1)] = rot
            if attn_layouts:
                for i in range(per):
                    rest[1][per * (j - nq) + i] = rot[:, A_HEAD * i:A_HEAD * (i + 1)]
    v = h[:, base + A_QW + A_KVW:]
    v_ref[...] = v
    if attn_layouts:
        for j in range(A_KVW // LANES):
            rest[2][LANES * j:LANES * (j + 1), :] = v[:, LANES * j:LANES * (j + 1)].T


def _in_proj(x, ln_g, ln_b, w, rope, tm, pos_tiles, pre_ln, attn_layouts):
    m = x.shape[0]
    row = lambda width: pl.BlockSpec((tm, width), lambda i: (i, 0))
    col = lambda width: pl.BlockSpec((width, tm), lambda i: (0, i))
    tab = pl.BlockSpec((tm, LANES), lambda i: (i % pos_tiles, 0))
    f32 = lambda *shape: jax.ShapeDtypeStruct(shape, F32)
    specs = [row(S_WIDTH), row(R_IN), row(A_KVW), row(A_KVW)]
    shapes = [f32(m, S_WIDTH), f32(m, R_IN), f32(m, A_KVW), f32(m, A_KVW)]
    if attn_layouts:
        specs += [col(A_QW), pl.BlockSpec((A_KV_HEADS, tm, A_HEAD), lambda i: (0, i, 0)), col(A_KVW)]
        shapes += [f32(A_QW, m), f32(A_KV_HEADS, m, A_HEAD), f32(A_KVW, m)]
    else:
        specs += [row(A_QW)]
        shapes += [f32(m, A_QW)]
    if pre_ln:
        specs += [row(D_MODEL)]
        shapes += [f32(m, D_MODEL)]
    return pl.pallas_call(
        functools.partial(_in_proj_kernel, pre_ln, attn_layouts),
        grid=(m // tm,),
        in_specs=[row(D_MODEL), _full((1, D_MODEL)), _full((1, D_MODEL)), _full((D_MODEL, N_REST)),
                  tab, tab, tab],
        out_specs=specs,
        out_shape=shapes,
        compiler_params=_cparams("parallel"),
        name="in_proj",
    )(x, ln_g, ln_b, w, *rope)


def _rope_tables(pos):
    half = ROT_DIM // 2
    inv = ROPE_THETA ** (-jnp.arange(half, dtype=F32) / half)
    ang = pos.astype(F32)[:, None] * inv[None, :]
    cos, sin = jnp.cos(ang), jnp.sin(ang)
    n = pos.shape[0]
    pad = jnp.zeros((n, A_HEAD - ROT_DIM), F32)
    zero = jnp.zeros((n, half), F32)
    cos_h = jnp.concatenate([cos, cos, pad + 1.0], -1)
    sa_h = jnp.concatenate([-sin, zero, pad], -1)
    sb_h = jnp.concatenate([zero, sin, pad], -1)
    rep = LANES // A_HEAD
    return tuple(jnp.tile(t, (1, rep)) for t in (cos_h, sa_h, sb_h))


def _s5_params(a_re, a_im, log_dt, b_re, b_im, c_re, c_im):
    dt = jnp.exp(log_dt)[:, None]
    lam_re, lam_im = a_re * dt, a_im * dt

    def power(k):
        mag = jnp.exp(lam_re * k)
        return (mag * jnp.cos(lam_im * k)).reshape(-1), (mag * jnp.sin(lam_im * k)).reshape(-1)

    abar_re, abar_im = power(1.0)
    den = (a_re * a_re + a_im * a_im).reshape(-1)
    ar, ai = a_re.reshape(-1), a_im.reshape(-1)
    em_re = abar_re - 1.0
    coef = jnp.stack([(em_re * ar + abar_im * ai) / den, (abar_im * ar - em_re * ai) / den])
    pw = jnp.stack([jnp.stack(x) for x in zip(*[power(float(k)) for k in range(1, SUBLANES + 1)])])
    dbl = jnp.stack([jnp.stack(x) for x in zip(*[power(float(k)) for k in (1, 2, 4)])])
    eye = jnp.eye(S_GROUPS, dtype=F32)
    wb = jnp.concatenate([jnp.einsum("gpc,gh->gchp", b, eye).reshape(S_WIDTH, S_LANES)
                          for b in (b_re, b_im)], axis=1)
    wc = jnp.concatenate([jnp.einsum("gcp,gh->gphc", c, eye).reshape(S_LANES, S_WIDTH)
                          for c in (c_re, -c_im)], axis=0)
    return dict(coef=coef, pw=pw, dbl=dbl, abar=jnp.stack([abar_re, abar_im]),
                wb=wb.astype(BF16), wc=wc.astype(BF16))


def _s5_glu(s_re, s_im, u, wc_ref, d_ref, wg_ref, bg_ref):
    s = jnp.concatenate([s_re, s_im], axis=1)
    y = _bdot(s, wc_ref[...]) + d_ref[...] * u
    z = jax.nn.gelu(y)
    return z * jax.nn.sigmoid(_bdot(z, wg_ref[...]) + bg_ref[...])


def _s5_kernel(u_ref, wb_ref, wc_ref, coef_ref, pw_ref, dbl_ref, d_ref, wg_ref, bg_ref, s0_ref,
               y_ref, sfin_ref, s_scr, carry_scr):
    j = pl.program_id(1)
    t = u_ref.shape[0]

    @pl.when(j == 0)
    def _():
        carry_scr[...] = s0_ref[0]

    u = u_ref[...]
    bu = _bdot(u, wb_ref[...])
    b_re, b_im = bu[:, :S_LANES], bu[:, S_LANES:]
    c_re, c_im = coef_ref[0:1, :], coef_ref[1:2, :]
    s_scr[0] = c_re * b_re - c_im * b_im
    s_scr[1] = c_re * b_im + c_im * b_re
    row = lax.broadcasted_iota(jnp.int32, (SUBLANES, S_LANES), 0)

    def body(g, carry):
        cr, ci = carry
        off = pl.multiple_of(g * SUBLANES, SUBLANES)
        sr = s_scr[0, pl.ds(off, SUBLANES), :]
        si = s_scr[1, pl.ds(off, SUBLANES), :]
        for n, d in enumerate((1, 2, 4)):
            ar, ai = dbl_ref[0, n:n + 1, :], dbl_ref[1, n:n + 1, :]
            pr = jnp.where(row >= d, pltpu.roll(sr, d, 0), 0.0)
            pi = jnp.where(row >= d, pltpu.roll(si, d, 0), 0.0)
            sr, si = sr + ar * pr - ai * pi, si + ar * pi + ai * pr
        p_re, p_im = pw_ref[0], pw_ref[1]
        sr, si = sr + p_re * cr - p_im * ci, si + p_re * ci + p_im * cr
        s_scr[0, pl.ds(off, SUBLANES), :] = sr
        s_scr[1, pl.ds(off, SUBLANES), :] = si
        return sr[SUBLANES - 1:SUBLANES, :], si[SUBLANES - 1:SUBLANES, :]

    cr, ci = lax.fori_loop(0, t // SUBLANES, body, (carry_scr[0:1, :], carry_scr[1:2, :]), unroll=2)
    carry_scr[0:1, :] = cr
    carry_scr[1:2, :] = ci
    y_ref[...] = _s5_glu(s_scr[0], s_scr[1], u, wc_ref, d_ref, wg_ref, bg_ref)

    @pl.when(j == pl.num_programs(1) - 1)
    def _():
        sfin_ref[0] = carry_scr[...]


def _s5_prompt(u, sp, d, w_glu, b_glu, s0, nb, seq, t):
    nt = seq // t
    return pl.pallas_call(
        _s5_kernel,
        grid=(nb, nt),
        in_specs=[pl.BlockSpec((t, S_WIDTH), lambda b, j: (b * nt + j, 0)),
                  _full((S_WIDTH, 2 * S_LANES)), _full((2 * S_LANES, S_WIDTH)), _full((2, S_LANES)),
                  _full((2, SUBLANES, S_LANES)), _full((2, 3, S_LANES)), _full((1, S_WIDTH)),
                  _full((S_WIDTH, S_WIDTH)), _full((1, S_WIDTH)),
                  pl.BlockSpec((1, 2, S_LANES), lambda b, j: (b, 0, 0))],
        out_specs=[pl.BlockSpec((t, S_WIDTH), lambda b, j: (b * nt + j, 0)),
                   pl.BlockSpec((1, 2, S_LANES), lambda b, j: (b, 0, 0))],
        out_shape=[jax.ShapeDtypeStruct((nb * seq, S_WIDTH), F32),
                   jax.ShapeDtypeStruct((nb, 2, S_LANES), F32)],
        scratch_shapes=[pltpu.VMEM((2, t, S_LANES), F32), pltpu.VMEM((2, S_LANES), F32)],
        compiler_params=_cparams("parallel", "arbitrary"),
        name="s5_scan",
    )(u, sp["wb"], sp["wc"], sp["coef"], sp["pw"], sp["dbl"], d, w_glu, b_glu, s0)


def _s5_step_kernel(u_ref, wb_ref, wc_ref, coef_ref, abar_ref, d_ref, wg_ref, bg_ref, s0_ref,
                    y_ref, s_ref):
    u = u_ref[...]
    bu = _bdot(u, wb_ref[...])
    b_re, b_im = bu[:, :S_LANES], bu[:, S_LANES:]
    c_re, c_im = coef_ref[0:1, :], coef_ref[1:2, :]
    a_re, a_im = abar_ref[0:1, :], abar_ref[1:2, :]
    s0_re, s0_im = s0_ref[:, :S_LANES], s0_ref[:, S_LANES:]
    s_re = c_re * b_re - c_im * b_im + (a_re * s0_re - a_im * s0_im)
    s_im = c_re * b_im + c_im * b_re + (a_re * s0_im + a_im * s0_re)
    s_ref[:, :S_LANES] = s_re
    s_ref[:, S_LANES:] = s_im
    y_ref[...] = _s5_glu(s_re, s_im, u, wc_ref, d_ref, wg_ref, bg_ref)


def _s5_step(u, sp, d, w_glu, b_glu, s0):
    n = u.shape[0]
    return pl.pallas_call(
        _s5_step_kernel,
        out_shape=[jax.ShapeDtypeStruct((n, S_WIDTH), F32), jax.ShapeDtypeStruct((n, 2 * S_LANES), F32)],
        compiler_params=_cparams(),
        name="s5_step",
    )(u, sp["wb"], sp["wc"], sp["coef"], sp["abar"], d, w_glu, b_glu, s0)


def _rwkv_pre_kernel(decode, c_ref, prev_ref, sh0_ref, mu_ref, w0_ref, w2_ref, a0_ref, a2_ref, g2_ref,
                     kk_ref, ka_ref, rk_ref,
                     r_o, lw_o, k_o, v_o, a_o, b_o, g_o, bonus_o):
    c = c_ref[...]
    if decode:
        prev = sh0_ref[...]
    else:
        j = pl.program_id(1)
        last = jnp.where(j == 0, sh0_ref[0], prev_ref[SUBLANES - 1:SUBLANES, :])
        row = lax.broadcasted_iota(jnp.int32, c.shape, 0)
        prev = jnp.where(row == 0, last, pltpu.roll(c, 1, 0))
    cf = c + (prev - c) * mu_ref[...]
    r = cf[:, :R_WIDTH]
    k = cf[:, R_WIDTH:2 * R_WIDTH]
    v = cf[:, 2 * R_WIDTH:3 * R_WIDTH]
    lora = cf[:, 3 * R_WIDTH:]
    w_log = -jax.nn.softplus(-(w0_ref[...] + _bdot(jnp.tanh(lora), w2_ref[...]))) - 0.5
    a = jax.nn.sigmoid(a0_ref[...] + _bdot(lora, a2_ref[...]))
    g = _bdot(jax.nn.sigmoid(lora), g2_ref[...])
    kk = k * kk_ref[...]
    kk = kk * lax.rsqrt(jnp.maximum(_head_sum(kk * kk, R_HEAD), 1e-24))
    k = k * (1.0 + (a - 1.0) * ka_ref[...])
    r_o[...] = r
    lw_o[...] = -jnp.exp(w_log)
    k_o[...] = k
    v_o[...] = v
    a_o[...] = -kk
    b_o[...] = kk * a
    g_o[...] = g
    bonus_o[...] = _head_sum(r * k * rk_ref[...], R_HEAD) * v


def _rwkv_weights(mu, w0, w2, a0, a2, g2, k_k, k_a, r_k):
    z = lambda n: jnp.zeros((n, R_WIDTH), F32)
    w2p = jnp.concatenate([w2, z(A_LORA + G_LORA)], 0).astype(BF16)
    a2p = jnp.concatenate([z(W_LORA), a2, z(G_LORA)], 0).astype(BF16)
    g2p = jnp.concatenate([z(W_LORA + A_LORA), g2], 0).astype(BF16)
    row = lambda x: x.reshape(1, -1)
    return [row(mu), row(w0), w2p, row(a0), a2p, g2p, row(k_k), row(k_a), row(r_k)]


def _rwkv_pre(c, sh0, wts, nb, seq, t, decode):
    m = c.shape[0]
    outs = [jax.ShapeDtypeStruct((m, R_WIDTH), F32)] * 8
    wspecs = [_full(w.shape) for w in wts]
    if decode:
        return pl.pallas_call(
            functools.partial(_rwkv_pre_kernel, True),
            grid=(1,),
            in_specs=[_full((m, R_IN)), _full((SUBLANES, R_IN)), _full((m, R_IN))] + wspecs,
            out_specs=[_full((m, R_WIDTH))] * 8,
            out_shape=outs,
            compiler_params=_cparams("arbitrary"),
            name="rwkv_pre_step",
        )(c, c[:SUBLANES], sh0, *wts)
    nt = seq // t
    per8 = t // SUBLANES
    return pl.pallas_call(
        functools.partial(_rwkv_pre_kernel, False),
        grid=(nb, nt),
        in_specs=[pl.BlockSpec((t, R_IN), lambda b, j: (b * nt + j, 0)),
                  pl.BlockSpec((SUBLANES, R_IN), lambda b, j: (jnp.maximum((b * nt + j) * per8 - 1, 0), 0)),
                  pl.BlockSpec((1, 1, R_IN), lambda b, j: (b, 0, 0))] + wspecs,
        out_specs=[pl.BlockSpec((t, R_WIDTH), lambda b, j: (b * nt + j, 0))] * 8,
        out_shape=outs,
        compiler_params=_cparams("parallel", "parallel"),
        name="rwkv_pre",
    )(c, c, sh0.reshape(nb, 1, R_IN), *wts)


def _rwkv_chunk_kernel(r_ref, lw_ref, k_ref, v_ref, a_ref, b_ref, y1_ref, y2_ref, g_ref, f_ref):
    t = r_ref.shape[0]
    lw = lw_ref[...]
    row = lax.broadcasted_iota(jnp.int32, lw.shape, 0)
    cw = lw
    d = 1
    while d < t:
        cw = cw + jnp.where(row >= d, pltpu.roll(cw, d, 0), 0.0)
        d *= 2
    cw_end = cw[t - 1:t, :]
    e_neg = jnp.exp(-cw)
    e_rem = jnp.exp(cw_end - cw)
    a_t = a_ref[...] * jnp.exp(cw - lw)
    r_t = r_ref[...] * jnp.exp(cw)
    b_t = b_ref[...] * e_neg
    k_t = k_ref[...] * e_neg
    b_e = b_ref[...] * e_rem
    k_e = k_ref[...] * e_rem
    e_end = jnp.exp(cw_end)
    v = v_ref[...]
    ri = lax.broadcasted_iota(jnp.int32, (t, t), 0)
    ci = lax.broadcasted_iota(jnp.int32, (t, t), 1)
    hi = lax.broadcasted_iota(jnp.int32, (R_HEAD, R_HEAD), 0)
    hj = lax.broadcasted_iota(jnp.int32, (R_HEAD, R_HEAD), 1)
    heads = range(R_HEADS)
    per_head = lambda x: [x[:, h * R_HEAD:(h + 1) * R_HEAD] for h in heads]
    ah, rh, bh, kh, vh = per_head(a_t), per_head(r_t), per_head(b_t), per_head(k_t), per_head(v)
    a_ab = [jnp.where(ri > ci, _dot3(ah[h], bh[h], _NT), 0.0) for h in heads]
    a_ak = [jnp.where(ri > ci, _dot3(ah[h], kh[h], _NT), 0.0) for h in heads]
    a_rb = [jnp.where(ri >= ci, _dot3(rh[h], bh[h], _NT), 0.0) for h in heads]
    a_rk = [jnp.where(ri >= ci, _dot3(rh[h], kh[h], _NT), 0.0) for h in heads]
    akv = [_dot3(a_ak[h], vh[h]) for h in heads]
    rkv = [_dot3(a_rk[h], vh[h]) for h in heads]
    inv = [jnp.where(ri == ci, 1.0, 0.0) + a_ab[h] for h in heads]
    apow = a_ab
    n = 2
    while n < t:
        apow = [_dot3(apow[h], apow[h]) for h in heads]
        inv = [inv[h] + _dot3(apow[h], inv[h]) for h in heads]
        n *= 2
    w1 = [_dot3(inv[h], ah[h]) for h in heads]
    w2 = [_dot3(inv[h], akv[h]) for h in heads]
    be_t = [x.T for x in per_head(b_e)]
    ke_t = [x.T for x in per_head(k_e)]
    ee = per_head(e_end)
    for h in heads:
        y1_ref[0, h] = rh[h] + _dot3(a_rb[h], w1[h])
        y2_ref[0, h] = _dot3(a_rb[h], w2[h]) + rkv[h]
        g_ref[0, h] = jnp.where(hi == hj, ee[h], 0.0) + _dot3(be_t[h], w1[h])
        f_ref[0, h] = _dot3(be_t[h], w2[h]) + _dot3(ke_t[h], vh[h])


def _rwkv_chunks(r, lw, k, v, a, b, t):
    m = r.shape[0]
    nc = m // t
    spec = pl.BlockSpec((t, R_WIDTH), lambda i: (i, 0))
    yspec = pl.BlockSpec((1, R_HEADS, t, R_HEAD), lambda i: (i, 0, 0, 0))
    gspec = pl.BlockSpec((1, R_HEADS, R_HEAD, R_HEAD), lambda i: (i, 0, 0, 0))
    return pl.pallas_call(
        _rwkv_chunk_kernel,
        grid=(nc,),
        in_specs=[spec] * 6,
        out_specs=[yspec, yspec, gspec, gspec],
        out_shape=[jax.ShapeDtypeStruct((nc, R_HEADS, t, R_HEAD), F32)] * 2
        + [jax.ShapeDtypeStruct((nc, R_HEADS, R_HEAD, R_HEAD), F32)] * 2,
        compiler_params=_cparams("parallel"),
        name="rwkv_chunk",
    )(r, lw, k, v, a, b)


def _rwkv_post(y, bonus, g, lng, lnb):
    m = _head_sum(y, R_HEAD) * (1.0 / R_HEAD)
    yc = y - m
    var = _head_sum(yc * yc, R_HEAD) * (1.0 / R_HEAD)
    return (yc * lax.rsqrt(var + GN_EPS) * lng + lnb + bonus) * g


def _rwkv_scan_kernel(y1_ref, y2_ref, g_ref, f_ref, bonus_ref, gate_ref, lng_ref, lnb_ref, h0_ref,
                      y_ref, hfin_ref, h_scr, y_scr):
    j = pl.program_id(1)

    @pl.when(j == 0)
    def _():
        h_scr[...] = h0_ref[0]

    for h in range(R_HEADS):
        state = h_scr[h]
        y_scr[:, h * R_HEAD:(h + 1) * R_HEAD] = _dot3(y1_ref[0, h], state) + y2_ref[0, h]
        h_scr[h] = _dot3(g_ref[0, h], state) + f_ref[0, h]
    y_ref[...] = _rwkv_post(y_scr[...], bonus_ref[...], gate_ref[...], lng_ref[...], lnb_ref[...])

    @pl.when(j == pl.num_programs(1) - 1)
    def _():
        hfin_ref[0] = h_scr[...]


def _rwkv_scan(y1, y2, g, f, bonus, gate, lng, lnb, h0, nb, seq, t):
    nc = seq // t
    yspec = pl.BlockSpec((1, R_HEADS, t, R_HEAD), lambda b, j: (b * nc + j, 0, 0, 0))
    gspec = pl.BlockSpec((1, R_HEADS, R_HEAD, R_HEAD), lambda b, j: (b * nc + j, 0, 0, 0))
    row = pl.BlockSpec((t, R_WIDTH), lambda b, j: (b * nc + j, 0))
    hspec = pl.BlockSpec((1, R_HEADS, R_HEAD, R_HEAD), lambda b, j: (b, 0, 0, 0))
    return pl.pallas_call(
        _rwkv_scan_kernel,
        grid=(nb, nc),
        in_specs=[yspec, yspec, gspec, gspec, row, row, _full((1, R_WIDTH)), _full((1, R_WIDTH)), hspec],
        out_specs=[row, hspec],
        out_shape=[jax.ShapeDtypeStruct((nb * seq, R_WIDTH), F32),
                   jax.ShapeDtypeStruct((nb, R_HEADS, R_HEAD, R_HEAD), F32)],
        scratch_shapes=[pltpu.VMEM((R_HEADS, R_HEAD, R_HEAD), F32), pltpu.VMEM((t, R_WIDTH), F32)],
        compiler_params=_cparams("parallel", "arbitrary"),
        name="rwkv_scan",
    )(y1, y2, g, f, bonus, gate, lng, lnb, h0)


RWKV_STEP_ROWS = 8


def _rwkv_step_kernel(r_ref, lw_ref, k_ref, v_ref, a_ref, b_ref, s_ref, y_ref, so_ref):
    ii = lax.broadcasted_iota(jnp.int32, (R_HEAD, R_HEAD), 0)
    jj = lax.broadcasted_iota(jnp.int32, (R_HEAD, R_HEAD), 1)
    eye = ii == jj
    for n in range(RWKV_STEP_ROWS):
        one = lambda ref: ref[n:n + 1, :]
        s = s_ref[n]
        sa = jnp.sum(s * one(a_ref), -1, keepdims=True)
        v_col = jnp.sum(jnp.where(eye, one(v_ref), 0.0), -1, keepdims=True)
        s_new = s * jnp.exp(one(lw_ref)) + sa * one(b_ref) + v_col * one(k_ref)
        y_col = jnp.sum(s_new * one(r_ref), -1, keepdims=True)
        so_ref[n] = s_new
        y_ref[n:n + 1, :] = jnp.sum(jnp.where(eye, y_col, 0.0), 0, keepdims=True)


def _rwkv_step(vecs, s0):
    n = s0.shape[0]
    vspec = pl.BlockSpec((RWKV_STEP_ROWS, R_HEAD), lambda i: (i, 0))
    sspec = pl.BlockSpec((RWKV_STEP_ROWS, R_HEAD, R_HEAD), lambda i: (i, 0, 0))
    return pl.pallas_call(
        _rwkv_step_kernel,
        grid=(n // RWKV_STEP_ROWS,),
        in_specs=[vspec] * 6 + [sspec],
        out_specs=[vspec, sspec],
        out_shape=[jax.ShapeDtypeStruct((n, R_HEAD), F32), jax.ShapeDtypeStruct((n, R_HEAD, R_HEAD), F32)],
        compiler_params=_cparams("parallel"),
        name="rwkv_step",
    )(*vecs, s0)


def _rwkv_post_kernel(y_ref, bonus_ref, gate_ref, lng_ref, lnb_ref, o_ref):
    o_ref[...] = _rwkv_post(y_ref[...], bonus_ref[...], gate_ref[...], lng_ref[...], lnb_ref[...])


def _rwkv_post_call(y, bonus, gate, lng, lnb):
    return pl.pallas_call(
        _rwkv_post_kernel,
        out_shape=jax.ShapeDtypeStruct(y.shape, F32),
        compiler_params=_cparams(),
        name="rwkv_post_step",
    )(y, bonus, gate, lng, lnb)


def _moba_kernel(qt_ref, k_ref, vt_ref, o_ref, km_scr, bias_scr):
    qi = pl.program_id(2)
    nblk = km_scr.shape[0]
    cols = A_GROUP * MOBA_BLOCK

    @pl.when(qi == 0)
    def _():
        km_scr[...] = jnp.mean(k_ref[0].reshape(nblk, MOBA_BLOCK, A_HEAD), axis=1)

    qt = jnp.concatenate([qt_ref[g * A_HEAD:(g + 1) * A_HEAD, :] for g in range(A_GROUP)], axis=1)
    qb = (qt * ATT_SCALE).astype(BF16)
    blk = lax.broadcasted_iota(jnp.int32, (nblk, cols), 0)
    gate = jnp.where(blk < qi, _bdot(km_scr[...], qb), NEG_INF)
    sel = jnp.zeros(gate.shape, jnp.bool_)
    for _ in range(MOBA_TOPK):
        m = jnp.max(gate, 0, keepdims=True)
        idx = jnp.min(jnp.where(gate == m, blk, nblk), 0, keepdims=True)
        pick = (blk == idx) & (m > NEG_INF)
        sel = sel | pick
        gate = jnp.where(pick, NEG_INF, gate)
    bias_scr[...] = jnp.where(sel, 0.0, NEG_INF)

    def scores(n):
        off = pl.multiple_of(n * MOBA_BLOCK, MOBA_BLOCK)
        return _bdot(k_ref[0, pl.ds(off, MOBA_BLOCK), :], qb)

    def values_t(n):
        off = pl.multiple_of(n * MOBA_BLOCK, MOBA_BLOCK)
        return vt_ref[:, pl.ds(off, MOBA_BLOCK)].astype(BF16)

    s, vtb = scores(qi), values_t(qi)
    key = lax.broadcasted_iota(jnp.int32, s.shape, 0)
    pos = lax.broadcasted_iota(jnp.int32, s.shape, 1) % MOBA_BLOCK
    s = jnp.where(key <= pos, s, NEG_INF)
    m0 = jnp.max(s, 0, keepdims=True)
    p = jnp.exp(s - m0)
    l0 = jnp.sum(p, 0, keepdims=True)
    acc0 = _bdot(vtb, p)

    def update(n, s, m, l, acc):
        s = s + bias_scr[pl.ds(n, 1), :]
        m_new = jnp.maximum(m, jnp.max(s, 0, keepdims=True))
        alpha = jnp.exp(m - m_new)
        p = jnp.exp(s - m_new)
        return m_new, alpha * l + jnp.sum(p, 0, keepdims=True), alpha * acc + _bdot(values_t(n), p)

    def body(j, carry):
        m, l, acc, s_even = carry
        n0 = 2 * j
        s_odd = scores(n0 + 1)
        m, l, acc = update(n0, s_even, m, l, acc)
        s_even = scores(jnp.minimum(n0 + 2, nblk - 1))
        m, l, acc = update(n0 + 1, s_odd, m, l, acc)
        return m, l, acc, s_even

    _, l, acc, _ = lax.fori_loop(0, lax.shift_right_logical(qi + 1, 1), body, (m0, l0, acc0, scores(0)))
    o = acc / l
    for g in range(A_GROUP):
        o_ref[g * A_HEAD:(g + 1) * A_HEAD, :] = o[:, g * MOBA_BLOCK:(g + 1) * MOBA_BLOCK]


def _moba_prompt(qt, kh, vt, nb, seq):
    nblk = seq // MOBA_BLOCK
    qspec = pl.BlockSpec((A_GROUP * A_HEAD, MOBA_BLOCK), lambda b, h, i: (h, b * nblk + i))
    return pl.pallas_call(
        _moba_kernel,
        grid=(nb, A_KV_HEADS, nblk),
        in_specs=[qspec,
                  pl.BlockSpec((1, seq, A_HEAD), lambda b, h, i: (h, b, 0)),
                  pl.BlockSpec((A_HEAD, seq), lambda b, h, i: (h, b))],
        out_specs=qspec,
        out_shape=jax.ShapeDtypeStruct(qt.shape, F32),
        scratch_shapes=[pltpu.VMEM((nblk, A_HEAD), F32), pltpu.VMEM((nblk, A_GROUP * MOBA_BLOCK), F32)],
        compiler_params=_cparams("parallel", "parallel", "arbitrary"),
        name="moba_prompt",
    )(qt, kh, vt)


def _kmeans_kernel(n_in, pt_ref, *refs):
    pages, o_ref = refs[:n_in], refs[n_in]
    per_blk = MOBA_BLOCK // PAGE_SIZE
    for i in range(n_in // per_blk):
        tot = sum(jnp.sum(pages[per_blk * i + p][0], axis=0, keepdims=True) for p in range(per_blk))
        o_ref[0, i:i + 1, :] = tot * (1.0 / MOBA_BLOCK)


def _decode_kmeans(cache, page_table, layer, n_pool):
    nb, n_pages = page_table.shape
    n_in = min(16, n_pages)
    steps = n_pages // n_in
    per_blk = MOBA_BLOCK // PAGE_SIZE
    pages = cache.reshape(-1, PAGE_SIZE, A_KVW)

    def page_spec(i):
        return pl.BlockSpec((1, PAGE_SIZE, A_KVW),
                            lambda b, j, pt: (layer * n_pool + pt[b * n_pages + j * n_in + i], 0, 0))

    return pl.pallas_call(
        functools.partial(_kmeans_kernel, n_in),
        grid_spec=pltpu.PrefetchScalarGridSpec(
            num_scalar_prefetch=1,
            grid=(nb, steps),
            in_specs=[page_spec(i) for i in range(n_in)],
            out_specs=pl.BlockSpec((1, n_in // per_blk, A_KVW), lambda b, j, pt: (b, j, 0)),
        ),
        out_shape=jax.ShapeDtypeStruct((nb, n_pages // per_blk, A_KVW), F32),
        compiler_params=_cparams("parallel", "arbitrary"),
        name="decode_kmeans",
    )(page_table.reshape(-1), *([pages] * n_in))


def _decode_select_kernel(q_ref, km_ref, o_ref):
    nblk = km_ref.shape[1]
    gate = _bdot(q_ref[0], km_ref[0], _NT)
    col = lax.broadcasted_iota(jnp.int32, gate.shape, 1)
    lane = lax.broadcasted_iota(jnp.int32, (A_HEADS, LANES), 1)
    out = jnp.zeros((A_HEADS, LANES), jnp.int32)
    for r in range(MOBA_TOPK):
        m = jnp.max(gate, -1, keepdims=True)
        idx = jnp.min(jnp.where(gate == m, col, nblk), -1, keepdims=True)
        out = jnp.where(lane == r, idx, out)
        gate = jnp.where(col == idx, NEG_INF, gate)
    o_ref[0] = out


def _decode_select(q_exp, kmeans):
    nb, nblk, _ = kmeans.shape
    return pl.pallas_call(
        _decode_select_kernel,
        grid=(nb,),
        in_specs=[pl.BlockSpec((1, A_HEADS, A_KVW), lambda b: (b, 0, 0)),
                  pl.BlockSpec((1, nblk, A_KVW), lambda b: (b, 0, 0))],
        out_specs=pl.BlockSpec((1, A_HEADS, LANES), lambda b: (b, 0, 0)),
        out_shape=jax.ShapeDtypeStruct((nb, A_HEADS, LANES), jnp.int32),
        compiler_params=_cparams("parallel"),
        name="decode_select",
    )(q_exp, kmeans)


DEC_PAGES = MOBA_TOPK * (MOBA_BLOCK // PAGE_SIZE)


def _decode_attn_kernel(pg_ref, q_ref, kn_ref, vn_ref, *refs):
    k_pages, v_pages, o_ref = refs[:DEC_PAGES], refs[DEC_PAGES:2 * DEC_PAGES], refs[2 * DEC_PAGES]
    h = pl.program_id(1)
    q = q_ref[0, pl.ds(h, 1), :].astype(BF16)
    rb = lambda x: x.astype(BF16).astype(F32)
    s_self = jnp.sum(rb(q) * rb(kn_ref[0]), -1, keepdims=True) * ATT_SCALE
    scores = [_bdot(q, kp[0], _NT) * ATT_SCALE for kp in k_pages]
    m = s_self
    for s in scores:
        m = jnp.maximum(m, jnp.max(s, -1, keepdims=True))
    p_self = jnp.exp(s_self - m)
    l = p_self
    acc = rb(p_self) * rb(vn_ref[0])
    for s, vp in zip(scores, v_pages):
        p = jnp.exp(s - m)
        l = l + jnp.sum(p, -1, keepdims=True)
        acc = acc + _bdot(p, vp[0])
    o_ref[0, pl.ds(h, 1), :] = acc / l


def _decode_attn(q_exp, k_new, v_new, cache_k, cache_v, page_ids, layer, n_pool):
    nb = q_exp.shape[0]
    kp = cache_k.reshape(-1, PAGE_SIZE, A_KVW)
    vp = cache_v.reshape(-1, PAGE_SIZE, A_KVW)

    def page_spec(i):
        return pl.BlockSpec(
            (1, PAGE_SIZE, A_KVW),
            lambda b, h, pg: (layer * n_pool + pg[(b * A_HEADS + h) * DEC_PAGES + i], 0, 0))

    bspec = pl.BlockSpec((1, A_HEADS, A_KVW), lambda b, h, pg: (b, 0, 0))
    nspec = pl.BlockSpec((1, 1, A_KVW), lambda b, h, pg: (b, 0, 0))
    return pl.pallas_call(
        _decode_attn_kernel,
        grid_spec=pltpu.PrefetchScalarGridSpec(
            num_scalar_prefetch=1,
            grid=(nb, A_HEADS),
            in_specs=[bspec, nspec, nspec] + [page_spec(i) for i in range(DEC_PAGES)] * 2,
            out_specs=bspec,
        ),
        out_shape=jax.ShapeDtypeStruct((nb, A_HEADS, A_KVW), F32),
        compiler_params=_cparams("parallel", "arbitrary"),
        name="decode_attn",
    )(page_ids.reshape(-1), q_exp, k_new.reshape(nb, 1, A_KVW), v_new.reshape(nb, 1, A_KVW),
      *([kp] * DEC_PAGES), *([vp] * DEC_PAGES))


def _moba_decode(q, k_new, v_new, cache_k, cache_v, page_table, layer):
    nb = q.shape[0]
    n_pool = cache_k.shape[1]
    onehot = (jnp.arange(A_KV_HEADS)[None, :] == (jnp.arange(A_HEADS) // A_GROUP)[:, None]).astype(F32)
    q_exp = (q.reshape(nb, A_HEADS, 1, A_HEAD) * onehot[None, :, :, None]).reshape(nb, A_HEADS, A_KVW)
    kmeans = _decode_kmeans(cache_k, page_table, layer, n_pool)
    blk = _decode_select(q_exp, kmeans)[:, :, :MOBA_TOPK]
    per_blk = MOBA_BLOCK // PAGE_SIZE
    pages_of = (blk[..., None] * per_blk + jnp.arange(per_blk)).reshape(nb, A_HEADS * DEC_PAGES)
    page_ids = jnp.take_along_axis(page_table, pages_of, axis=1)
    o = _decode_attn(q_exp, k_new, v_new, cache_k, cache_v, page_ids, layer, n_pool)
    o = o.reshape(nb, A_HEADS, A_KV_HEADS, A_HEAD)
    return jnp.einsum("bhkd,hk->bhd", o, onehot).reshape(nb, A_QW)


def _merge_kernel(ya_transposed, x_ref, ys_ref, yr_ref, ya_ref, wg_ref, ps_ref, pr_ref, pa_ref, wo_ref,
                  g_ref, b_ref, o_ref):
    x = x_ref[...]
    xb = x.astype(BF16)
    ya = ya_ref[...].T if ya_transposed else ya_ref[...]
    merged = None
    for i, (y, p_ref) in enumerate(((ys_ref[...], ps_ref), (yr_ref[...], pr_ref), (ya, pa_ref))):
        gate = jax.nn.sigmoid(_bdot(xb, wg_ref[:, i * D_MODEL:(i + 1) * D_MODEL]))
        term = gate * _bdot(y, p_ref[...])
        merged = term if merged is None else merged + term
    o_ref[...] = _ln(ALPHA * x + _bdot(merged, wo_ref[...]), g_ref[...], b_ref[...])


def _merge(x, ys, yr, ya, wg, ps, pr, pa, wo, g, b, tm, ya_transposed):
    m = x.shape[0]
    row = lambda width: pl.BlockSpec((tm, width), lambda i: (i, 0))
    ya_spec = pl.BlockSpec((A_QW, tm), lambda i: (0, i)) if ya_transposed else row(A_QW)
    return pl.pallas_call(
        functools.partial(_merge_kernel, ya_transposed),
        grid=(m // tm,),
        in_specs=[row(D_MODEL), row(S_WIDTH), row(R_WIDTH), ya_spec,
                  _full(wg.shape), _full(ps.shape), _full(pr.shape), _full(pa.shape), _full(wo.shape),
                  _full((1, D_MODEL)), _full((1, D_MODEL))],
        out_specs=row(D_MODEL),
        out_shape=jax.ShapeDtypeStruct((m, D_MODEL), F32),
        compiler_params=_cparams("parallel"),
        name="merge",
    )(x, ys, yr, ya, wg, ps, pr, pa, wo, g, b)


FFN_COLS = D_FF // 2


def _ffn_kernel(tiles_per_seq, x_ref, halo_ref, wup_ref, cw_ref, cb_ref, wdn_ref, g_ref, b_ref, o_ref):
    i = pl.program_id(0)
    x = x_ref[...]
    tm = x.shape[0]
    xe = jnp.concatenate([halo_ref[...], x], axis=0).astype(BF16)
    row = lax.broadcasted_iota(jnp.int32, (tm + SUBLANES, 1), 0)
    keep = jnp.logical_or(row >= SUBLANES, i % tiles_per_seq != 0)
    acc = jnp.zeros((tm, D_MODEL), F32)
    for c in range(D_FF // FFN_COLS):
        halves = []
        for off in (c * FFN_COLS, D_FF + c * FFN_COLS):
            up = jnp.where(keep, _bdot(xe, wup_ref[:, off:off + FFN_COLS]), 0.0)
            cv = cb_ref[:, off:off + FFN_COLS]
            for j in range(CONV_W):
                lo = SUBLANES - (CONV_W - 1) + j
                cv = cv + cw_ref[j:j + 1, off:off + FFN_COLS] * up[lo:lo + tm, :]
            halves.append(cv)
        hmid = jax.nn.gelu(halves[0]) * halves[1]
        acc = acc + _bdot(hmid, wdn_ref[c * FFN_COLS:(c + 1) * FFN_COLS, :])
    o_ref[...] = _ln(ALPHA * x + acc, g_ref[...], b_ref[...])


def _ffn(x, wup, cw, cb, wdn, g, b, seq, tm):
    m = x.shape[0]
    per8 = tm // SUBLANES
    return pl.pallas_call(
        functools.partial(_ffn_kernel, seq // tm),
        grid=(m // tm,),
        in_specs=[pl.BlockSpec((tm, D_MODEL), lambda i: (i, 0)),
                  pl.BlockSpec((SUBLANES, D_MODEL), lambda i: (jnp.maximum(i * per8 - 1, 0), 0)),
                  _full(wup.shape), _full(cw.shape), _full(cb.shape), _full(wdn.shape),
                  _full((1, D_MODEL)), _full((1, D_MODEL))],
        out_specs=pl.BlockSpec((tm, D_MODEL), lambda i: (i, 0)),
        out_shape=jax.ShapeDtypeStruct((m, D_MODEL), F32),
        compiler_params=_cparams("parallel"),
        name="ffn",
    )(x, x, wup, cw, cb, wdn, g, b)


def _ffn_step_kernel(x_ref, c0a_ref, c0b_ref, wup_ref, cw_ref, cb_ref, wdn_ref, g_ref, b_ref, o_ref, up_ref):
    x = x_ref[...]
    up = _bdot(x, wup_ref[...])
    up_ref[...] = up
    cv = cb_ref[...] + cw_ref[0:1, :] * c0a_ref[...] + cw_ref[1:2, :] * c0b_ref[...] + cw_ref[2:3, :] * up
    hmid = jax.nn.gelu(cv[:, :D_FF]) * cv[:, D_FF:]
    o_ref[...] = _ln(ALPHA * x + _bdot(hmid, wdn_ref[...]), g_ref[...], b_ref[...])


def _ffn_step(x, c0a, c0b, wup, cw, cb, wdn, g, b):
    n = x.shape[0]
    return pl.pallas_call(
        _ffn_step_kernel,
        out_shape=[jax.ShapeDtypeStruct((n, D_MODEL), F32), jax.ShapeDtypeStruct((n, 2 * D_FF), F32)],
        compiler_params=_cparams(),
        name="ffn_step",
    )(x, c0a, c0b, wup, cw, cb, wdn, g, b)


def _up_rows_kernel(x_ref, w_ref, o_ref):
    o_ref[...] = _bdot(x_ref[...], w_ref[...])


def _up_rows(x, w):
    return pl.pallas_call(
        _up_rows_kernel,
        out_shape=jax.ShapeDtypeStruct((x.shape[0], w.shape[1]), F32),
        compiler_params=_cparams(),
        name="ffn_up_tail",
    )(x, w)


def _layer_params(l, P):
    bf = lambda x: x.astype(BF16)
    row = lambda x: x.reshape(1, -1)
    w_in = P["w_in"][l]
    return dict(
        w_gate=bf(w_in[:, :O_SSM]), w_rest=bf(w_in[:, O_SSM:]),
        s5=_s5_params(P["ssm_a_re"][l], P["ssm_a_im"][l], P["ssm_log_dt"][l], P["ssm_b_re"][l],
                      P["ssm_b_im"][l], P["ssm_c_re"][l], P["ssm_c_im"][l]),
        ssm_d=row(P["ssm_d"][l]), w_glu=bf(P["ssm_w_glu"][l]), b_glu=row(P["ssm_b_glu"][l]),
        rwkv=_rwkv_weights(P["rwkv_mu"][l], P["rwkv_w0"][l], P["rwkv_w2"][l], P["rwkv_a0"][l],
                           P["rwkv_a2"][l], P["rwkv_g2"][l], P["rwkv_k_k"][l], P["rwkv_k_a"][l],
                           P["rwkv_r_k"][l]),
        lnx_g=row(P["rwkv_lnx_g"][l]), lnx_b=row(P["rwkv_lnx_b"][l]),
        proj_ssm=bf(P["proj_ssm"][l]), proj_rwkv=bf(P["proj_rwkv"][l]), proj_attn=bf(P["proj_attn"][l]),
        w_o=bf(P["w_o"][l]), ln1_g=row(P["ln1_g"][l]), ln1_b=row(P["ln1_b"][l]),
        w_up=bf(P["ffn_w_up"][l]), conv_w=P["ffn_conv_w"][l], conv_b=row(P["ffn_conv_b"][l]),
        w_down=bf(P["ffn_w_down"][l]), ln2_g=row(P["ln2_g"][l]), ln2_b=row(P["ln2_b"][l]),
    )


def _pick_tile(n, pref):
    t = min(pref, n)
    while n % t:
        t //= 2
    return t


def _prompt_trunk(x_prompt, P, layers):
    nb, seq, _ = x_prompt.shape
    m = nb * seq
    x = x_prompt.reshape(m, D_MODEL)
    tm = _pick_tile(seq, 512)
    rope = _rope_tables(jnp.arange(seq, dtype=jnp.int32))
    ln_g, ln_b = P["ln_in_g"].reshape(1, -1), P["ln_in_b"].reshape(1, -1)
    outs = []
    for l, lp in enumerate(layers):
        res = _in_proj(x, ln_g, ln_b, lp["w_rest"], rope, tm, seq // tm, pre_ln=(l == 0), attn_layouts=True)
        u, c, k, v, qt, kh, vt = res[:7]
        if l == 0:
            x = res[7]
        y_s, s_fin = _s5_prompt(u, lp["s5"], lp["ssm_d"], lp["w_glu"], lp["b_glu"],
                                jnp.zeros((nb, 2, S_LANES), F32), nb, seq, _pick_tile(seq, 256))
        r, lw, k2, vv, a, b, gate, bonus = _rwkv_pre(c, jnp.zeros((nb, R_IN), F32), lp["rwkv"], nb, seq,
                                                     _pick_tile(seq, 256), decode=False)
        y1, y2, g, f = _rwkv_chunks(r, lw, k2, vv, a, b, RWKV_CHUNK)
        y_r, h_fin = _rwkv_scan(y1, y2, g, f, bonus, gate, lp["lnx_g"], lp["lnx_b"],
                                jnp.zeros((nb, R_HEADS, R_HEAD, R_HEAD), F32), nb, seq, RWKV_CHUNK)
        y_at = _moba_prompt(qt, kh, vt, nb, seq)
        x = _merge(x, y_s, y_r, y_at, lp["w_gate"], lp["proj_ssm"], lp["proj_rwkv"], lp["proj_attn"],
                   lp["w_o"], lp["ln1_g"], lp["ln1_b"], _pick_tile(seq, 256), ya_transposed=True)
        tail = x.reshape(nb, seq, D_MODEL)[:, seq - (CONV_W - 1):].reshape(nb * (CONV_W - 1), D_MODEL)
        conv = _up_rows(tail, lp["w_up"]).reshape(nb, CONV_W - 1, 2 * D_FF)
        x = _ffn(x, lp["w_up"], lp["conv_w"], lp["conv_b"], lp["w_down"], lp["ln2_g"], lp["ln2_b"],
                 seq, _pick_tile(seq, 256))
        outs.append(dict(
            k=k.reshape(nb, seq, A_KV_HEADS, A_HEAD), v=v.reshape(nb, seq, A_KV_HEADS, A_HEAD),
            s_re=s_fin[:, 0].reshape(nb, S_GROUPS, S_STATE), s_im=s_fin[:, 1].reshape(nb, S_GROUPS, S_STATE),
            rwkv=jnp.swapaxes(h_fin, -1, -2), shift=c.reshape(nb, seq, R_IN)[:, -1], conv=conv))
    return x.reshape(nb, seq, D_MODEL), outs


def _sample_trunk(x_sample, P, layers, cache_k, cache_v, page_table, st_re, st_im, st_rwkv, st_shift, st_conv):
    nb = x_sample.shape[0]
    x = x_sample.reshape(nb, D_MODEL)
    past_len = page_table.shape[1] * PAGE_SIZE
    rope = _rope_tables(jnp.full((nb,), past_len, jnp.int32))
    ln_g, ln_b = P["ln_in_g"].reshape(1, -1), P["ln_in_b"].reshape(1, -1)
    outs = []
    for l, lp in enumerate(layers):
        res = _in_proj(x, ln_g, ln_b, lp["w_rest"], rope, nb, 1, pre_ln=(l == 0), attn_layouts=False)
        u, c, k, v, q = res[:5]
        if l == 0:
            x = res[5]
        s0 = jnp.concatenate([st_re[l].reshape(nb, S_LANES), st_im[l].reshape(nb, S_LANES)], axis=1)
        y_s, s_new = _s5_step(u, lp["s5"], lp["ssm_d"], lp["w_glu"], lp["b_glu"], s0)
        r, lw, k2, vv, a, b, gate, bonus = _rwkv_pre(c, st_shift[l], lp["rwkv"], nb, 1, nb, decode=True)
        per_head = lambda t: t.reshape(nb * R_HEADS, R_HEAD)
        y_rows, s_rwkv = _rwkv_step([per_head(t) for t in (r, lw, k2, vv, a, b)],
                                    st_rwkv[l].reshape(nb * R_HEADS, R_HEAD, R_HEAD))
        y_r = _rwkv_post_call(y_rows.reshape(nb, R_WIDTH), bonus, gate, lp["lnx_g"], lp["lnx_b"])
        y_a = _moba_decode(q, k, v, cache_k, cache_v, page_table, l)
        x = _merge(x, y_s, y_r, y_a, lp["w_gate"], lp["proj_ssm"], lp["proj_rwkv"], lp["proj_attn"],
                   lp["w_o"], lp["ln1_g"], lp["ln1_b"], nb, ya_transposed=False)
        x, up = _ffn_step(x, st_conv[l][:, 0], st_conv[l][:, 1], lp["w_up"], lp["conv_w"], lp["conv_b"],
                          lp["w_down"], lp["ln2_g"], lp["ln2_b"])
        outs.append(dict(
            k=k.reshape(nb, 1, A_KV_HEADS, A_HEAD), v=v.reshape(nb, 1, A_KV_HEADS, A_HEAD),
            s_re=s_new[:, :S_LANES].reshape(nb, S_GROUPS, S_STATE),
            s_im=s_new[:, S_LANES:].reshape(nb, S_GROUPS, S_STATE),
            rwkv=s_rwkv.reshape(nb, R_HEADS, R_HEAD, R_HEAD), shift=c,
            conv=jnp.stack([st_conv[l][:, 1], up], axis=1)))
    return x.reshape(nb, 1, D_MODEL), outs


def kernel(x_prompt, x_sample, cache_k, cache_v, page_table, state_ssm_re, state_ssm_im, state_rwkv, state_rwkv_shift, state_conv, ln_in_g, ln_in_b, w_in, ssm_a_re, ssm_a_im, ssm_log_dt, ssm_b_re, ssm_b_im, ssm_c_re, ssm_c_im, ssm_d, ssm_w_glu, ssm_b_glu, rwkv_mu, rwkv_w0, rwkv_w2, rwkv_a0, rwkv_a2, rwkv_g2, rwkv_k_k, rwkv_k_a, rwkv_r_k, rwkv_lnx_g, rwkv_lnx_b, proj_ssm, proj_rwkv, proj_attn, w_o, ln1_g, ln1_b, ffn_w_up, ffn_conv_w, ffn_conv_b, ffn_w_down, ln2_g, ln2_b):
    P = dict(ln_in_g=ln_in_g, ln_in_b=ln_in_b, w_in=w_in,
             ssm_a_re=ssm_a_re, ssm_a_im=ssm_a_im, ssm_log_dt=ssm_log_dt,
             ssm_b_re=ssm_b_re, ssm_b_im=ssm_b_im, ssm_c_re=ssm_c_re, ssm_c_im=ssm_c_im,
             ssm_d=ssm_d, ssm_w_glu=ssm_w_glu, ssm_b_glu=ssm_b_glu,
             rwkv_mu=rwkv_mu, rwkv_w0=rwkv_w0, rwkv_w2=rwkv_w2, rwkv_a0=rwkv_a0, rwkv_a2=rwkv_a2,
             rwkv_g2=rwkv_g2, rwkv_k_k=rwkv_k_k, rwkv_k_a=rwkv_k_a, rwkv_r_k=rwkv_r_k,
             rwkv_lnx_g=rwkv_lnx_g, rwkv_lnx_b=rwkv_lnx_b,
             proj_ssm=proj_ssm, proj_rwkv=proj_rwkv, proj_attn=proj_attn, w_o=w_o,
             ln1_g=ln1_g, ln1_b=ln1_b, ffn_w_up=ffn_w_up, ffn_conv_w=ffn_conv_w,
             ffn_conv_b=ffn_conv_b, ffn_w_down=ffn_w_down, ln2_g=ln2_g, ln2_b=ln2_b)
    layers = [_layer_params(l, P) for l in range(w_in.shape[0])]
    y_p, op = _prompt_trunk(x_prompt, P, layers)
    y_s, os_ = _sample_trunk(x_sample, P, layers, cache_k, cache_v, page_table, state_ssm_re, state_ssm_im,
                             state_rwkv, state_rwkv_shift, state_conv)
    st = lambda outs, key: jnp.stack([o[key] for o in outs])
    return (y_p, y_s, st(op, "k"), st(op, "v"), st(os_, "k"), st(os_, "v"),
            st(op, "s_re"), st(op, "s_im"), st(os_, "s_re"), st(os_, "s_im"),
            st(op, "rwkv"), st(os_, "rwkv"), st(op, "shift"), st(os_, "shift"),
            st(op, "conv"), st(os_, "conv"))
```

```python
import functools
import math

import jax
import jax.numpy as jnp
from jax import lax
from jax.experimental import pallas as pl
from jax.experimental.pallas import tpu as pltpu

F32 = jnp.float32
BF16 = jnp.bfloat16

D_MODEL = 1024
PAGE_SIZE = 128
S_GROUP = 16
S_GROUPS = 16
S_STATE = 64
S_WIDTH = S_GROUPS * S_GROUP
S_LANES = S_GROUPS * S_STATE
R_HEAD = 64
R_HEADS = 4
R_WIDTH = R_HEADS * R_HEAD
W_LORA = 32
A_LORA = 32
G_LORA = 64
N_LORA = W_LORA + A_LORA + G_LORA
R_IN = 3 * R_WIDTH + N_LORA
GN_EPS = 64e-5
A_HEAD = 64
A_HEADS = 8
A_KV_HEADS = 4
A_GROUP = A_HEADS // A_KV_HEADS
A_QW = A_HEADS * A_HEAD
A_KVW = A_KV_HEADS * A_HEAD
ROT_DIM = A_HEAD // 4
ROPE_THETA = 500000.0
MOBA_BLOCK = 256
MOBA_TOPK = 3
N_BRANCH = 3
O_SSM = N_BRANCH * D_MODEL
N_REST = S_WIDTH + R_IN + A_QW + 2 * A_KVW
D_FF = 2816
CONV_W = 3
DEPTH = 2
ALPHA = (2 * DEPTH) ** 0.25
LN_EPS = 1e-5
ATT_SCALE = A_HEAD ** -0.5

VMEM_LIMIT_BYTES = 56 * 1024 * 1024
LANES = 128
SUBLANES = 8

RWKV_CHUNK = 64
NEG_INF = float("-inf")

_NN = (((1,), (0,)), ((), ()))
_NT = (((1,), (1,)), ((), ()))
_TN = (((0,), (0,)), ((), ()))


def _cparams(*sem):
    return pltpu.CompilerParams(dimension_semantics=sem or None, vmem_limit_bytes=VMEM_LIMIT_BYTES)


def _bdot(a, b, dims=_NN):
    return lax.dot_general(a.astype(BF16), b.astype(BF16), dims, preferred_element_type=F32)


def _dot3(a, b, dims=_NN):
    ah = a.astype(BF16)
    al = (a - ah.astype(F32)).astype(BF16)
    bh = b.astype(BF16)
    bl = (b - bh.astype(F32)).astype(BF16)
    f = lambda x, y: lax.dot_general(x, y, dims, preferred_element_type=F32)
    return f(ah, bh) + f(ah, bl) + f(al, bh)


def _ln(x, g, b):
    mu = jnp.mean(x, -1, keepdims=True)
    xc = x - mu
    var = jnp.mean(xc * xc, -1, keepdims=True)
    return xc * lax.rsqrt(var + LN_EPS) * g + b


def _head_sum(x, head):
    lane_head = lax.broadcasted_iota(jnp.int32, (1, x.shape[1]), 1) // head
    out = jnp.zeros_like(x)
    for h in range(x.shape[1] // head):
        m = lane_head == h
        s = jnp.sum(jnp.where(m, x, 0.0), -1, keepdims=True)
        out = jnp.where(m, s, out)
    return out


def _full(shape):
    n = len(shape)
    return pl.BlockSpec(shape, lambda *_: (0,) * n)


def _in_proj_kernel(pre_ln, attn_layouts, x_ref, g_ref, b_ref, w_ref, cos_ref, sa_ref, sb_ref,
                    u_ref, c_ref, k_ref, v_ref, *rest):
    x = x_ref[...]
    if pre_ln:
        x = _ln(x, g_ref[...], b_ref[...])
        rest[-1][...] = x
    h = _bdot(x, w_ref[...])
    u_ref[...] = h[:, :S_WIDTH]
    c_ref[...] = h[:, S_WIDTH:S_WIDTH + R_IN]
    cos, sa, sb = cos_ref[...], sa_ref[...], sb_ref[...]
    base = S_WIDTH + R_IN
    nq = A_QW // LANES
    per = LANES // A_HEAD
    for j in range((A_QW + A_KVW) // LANES):
        ch = h[:, base + LANES * j: base + LANES * (j + 1)]
        rot = (ch * cos + pltpu.roll(ch, LANES - ROT_DIM // 2, 1) * sa
               + pltpu.roll(ch, ROT_DIM // 2, 1) * sb)
        if j < nq:
            if attn_layouts:
                rest[0][LANES * j:LANES * (j + 1), :] = rot.T
            else:
                rest[0][:, LANES * j:LANES * (j + 1)] = rot
        else:
            k_ref[:, LANES * (j - nq):LANES * (j - nq + 1)] = rot
            if attn_layouts:
                for i in range(per):
                    rest[1][per * (j - nq) + i] = rot[:, A_HEAD * i:A_HEAD * (i + 1)]
    v = h[:, base + A_QW + A_KVW:]
    v_ref[...] = v
    if attn_layouts:
        for j in range(A_KVW // LANES):
            rest[2][LANES * j:LANES * (j + 1), :] = v[:, LANES * j:LANES * (j + 1)].T


def _in_proj(x, ln_g, ln_b, w, rope, tm, pos_tiles, pre_ln, attn_layouts):
    m = x.shape[0]
    row = lambda width: pl.BlockSpec((tm, width), lambda i: (i, 0))
    col = lambda width: pl.BlockSpec((width, tm), lambda i: (0, i))
    tab = pl.BlockSpec((tm, LANES), lambda i: (i % pos_tiles, 0))
    f32 = lambda *shape: jax.ShapeDtypeStruct(shape, F32)
    specs = [row(S_WIDTH), row(R_IN), row(A_KVW), row(A_KVW)]
    shapes = [f32(m, S_WIDTH), f32(m, R_IN), f32(m, A_KVW), f32(m, A_KVW)]
    if attn_layouts:
        specs += [col(A_QW), pl.BlockSpec((A_KV_HEADS, tm, A_HEAD), lambda i: (0, i, 0)), col(A_KVW)]
        shapes += [f32(A_QW, m), f32(A_KV_HEADS, m, A_HEAD), f32(A_KVW, m)]
    else:
        specs += [row(A_QW)]
        shapes += [f32(m, A_QW)]
    if pre_ln:
        specs += [row(D_MODEL)]
        shapes += [f32(m, D_MODEL)]
    return pl.pallas_call(
        functools.partial(_in_proj_kernel, pre_ln, attn_layouts),
        grid=(m // tm,),
        in_specs=[row(D_MODEL), _full((1, D_MODEL)), _full((1, D_MODEL)), _full((D_MODEL, N_REST)),
                  tab, tab, tab],
        out_specs=specs,
        out_shape=shapes,
        compiler_params=_cparams("parallel"),
        name="in_proj",
    )(x, ln_g, ln_b, w, *rope)


def _rope_tables(pos):
    half = ROT_DIM // 2
    inv = ROPE_THETA ** (-jnp.arange(half, dtype=F32) / half)
    ang = pos.astype(F32)[:, None] * inv[None, :]
    cos, sin = jnp.cos(ang), jnp.sin(ang)
    n = pos.shape[0]
    pad = jnp.zeros((n, A_HEAD - ROT_DIM), F32)
    zero = jnp.zeros((n, half), F32)
    cos_h = jnp.concatenate([cos, cos, pad + 1.0], -1)
    sa_h = jnp.concatenate([-sin, zero, pad], -1)
    sb_h = jnp.concatenate([zero, sin, pad], -1)
    rep = LANES // A_HEAD
    return tuple(jnp.tile(t, (1, rep)) for t in (cos_h, sa_h, sb_h))


def _s5_params(a_re, a_im, log_dt, b_re, b_im, c_re, c_im):
    dt = jnp.exp(log_dt)[:, None]
    lam_re, lam_im = a_re * dt, a_im * dt

    def power(k):
        mag = jnp.exp(lam_re * k)
        return (mag * jnp.cos(lam_im * k)).reshape(-1), (mag * jnp.sin(lam_im * k)).reshape(-1)

    abar_re, abar_im = power(1.0)
    den = (a_re * a_re + a_im * a_im).reshape(-1)
    ar, ai = a_re.reshape(-1), a_im.reshape(-1)
    em_re = abar_re - 1.0
    coef = jnp.stack([(em_re * ar + abar_im * ai) / den, (abar_im * ar - em_re * ai) / den])
    pw = jnp.stack([jnp.stack(x) for x in zip(*[power(float(k)) for k in range(1, SUBLANES + 1)])])
    dbl = jnp.stack([jnp.stack(x) for x in zip(*[power(float(k)) for k in (1, 2, 4)])])
    eye = jnp.eye(S_GROUPS, dtype=F32)
    wb = jnp.concatenate([jnp.einsum("gpc,gh->gchp", b, eye).reshape(S_WIDTH, S_LANES)
                          for b in (b_re, b_im)], axis=1)
    wc = jnp.concatenate([jnp.einsum("gcp,gh->gphc", c, eye).reshape(S_LANES, S_WIDTH)
                          for c in (c_re, -c_im)], axis=0)
    return dict(coef=coef, pw=pw, dbl=dbl, abar=jnp.stack([abar_re, abar_im]),
                wb=wb.astype(BF16), wc=wc.astype(BF16))


def _s5_glu(s_re, s_im, u, wc_ref, d_ref, wg_ref, bg_ref):
    s = jnp.concatenate([s_re, s_im], axis=1)
    y = _bdot(s, wc_ref[...]) + d_ref[...] * u
    z = jax.nn.gelu(y)
    return z * jax.nn.sigmoid(_bdot(z, wg_ref[...]) + bg_ref[...])


def _s5_kernel(u_ref, wb_ref, wc_ref, coef_ref, pw_ref, dbl_ref, d_ref, wg_ref, bg_ref, s0_ref,
               y_ref, sfin_ref, s_scr, carry_scr):
    j = pl.program_id(1)
    t = u_ref.shape[0]

    @pl.when(j == 0)
    def _():
        carry_scr[...] = s0_ref[0]

    u = u_ref[...]
    bu = _bdot(u, wb_ref[...])
    b_re, b_im = bu[:, :S_LANES], bu[:, S_LANES:]
    c_re, c_im = coef_ref[0:1, :], coef_ref[1:2, :]
    s_scr[0] = c_re * b_re - c_im * b_im
    s_scr[1] = c_re * b_im + c_im * b_re
    row = lax.broadcasted_iota(jnp.int32, (SUBLANES, S_LANES), 0)

    def body(g, carry):
        cr, ci = carry
        off = pl.multiple_of(g * SUBLANES, SUBLANES)
        sr = s_scr[0, pl.ds(off, SUBLANES), :]
        si = s_scr[1, pl.ds(off, SUBLANES), :]
        for n, d in enumerate((1, 2, 4)):
            ar, ai = dbl_ref[0, n:n + 1, :], dbl_ref[1, n:n + 1, :]
            pr = jnp.where(row >= d, pltpu.roll(sr, d, 0), 0.0)
            pi = jnp.where(row >= d, pltpu.roll(si, d, 0), 0.0)
            sr, si = sr + ar * pr - ai * pi, si + ar * pi + ai * pr
        p_re, p_im = pw_ref[0], pw_ref[1]
        sr, si = sr + p_re * cr - p_im * ci, si + p_re * ci + p_im * cr
        s_scr[0, pl.ds(off, SUBLANES), :] = sr
        s_scr[1, pl.ds(off, SUBLANES), :] = si
        return sr[SUBLANES - 1:SUBLANES, :], si[SUBLANES - 1:SUBLANES, :]

    cr, ci = lax.fori_loop(0, t // SUBLANES, body, (carry_scr[0:1, :], carry_scr[1:2, :]), unroll=2)
    carry_scr[0:1, :] = cr
    carry_scr[1:2, :] = ci
    y_ref[...] = _s5_glu(s_scr[0], s_scr[1], u, wc_ref, d_ref, wg_ref, bg_ref)

    @pl.when(j == pl.num_programs(1) - 1)
    def _():
        sfin_ref[0] = carry_scr[...]


def _s5_prompt(u, sp, d, w_glu, b_glu, s0, nb, seq, t):
    nt = seq // t
    return pl.pallas_call(
        _s5_kernel,
        grid=(nb, nt),
        in_specs=[pl.BlockSpec((t, S_WIDTH), lambda b, j: (b * nt + j, 0)),
                  _full((S_WIDTH, 2 * S_LANES)), _full((2 * S_LANES, S_WIDTH)), _full((2, S_LANES)),
                  _full((2, SUBLANES, S_LANES)), _full((2, 3, S_LANES)), _full((1, S_WIDTH)),
                  _full((S_WIDTH, S_WIDTH)), _full((1, S_WIDTH)),
                  pl.BlockSpec((1, 2, S_LANES), lambda b, j: (b, 0, 0))],
        out_specs=[pl.BlockSpec((t, S_WIDTH), lambda b, j: (b * nt + j, 0)),
                   pl.BlockSpec((1, 2, S_LANES), lambda b, j: (b, 0, 0))],
        out_shape=[jax.ShapeDtypeStruct((nb * seq, S_WIDTH), F32),
                   jax.ShapeDtypeStruct((nb, 2, S_LANES), F32)],
        scratch_shapes=[pltpu.VMEM((2, t, S_LANES), F32), pltpu.VMEM((2, S_LANES), F32)],
        compiler_params=_cparams("parallel", "arbitrary"),
        name="s5_scan",
    )(u, sp["wb"], sp["wc"], sp["coef"], sp["pw"], sp["dbl"], d, w_glu, b_glu, s0)


def _s5_step_kernel(u_ref, wb_ref, wc_ref, coef_ref, abar_ref, d_ref, wg_ref, bg_ref, s0_ref,
                    y_ref, s_ref):
    u = u_ref[...]
    bu = _bdot(u, wb_ref[...])
    b_re, b_im = bu[:, :S_LANES], bu[:, S_LANES:]
    c_re, c_im = coef_ref[0:1, :], coef_ref[1:2, :]
    a_re, a_im = abar_ref[0:1, :], abar_ref[1:2, :]
    s0_re, s0_im = s0_ref[:, :S_LANES], s0_ref[:, S_LANES:]
    s_re = c_re * b_re - c_im * b_im + (a_re * s0_re - a_im * s0_im)
    s_im = c_re * b_im + c_im * b_re + (a_re * s0_im + a_im * s0_re)
    s_ref[:, :S_LANES] = s_re
    s_ref[:, S_LANES:] = s_im
    y_ref[...] = _s5_glu(s_re, s_im, u, wc_ref, d_ref, wg_ref, bg_ref)


def _s5_step(u, sp, d, w_glu, b_glu, s0):
    n = u.shape[0]
    return pl.pallas_call(
        _s5_step_kernel,
        out_shape=[jax.ShapeDtypeStruct((n, S_WIDTH), F32), jax.ShapeDtypeStruct((n, 2 * S_LANES), F32)],
        compiler_params=_cparams(),
        name="s5_step",
    )(u, sp["wb"], sp["wc"], sp["coef"], sp["abar"], d, w_glu, b_glu, s0)


def _rwkv_pre_kernel(decode, c_ref, prev_ref, sh0_ref, mu_ref, w0_ref, w2_ref, a0_ref, a2_ref, g2_ref,
                     kk_ref, ka_ref, rk_ref,
                     r_o, lw_o, k_o, v_o, a_o, b_o, g_o, bonus_o):
    c = c_ref[...]
    if decode:
        prev = sh0_ref[...]
    else:
        j = pl.program_id(1)
        last = jnp.where(j == 0, sh0_ref[0], prev_ref[SUBLANES - 1:SUBLANES, :])
        row = lax.broadcasted_iota(jnp.int32, c.shape, 0)
        prev = jnp.where(row == 0, last, pltpu.roll(c, 1, 0))
    cf = c + (prev - c) * mu_ref[...]
    r = cf[:, :R_WIDTH]
    k = cf[:, R_WIDTH:2 * R_WIDTH]
    v = cf[:, 2 * R_WIDTH:3 * R_WIDTH]
    lora = cf[:, 3 * R_WIDTH:]
    w_log = -jax.nn.softplus(-(w0_ref[...] + _bdot(jnp.tanh(lora), w2_ref[...]))) - 0.5
    a = jax.nn.sigmoid(a0_ref[...] + _bdot(lora, a2_ref[...]))
    g = _bdot(jax.nn.sigmoid(lora), g2_ref[...])
    kk = k * kk_ref[...]
    kk = kk * lax.rsqrt(jnp.maximum(_head_sum(kk * kk, R_HEAD), 1e-24))
    k = k * (1.0 + (a - 1.0) * ka_ref[...])
    r_o[...] = r
    lw_o[...] = -jnp.exp(w_log)
    k_o[...] = k
    v_o[...] = v
    a_o[...] = -kk
    b_o[...] = kk * a
    g_o[...] = g
    bonus_o[...] = _head_sum(r * k * rk_ref[...], R_HEAD) * v


def _rwkv_weights(mu, w0, w2, a0, a2, g2, k_k, k_a, r_k):
    z = lambda n: jnp.zeros((n, R_WIDTH), F32)
    w2p = jnp.concatenate([w2, z(A_LORA + G_LORA)], 0).astype(BF16)
    a2p = jnp.concatenate([z(W_LORA), a2, z(G_LORA)], 0).astype(BF16)
    g2p = jnp.concatenate([z(W_LORA + A_LORA), g2], 0).astype(BF16)
    row = lambda x: x.reshape(1, -1)
    return [row(mu), row(w0), w2p, row(a0), a2p, g2p, row(k_k), row(k_a), row(r_k)]


def _rwkv_pre(c, sh0, wts, nb, seq, t, decode):
    m = c.shape[0]
    outs = [jax.ShapeDtypeStruct((m, R_WIDTH), F32)] * 8
    wspecs = [_full(w.shape) for w in wts]
    if decode:
        return pl.pallas_call(
            functools.partial(_rwkv_pre_kernel, True),
            grid=(1,),
            in_specs=[_full((m, R_IN)), _full((SUBLANES, R_IN)), _full((m, R_IN))] + wspecs,
            out_specs=[_full((m, R_WIDTH))] * 8,
            out_shape=outs,
            compiler_params=_cparams("arbitrary"),
            name="rwkv_pre_step",
        )(c, c[:SUBLANES], sh0, *wts)
    nt = seq // t
    per8 = t // SUBLANES
    return pl.pallas_call(
        functools.partial(_rwkv_pre_kernel, False),
        grid=(nb, nt),
        in_specs=[pl.BlockSpec((t, R_IN), lambda b, j: (b * nt + j, 0)),
                  pl.BlockSpec((SUBLANES, R_IN), lambda b, j: (jnp.maximum((b * nt + j) * per8 - 1, 0), 0)),
                  pl.BlockSpec((1, 1, R_IN), lambda b, j: (b, 0, 0))] + wspecs,
        out_specs=[pl.BlockSpec((t, R_WIDTH), lambda b, j: (b * nt + j, 0))] * 8,
        out_shape=outs,
        compiler_params=_cparams("parallel", "parallel"),
        name="rwkv_pre",
    )(c, c, sh0.reshape(nb, 1, R_IN), *wts)


def _rwkv_chunk_kernel(r_ref, lw_ref, k_ref, v_ref, a_ref, b_ref, y1_ref, y2_ref, g_ref, f_ref):
    t = r_ref.shape[0]
    lw = lw_ref[...]
    row = lax.broadcasted_iota(jnp.int32, lw.shape, 0)
    cw = lw
    d = 1
    while d < t:
        cw = cw + jnp.where(row >= d, pltpu.roll(cw, d, 0), 0.0)
        d *= 2
    cw_end = cw[t - 1:t, :]
    e_neg = jnp.exp(-cw)
    e_rem = jnp.exp(cw_end - cw)
    a_t = a_ref[...] * jnp.exp(cw - lw)
    r_t = r_ref[...] * jnp.exp(cw)
    b_t = b_ref[...] * e_neg
    k_t = k_ref[...] * e_neg
    b_e = b_ref[...] * e_rem
    k_e = k_ref[...] * e_rem
    e_end = jnp.exp(cw_end)
    v = v_ref[...]
    ri = lax.broadcasted_iota(jnp.int32, (t, t), 0)
    ci = lax.broadcasted_iota(jnp.int32, (t, t), 1)
    hi = lax.broadcasted_iota(jnp.int32, (R_HEAD, R_HEAD), 0)
    hj = lax.broadcasted_iota(jnp.int32, (R_HEAD, R_HEAD), 1)
    heads = range(R_HEADS)
    per_head = lambda x: [x[:, h * R_HEAD:(h + 1) * R_HEAD] for h in heads]
    ah, rh, bh, kh, vh = per_head(a_t), per_head(r_t), per_head(b_t), per_head(k_t), per_head(v)
    a_ab = [jnp.where(ri > ci, _dot3(ah[h], bh[h], _NT), 0.0) for h in heads]
    a_ak = [jnp.where(ri > ci, _dot3(ah[h], kh[h], _NT), 0.0) for h in heads]
    a_rb = [jnp.where(ri >= ci, _dot3(rh[h], bh[h], _NT), 0.0) for h in heads]
    a_rk = [jnp.where(ri >= ci, _dot3(rh[h], kh[h], _NT), 0.0) for h in heads]
    akv = [_dot3(a_ak[h], vh[h]) for h in heads]
    rkv = [_dot3(a_rk[h], vh[h]) for h in heads]
    inv = [jnp.where(ri == ci, 1.0, 0.0) + a_ab[h] for h in heads]
    apow = a_ab
    n = 2
    while n < t:
        apow = [_dot3(apow[h], apow[h]) for h in heads]
        inv = [inv[h] + _dot3(apow[h], inv[h]) for h in heads]
        n *= 2
    w1 = [_dot3(inv[h], ah[h]) for h in heads]
    w2 = [_dot3(inv[h], akv[h]) for h in heads]
    be_t = [x.T for x in per_head(b_e)]
    ke_t = [x.T for x in per_head(k_e)]
    ee = per_head(e_end)
    for h in heads:
        y1_ref[0, h] = rh[h] + _dot3(a_rb[h], w1[h])
        y2_ref[0, h] = _dot3(a_rb[h], w2[h]) + rkv[h]
        g_ref[0, h] = jnp.where(hi == hj, ee[h], 0.0) + _dot3(be_t[h], w1[h])
        f_ref[0, h] = _dot3(be_t[h], w2[h]) + _dot3(ke_t[h], vh[h])


def _rwkv_chunks(r, lw, k, v, a, b, t):
    m = r.shape[0]
    nc = m // t
    spec = pl.BlockSpec((t, R_WIDTH), lambda i: (i, 0))
    yspec = pl.BlockSpec((1, R_HEADS, t, R_HEAD), lambda i: (i, 0, 0, 0))
    gspec = pl.BlockSpec((1, R_HEADS, R_HEAD, R_HEAD), lambda i: (i, 0, 0, 0))
    return pl.pallas_call(
        _rwkv_chunk_kernel,
        grid=(nc,),
        in_specs=[spec] * 6,
        out_specs=[yspec, yspec, gspec, gspec],
        out_shape=[jax.ShapeDtypeStruct((nc, R_HEADS, t, R_HEAD), F32)] * 2
        + [jax.ShapeDtypeStruct((nc, R_HEADS, R_HEAD, R_HEAD), F32)] * 2,
        compiler_params=_cparams("parallel"),
        name="rwkv_chunk",
    )(r, lw, k, v, a, b)


def _rwkv_post(y, bonus, g, lng, lnb):
    m = _head_sum(y, R_HEAD) * (1.0 / R_HEAD)
    yc = y - m
    var = _head_sum(yc * yc, R_HEAD) * (1.0 / R_HEAD)
    return (yc * lax.rsqrt(var + GN_EPS) * lng + lnb + bonus) * g


def _rwkv_scan_kernel(y1_ref, y2_ref, g_ref, f_ref, bonus_ref, gate_ref, lng_ref, lnb_ref, h0_ref,
                      y_ref, hfin_ref, h_scr, y_scr):
    j = pl.program_id(1)

    @pl.when(j == 0)
    def _():
        h_scr[...] = h0_ref[0]

    for h in range(R_HEADS):
        state = h_scr[h]
        y_scr[:, h * R_HEAD:(h + 1) * R_HEAD] = _dot3(y1_ref[0, h], state) + y2_ref[0, h]
        h_scr[h] = _dot3(g_ref[0, h], state) + f_ref[0, h]
    y_ref[...] = _rwkv_post(y_scr[...], bonus_ref[...], gate_ref[...], lng_ref[...], lnb_ref[...])

    @pl.when(j == pl.num_programs(1) - 1)
    def _():
        hfin_ref[0] = h_scr[...]


def _rwkv_scan(y1, y2, g, f, bonus, gate, lng, lnb, h0, nb, seq, t):
    nc = seq // t
    yspec = pl.BlockSpec((1, R_HEADS, t, R_HEAD), lambda b, j: (b * nc + j, 0, 0, 0))
    gspec = pl.BlockSpec((1, R_HEADS, R_HEAD, R_HEAD), lambda b, j: (b * nc + j, 0, 0, 0))
    row = pl.BlockSpec((t, R_WIDTH), lambda b, j: (b * nc + j, 0))
    hspec = pl.BlockSpec((1, R_HEADS, R_HEAD, R_HEAD), lambda b, j: (b, 0, 0, 0))
    return pl.pallas_call(
        _rwkv_scan_kernel,
        grid=(nb, nc),
        in_specs=[yspec, yspec, gspec, gspec, row, row, _full((1, R_WIDTH)), _full((1, R_WIDTH)), hspec],
        out_specs=[row, hspec],
        out_shape=[jax.ShapeDtypeStruct((nb * seq, R_WIDTH), F32),
                   jax.ShapeDtypeStruct((nb, R_HEADS, R_HEAD, R_HEAD), F32)],
        scratch_shapes=[pltpu.VMEM((R_HEADS, R_HEAD, R_HEAD), F32), pltpu.VMEM((t, R_WIDTH), F32)],
        compiler_params=_cparams("parallel", "arbitrary"),
        name="rwkv_scan",
    )(y1, y2, g, f, bonus, gate, lng, lnb, h0)


RWKV_STEP_ROWS = 8


def _rwkv_step_kernel(r_ref, lw_ref, k_ref, v_ref, a_ref, b_ref, s_ref, y_ref, so_ref):
    ii = lax.broadcasted_iota(jnp.int32, (R_HEAD, R_HEAD), 0)
    jj = lax.broadcasted_iota(jnp.int32, (R_HEAD, R_HEAD), 1)
    eye = ii == jj
    for n in range(RWKV_STEP_ROWS):
        one = lambda ref: ref[n:n + 1, :]
        s = s_ref[n]
        sa = jnp.sum(s * one(a_ref), -1, keepdims=True)
        v_col = jnp.sum(jnp.where(eye, one(v_ref), 0.0), -1, keepdims=True)
        s_new = s * jnp.exp(one(lw_ref)) + sa * one(b_ref) + v_col * one(k_ref)
        y_col = jnp.sum(s_new * one(r_ref), -1, keepdims=True)
        so_ref[n] = s_new
        y_ref[n:n + 1, :] = jnp.sum(jnp.where(eye, y_col, 0.0), 0, keepdims=True)


def _rwkv_step(vecs, s0):
    n = s0.shape[0]
    vspec = pl.BlockSpec((RWKV_STEP_ROWS, R_HEAD), lambda i: (i, 0))
    sspec = pl.BlockSpec((RWKV_STEP_ROWS, R_HEAD, R_HEAD), lambda i: (i, 0, 0))
    return pl.pallas_call(
        _rwkv_step_kernel,
        grid=(n // RWKV_STEP_ROWS,),
        in_specs=[vspec] * 6 + [sspec],
        out_specs=[vspec, sspec],
        out_shape=[jax.ShapeDtypeStruct((n, R_HEAD), F32), jax.ShapeDtypeStruct((n, R_HEAD, R_HEAD), F32)],
        compiler_params=_cparams("parallel"),
        name="rwkv_step",
    )(*vecs, s0)


def _rwkv_post_kernel(y_ref, bonus_ref, gate_ref, lng_ref, lnb_ref, o_ref):
    o_ref[...] = _rwkv_post(y_ref[...], bonus_ref[...], gate_ref[...], lng_ref[...], lnb_ref[...])


def _rwkv_post_call(y, bonus, gate, lng, lnb):
    return pl.pallas_call(
        _rwkv_post_kernel,
        out_shape=jax.ShapeDtypeStruct(y.shape, F32),
        compiler_params=_cparams(),
        name="rwkv_post_step",
    )(y, bonus, gate, lng, lnb)


def _moba_kernel(qt_ref, k_ref, vt_ref, o_ref, km_scr, bias_scr):
    qi = pl.program_id(2)
    nblk = km_scr.shape[0]
    cols = A_GROUP * MOBA_BLOCK

    @pl.when(qi == 0)
    def _():
        km_scr[...] = jnp.mean(k_ref[0].reshape(nblk, MOBA_BLOCK, A_HEAD), axis=1)

    qt = jnp.concatenate([qt_ref[g * A_HEAD:(g + 1) * A_HEAD, :] for g in range(A_GROUP)], axis=1)
    qb = (qt * ATT_SCALE).astype(BF16)
    blk = lax.broadcasted_iota(jnp.int32, (nblk, cols), 0)
    gate = jnp.where(blk < qi, _bdot(km_scr[...], qb), NEG_INF)
    sel = jnp.zeros(gate.shape, jnp.bool_)
    for _ in range(MOBA_TOPK):
        m = jnp.max(gate, 0, keepdims=True)
        idx = jnp.min(jnp.where(gate == m, blk, nblk), 0, keepdims=True)
        pick = (blk == idx) & (m > NEG_INF)
        sel = sel | pick
        gate = jnp.where(pick, NEG_INF, gate)
    bias_scr[...] = jnp.where(sel, 0.0, NEG_INF)

    def scores(n):
        off = pl.multiple_of(n * MOBA_BLOCK, MOBA_BLOCK)
        return _bdot(k_ref[0, pl.ds(off, MOBA_BLOCK), :], qb)

    def values_t(n):
        off = pl.multiple_of(n * MOBA_BLOCK, MOBA_BLOCK)
        return vt_ref[:, pl.ds(off, MOBA_BLOCK)].astype(BF16)

    s, vtb = scores(qi), values_t(qi)
    key = lax.broadcasted_iota(jnp.int32, s.shape, 0)
    pos = lax.broadcasted_iota(jnp.int32, s.shape, 1) % MOBA_BLOCK
    s = jnp.where(key <= pos, s, NEG_INF)
    m0 = jnp.max(s, 0, keepdims=True)
    p = jnp.exp(s - m0)
    l0 = jnp.sum(p, 0, keepdims=True)
    acc0 = _bdot(vtb, p)

    def softmax_step(n, s, m, l):
        s = s + bias_scr[pl.ds(n, 1), :]
        m_new = jnp.maximum(m, jnp.max(s, 0, keepdims=True))
        alpha = jnp.exp(m - m_new)
        p = jnp.exp(s - m_new)
        return m_new, alpha * l + jnp.sum(p, 0, keepdims=True), alpha, p.astype(BF16)

    def body(j, carry):
        m, l, acc, s_even, p_prev = carry
        n0 = 2 * j
        pv_prev = _bdot(values_t(jnp.maximum(n0 - 1, 0)), p_prev)
        s_odd = scores(n0 + 1)
        m, l, alpha, p_even = softmax_step(n0, s_even, m, l)
        acc = alpha * (acc + pv_prev) + _bdot(values_t(n0), p_even)
        s_even = scores(jnp.minimum(n0 + 2, nblk - 1))
        m, l, alpha, p_odd = softmax_step(n0 + 1, s_odd, m, l)
        return m, l, alpha * acc, s_even, p_odd

    trips = lax.shift_right_logical(qi + 1, 1)
    init = (m0, l0, acc0, scores(0), jnp.zeros(s.shape, BF16))
    _, l, acc, _, p_last = lax.fori_loop(0, trips, body, init)
    acc = acc + _bdot(values_t(jnp.maximum(2 * trips - 1, 0)), p_last)
    o = acc / l
    for g in range(A_GROUP):
        o_ref[g * A_HEAD:(g + 1) * A_HEAD, :] = o[:, g * MOBA_BLOCK:(g + 1) * MOBA_BLOCK]


def _moba_prompt(qt, kh, vt, nb, seq):
    nblk = seq // MOBA_BLOCK
    qspec = pl.BlockSpec((A_GROUP * A_HEAD, MOBA_BLOCK), lambda b, h, i: (h, b * nblk + i))
    return pl.pallas_call(
        _moba_kernel,
        grid=(nb, A_KV_HEADS, nblk),
        in_specs=[qspec,
                  pl.BlockSpec((1, seq, A_HEAD), lambda b, h, i: (h, b, 0)),
                  pl.BlockSpec((A_HEAD, seq), lambda b, h, i: (h, b))],
        out_specs=qspec,
        out_shape=jax.ShapeDtypeStruct(qt.shape, F32),
        scratch_shapes=[pltpu.VMEM((nblk, A_HEAD), F32), pltpu.VMEM((nblk, A_GROUP * MOBA_BLOCK), F32)],
        compiler_params=_cparams("parallel", "parallel", "arbitrary"),
        name="moba_prompt",
    )(qt, kh, vt)


def _pages_t(cache):
    return cache.transpose(0, 1, 3, 4, 2).reshape(-1, A_KVW, PAGE_SIZE)


def _kmeans_kernel(n_in, pt_ref, *refs):
    pages, o_ref = refs[:n_in], refs[n_in]
    j = pl.program_id(1)
    per_blk = MOBA_BLOCK // PAGE_SIZE
    blocks = n_in // per_blk

    @pl.when(j == 0)
    def _():
        o_ref[...] = jnp.zeros(o_ref.shape, F32)

    out = o_ref[0]
    lane = lax.broadcasted_iota(jnp.int32, out.shape, 1)
    for i in range(blocks):
        tot = sum(pages[per_blk * i + p][0] for p in range(per_blk))
        mean = jnp.sum(tot, axis=1, keepdims=True) * (1.0 / MOBA_BLOCK)
        out = jnp.where(lane == j * blocks + i, mean, out)
    o_ref[0] = out


def _decode_kmeans(pages, page_table, layer, n_pool):
    nb, n_pages = page_table.shape
    n_in = min(16, n_pages)
    steps = n_pages // n_in
    per_blk = MOBA_BLOCK // PAGE_SIZE

    def page_spec(i):
        return pl.BlockSpec((1, A_KVW, PAGE_SIZE),
                            lambda b, j, pt: (layer * n_pool + pt[b * n_pages + j * n_in + i], 0, 0))

    return pl.pallas_call(
        functools.partial(_kmeans_kernel, n_in),
        grid_spec=pltpu.PrefetchScalarGridSpec(
            num_scalar_prefetch=1,
            grid=(nb, steps),
            in_specs=[page_spec(i) for i in range(n_in)],
            out_specs=pl.BlockSpec((1, A_KVW, n_pages // per_blk), lambda b, j, pt: (b, 0, 0)),
        ),
        out_shape=jax.ShapeDtypeStruct((nb, A_KVW, n_pages // per_blk), F32),
        compiler_params=_cparams("parallel", "arbitrary"),
        name="decode_kmeans",
    )(page_table.reshape(-1), *([pages] * n_in))


def _decode_select_kernel(q_ref, km_ref, o_ref):
    nblk = km_ref.shape[2]
    gate = _bdot(q_ref[0], km_ref[0])
    col = lax.broadcasted_iota(jnp.int32, gate.shape, 1)
    lane = lax.broadcasted_iota(jnp.int32, (A_HEADS, LANES), 1)
    out = jnp.zeros((A_HEADS, LANES), jnp.int32)
    for r in range(MOBA_TOPK):
        m = jnp.max(gate, -1, keepdims=True)
        idx = jnp.min(jnp.where(gate == m, col, nblk), -1, keepdims=True)
        out = jnp.where(lane == r, idx, out)
        gate = jnp.where(col == idx, NEG_INF, gate)
    o_ref[0] = out


def _decode_select(q_exp, kmeans):
    nb, _, nblk = kmeans.shape
    return pl.pallas_call(
        _decode_select_kernel,
        grid=(nb,),
        in_specs=[pl.BlockSpec((1, A_HEADS, A_KVW), lambda b: (b, 0, 0)),
                  pl.BlockSpec((1, A_KVW, nblk), lambda b: (b, 0, 0))],
        out_specs=pl.BlockSpec((1, A_HEADS, LANES), lambda b: (b, 0, 0)),
        out_shape=jax.ShapeDtypeStruct((nb, A_HEADS, LANES), jnp.int32),
        compiler_params=_cparams("parallel"),
        name="decode_select",
    )(q_exp, kmeans)


DEC_PAGES = MOBA_TOPK * (MOBA_BLOCK // PAGE_SIZE)


def _decode_attn_kernel(pg_ref, q_ref, kn_ref, vn_ref, *refs):
    k_pages, v_pages, o_ref = refs[:DEC_PAGES], refs[DEC_PAGES:2 * DEC_PAGES], refs[2 * DEC_PAGES]
    h = pl.program_id(1)
    q = q_ref[0, pl.ds(h, 1), :].astype(BF16)
    rb = lambda x: x.astype(BF16).astype(F32)
    s_self = jnp.sum(rb(q) * rb(kn_ref[0]), -1, keepdims=True) * ATT_SCALE
    scores = [_bdot(q, kp[0]) * ATT_SCALE for kp in k_pages]
    m = s_self
    for s in scores:
        m = jnp.maximum(m, jnp.max(s, -1, keepdims=True))
    p_self = jnp.exp(s_self - m)
    l = p_self
    acc = rb(p_self) * rb(vn_ref[0])
    for s, vp in zip(scores, v_pages):
        p = jnp.exp(s - m)
        l = l + jnp.sum(p, -1, keepdims=True)
        acc = acc + _bdot(p, vp[0], _NT)
    o_ref[0, pl.ds(h, 1), :] = acc / l


def _decode_attn(q_exp, k_new, v_new, kp, vp, page_ids, layer, n_pool):
    nb = q_exp.shape[0]

    def page_spec(i):
        return pl.BlockSpec(
            (1, A_KVW, PAGE_SIZE),
            lambda b, h, pg: (layer * n_pool + pg[(b * A_HEADS + h) * DEC_PAGES + i], 0, 0))

    bspec = pl.BlockSpec((1, A_HEADS, A_KVW), lambda b, h, pg: (b, 0, 0))
    nspec = pl.BlockSpec((1, 1, A_KVW), lambda b, h, pg: (b, 0, 0))
    return pl.pallas_call(
        _decode_attn_kernel,
        grid_spec=pltpu.PrefetchScalarGridSpec(
            num_scalar_prefetch=1,
            grid=(nb, A_HEADS),
            in_specs=[bspec, nspec, nspec] + [page_spec(i) for i in range(DEC_PAGES)] * 2,
            out_specs=bspec,
        ),
        out_shape=jax.ShapeDtypeStruct((nb, A_HEADS, A_KVW), F32),
        compiler_params=_cparams("parallel", "arbitrary"),
        name="decode_attn",
    )(page_ids.reshape(-1), q_exp, k_new.reshape(nb, 1, A_KVW), v_new.reshape(nb, 1, A_KVW),
      *([kp] * DEC_PAGES), *([vp] * DEC_PAGES))


def _moba_decode(q, k_new, v_new, k_pages, v_pages, n_pool, page_table, layer):
    nb = q.shape[0]
    onehot = (jnp.arange(A_KV_HEADS)[None, :] == (jnp.arange(A_HEADS) // A_GROUP)[:, None]).astype(F32)
    q_exp = (q.reshape(nb, A_HEADS, 1, A_HEAD) * onehot[None, :, :, None]).reshape(nb, A_HEADS, A_KVW)
    kmeans = _decode_kmeans(k_pages, page_table, layer, n_pool)
    blk = _decode_select(q_exp, kmeans)[:, :, :MOBA_TOPK]
    per_blk = MOBA_BLOCK // PAGE_SIZE
    pages_of = (blk[..., None] * per_blk + jnp.arange(per_blk)).reshape(nb, A_HEADS * DEC_PAGES)
    page_ids = jnp.take_along_axis(page_table, pages_of, axis=1)
    o = _decode_attn(q_exp, k_new, v_new, k_pages, v_pages, page_ids, layer, n_pool)
    o = o.reshape(nb, A_HEADS, A_KV_HEADS, A_HEAD)
    return jnp.einsum("bhkd,hk->bhd", o, onehot).reshape(nb, A_QW)


def _merge_kernel(ya_transposed, x_ref, ys_ref, yr_ref, ya_ref, wg_ref, ps_ref, pr_ref, pa_ref, wo_ref,
                  g_ref, b_ref, o_ref):
    x = x_ref[...]
    xb = x.astype(BF16)
    ya = ya_ref[...].T if ya_transposed else ya_ref[...]
    merged = None
    for i, (y, p_ref) in enumerate(((ys_ref[...], ps_ref), (yr_ref[...], pr_ref), (ya, pa_ref))):
        gate = jax.nn.sigmoid(_bdot(xb, wg_ref[:, i * D_MODEL:(i + 1) * D_MODEL]))
        term = gate * _bdot(y, p_ref[...])
        merged = term if merged is None else merged + term
    o_ref[...] = _ln(ALPHA * x + _bdot(merged, wo_ref[...]), g_ref[...], b_ref[...])


def _merge(x, ys, yr, ya, wg, ps, pr, pa, wo, g, b, tm, ya_transposed):
    m = x.shape[0]
    row = lambda width: pl.BlockSpec((tm, width), lambda i: (i, 0))
    ya_spec = pl.BlockSpec((A_QW, tm), lambda i: (0, i)) if ya_transposed else row(A_QW)
    return pl.pallas_call(
        functools.partial(_merge_kernel, ya_transposed),
        grid=(m // tm,),
        in_specs=[row(D_MODEL), row(S_WIDTH), row(R_WIDTH), ya_spec,
                  _full(wg.shape), _full(ps.shape), _full(pr.shape), _full(pa.shape), _full(wo.shape),
                  _full((1, D_MODEL)), _full((1, D_MODEL))],
        out_specs=row(D_MODEL),
        out_shape=jax.ShapeDtypeStruct((m, D_MODEL), F32),
        compiler_params=_cparams("parallel"),
        name="merge",
    )(x, ys, yr, ya, wg, ps, pr, pa, wo, g, b)


FFN_COLS = D_FF // 2


def _ffn_kernel(tiles_per_seq, x_ref, halo_ref, wup_ref, cw_ref, cb_ref, wdn_ref, g_ref, b_ref, o_ref):
    i = pl.program_id(0)
    x = x_ref[...]
    tm = x.shape[0]
    xe = jnp.concatenate([halo_ref[...], x], axis=0).astype(BF16)
    row = lax.broadcasted_iota(jnp.int32, (tm + SUBLANES, 1), 0)
    keep = jnp.logical_or(row >= SUBLANES, i % tiles_per_seq != 0)
    acc = jnp.zeros((tm, D_MODEL), F32)
    for c in range(D_FF // FFN_COLS):
        halves = []
        for off in (c * FFN_COLS, D_FF + c * FFN_COLS):
            up = jnp.where(keep, _bdot(xe, wup_ref[:, off:off + FFN_COLS]), 0.0)
            cv = cb_ref[:, off:off + FFN_COLS]
            for j in range(CONV_W):
                lo = SUBLANES - (CONV_W - 1) + j
                cv = cv + cw_ref[j:j + 1, off:off + FFN_COLS] * up[lo:lo + tm, :]
            halves.append(cv)
        hmid = jax.nn.gelu(halves[0]) * halves[1]
        acc = acc + _bdot(hmid, wdn_ref[c * FFN_COLS:(c + 1) * FFN_COLS, :])
    o_ref[...] = _ln(ALPHA * x + acc, g_ref[...], b_ref[...])


def _ffn(x, wup, cw, cb, wdn, g, b, seq, tm):
    m = x.shape[0]
    per8 = tm // SUBLANES
    return pl.pallas_call(
        functools.partial(_ffn_kernel, seq // tm),
        grid=(m // tm,),
        in_specs=[pl.BlockSpec((tm, D_MODEL), lambda i: (i, 0)),
                  pl.BlockSpec((SUBLANES, D_MODEL), lambda i: (jnp.maximum(i * per8 - 1, 0), 0)),
                  _full(wup.shape), _full(cw.shape), _full(cb.shape), _full(wdn.shape),
                  _full((1, D_MODEL)), _full((1, D_MODEL))],
        out_specs=pl.BlockSpec((tm, D_MODEL), lambda i: (i, 0)),
        out_shape=jax.ShapeDtypeStruct((m, D_MODEL), F32),
        compiler_params=_cparams("parallel"),
        name="ffn",
    )(x, x, wup, cw, cb, wdn, g, b)


def _ffn_step_kernel(x_ref, c0a_ref, c0b_ref, wup_ref, cw_ref, cb_ref, wdn_ref, g_ref, b_ref, o_ref, up_ref):
    x = x_ref[...]
    up = _bdot(x, wup_ref[...])
    up_ref[...] = up
    cv = cb_ref[...] + cw_ref[0:1, :] * c0a_ref[...] + cw_ref[1:2, :] * c0b_ref[...] + cw_ref[2:3, :] * up
    hmid = jax.nn.gelu(cv[:, :D_FF]) * cv[:, D_FF:]
    o_ref[...] = _ln(ALPHA * x + _bdot(hmid, wdn_ref[...]), g_ref[...], b_ref[...])


def _ffn_step(x, c0a, c0b, wup, cw, cb, wdn, g, b):
    n = x.shape[0]
    return pl.pallas_call(
        _ffn_step_kernel,
        out_shape=[jax.ShapeDtypeStruct((n, D_MODEL), F32), jax.ShapeDtypeStruct((n, 2 * D_FF), F32)],
        compiler_params=_cparams(),
        name="ffn_step",
    )(x, c0a, c0b, wup, cw, cb, wdn, g, b)


def _up_rows_kernel(x_ref, w_ref, o_ref):
    o_ref[...] = _bdot(x_ref[...], w_ref[...])


def _up_rows(x, w):
    return pl.pallas_call(
        _up_rows_kernel,
        out_shape=jax.ShapeDtypeStruct((x.shape[0], w.shape[1]), F32),
        compiler_params=_cparams(),
        name="ffn_up_tail",
    )(x, w)


def _layer_params(l, P):
    bf = lambda x: x.astype(BF16)
    row = lambda x: x.reshape(1, -1)
    w_in = P["w_in"][l]
    return dict(
        w_gate=bf(w_in[:, :O_SSM]), w_rest=bf(w_in[:, O_SSM:]),
        s5=_s5_params(P["ssm_a_re"][l], P["ssm_a_im"][l], P["ssm_log_dt"][l], P["ssm_b_re"][l],
                      P["ssm_b_im"][l], P["ssm_c_re"][l], P["ssm_c_im"][l]),
        ssm_d=row(P["ssm_d"][l]), w_glu=bf(P["ssm_w_glu"][l]), b_glu=row(P["ssm_b_glu"][l]),
        rwkv=_rwkv_weights(P["rwkv_mu"][l], P["rwkv_w0"][l], P["rwkv_w2"][l], P["rwkv_a0"][l],
                           P["rwkv_a2"][l], P["rwkv_g2"][l], P["rwkv_k_k"][l], P["rwkv_k_a"][l],
                           P["rwkv_r_k"][l]),
        lnx_g=row(P["rwkv_lnx_g"][l]), lnx_b=row(P["rwkv_lnx_b"][l]),
        proj_ssm=bf(P["proj_ssm"][l]), proj_rwkv=bf(P["proj_rwkv"][l]), proj_attn=bf(P["proj_attn"][l]),
        w_o=bf(P["w_o"][l]), ln1_g=row(P["ln1_g"][l]), ln1_b=row(P["ln1_b"][l]),
        w_up=bf(P["ffn_w_up"][l]), conv_w=P["ffn_conv_w"][l], conv_b=row(P["ffn_conv_b"][l]),
        w_down=bf(P["ffn_w_down"][l]), ln2_g=row(P["ln2_g"][l]), ln2_b=row(P["ln2_b"][l]),
    )


def _pick_tile(n, pref):
    t = min(pref, n)
    while n % t:
        t //= 2
    return t


def _prompt_trunk(x_prompt, P, layers):
    nb, seq, _ = x_prompt.shape
    m = nb * seq
    x = x_prompt.reshape(m, D_MODEL)
    tm = _pick_tile(seq, 512)
    rope = _rope_tables(jnp.arange(seq, dtype=jnp.int32))
    ln_g, ln_b = P["ln_in_g"].reshape(1, -1), P["ln_in_b"].reshape(1, -1)
    outs = []
    for l, lp in enumerate(layers):
        res = _in_proj(x, ln_g, ln_b, lp["w_rest"], rope, tm, seq // tm, pre_ln=(l == 0), attn_layouts=True)
        u, c, k, v, qt, kh, vt = res[:7]
        if l == 0:
            x = res[7]
        y_s, s_fin = _s5_prompt(u, lp["s5"], lp["ssm_d"], lp["w_glu"], lp["b_glu"],
                                jnp.zeros((nb, 2, S_LANES), F32), nb, seq, _pick_tile(seq, 256))
        r, lw, k2, vv, a, b, gate, bonus = _rwkv_pre(c, jnp.zeros((nb, R_IN), F32), lp["rwkv"], nb, seq,
                                                     _pick_tile(seq, 256), decode=False)
        y1, y2, g, f = _rwkv_chunks(r, lw, k2, vv, a, b, RWKV_CHUNK)
        y_r, h_fin = _rwkv_scan(y1, y2, g, f, bonus, gate, lp["lnx_g"], lp["lnx_b"],
                                jnp.zeros((nb, R_HEADS, R_HEAD, R_HEAD), F32), nb, seq, RWKV_CHUNK)
        y_at = _moba_prompt(qt, kh, vt, nb, seq)
        x = _merge(x, y_s, y_r, y_at, lp["w_gate"], lp["proj_ssm"], lp["proj_rwkv"], lp["proj_attn"],
                   lp["w_o"], lp["ln1_g"], lp["ln1_b"], _pick_tile(seq, 256), ya_transposed=True)
        tail = x.reshape(nb, seq, D_MODEL)[:, seq - (CONV_W - 1):].reshape(nb * (CONV_W - 1), D_MODEL)
        conv = _up_rows(tail, lp["w_up"]).reshape(nb, CONV_W - 1, 2 * D_FF)
        x = _ffn(x, lp["w_up"], lp["conv_w"], lp["conv_b"], lp["w_down"], lp["ln2_g"], lp["ln2_b"],
                 seq, _pick_tile(seq, 256))
        outs.append(dict(
            k=k.reshape(nb, seq, A_KV_HEADS, A_HEAD), v=v.reshape(nb, seq, A_KV_HEADS, A_HEAD),
            s_re=s_fin[:, 0].reshape(nb, S_GROUPS, S_STATE), s_im=s_fin[:, 1].reshape(nb, S_GROUPS, S_STATE),
            rwkv=jnp.swapaxes(h_fin, -1, -2), shift=c.reshape(nb, seq, R_IN)[:, -1], conv=conv))
    return x.reshape(nb, seq, D_MODEL), outs


def _sample_trunk(x_sample, P, layers, cache_k, cache_v, page_table, st_re, st_im, st_rwkv, st_shift, st_conv):
    nb = x_sample.shape[0]
    x = x_sample.reshape(nb, D_MODEL)
    past_len = page_table.shape[1] * PAGE_SIZE
    rope = _rope_tables(jnp.full((nb,), past_len, jnp.int32))
    ln_g, ln_b = P["ln_in_g"].reshape(1, -1), P["ln_in_b"].reshape(1, -1)
    k_pages, v_pages = _pages_t(cache_k), _pages_t(cache_v)
    outs = []
    for l, lp in enumerate(layers):
        res = _in_proj(x, ln_g, ln_b, lp["w_rest"], rope, nb, 1, pre_ln=(l == 0), attn_layouts=False)
        u, c, k, v, q = res[:5]
        if l == 0:
            x = res[5]
        s0 = jnp.concatenate([st_re[l].reshape(nb, S_LANES), st_im[l].reshape(nb, S_LANES)], axis=1)
        y_s, s_new = _s5_step(u, lp["s5"], lp["ssm_d"], lp["w_glu"], lp["b_glu"], s0)
        r, lw, k2, vv, a, b, gate, bonus = _rwkv_pre(c, st_shift[l], lp["rwkv"], nb, 1, nb, decode=True)
        per_head = lambda t: t.reshape(nb * R_HEADS, R_HEAD)
        y_rows, s_rwkv = _rwkv_step([per_head(t) for t in (r, lw, k2, vv, a, b)],
                                    st_rwkv[l].reshape(nb * R_HEADS, R_HEAD, R_HEAD))
        y_r = _rwkv_post_call(y_rows.reshape(nb, R_WIDTH), bonus, gate, lp["lnx_g"], lp["lnx_b"])
        y_a = _moba_decode(q, k, v, k_pages, v_pages, cache_k.shape[1], page_table, l)
        x = _merge(x, y_s, y_r, y_a, lp["w_gate"], lp["proj_ssm"], lp["proj_rwkv"], lp["proj_attn"],
                   lp["w_o"], lp["ln1_g"], lp["ln1_b"], nb, ya_transposed=False)
        x, up = _ffn_step(x, st_conv[l][:, 0], st_conv[l][:, 1], lp["w_up"], lp["conv_w"], lp["conv_b"],
                          lp["w_down"], lp["ln2_g"], lp["ln2_b"])
        outs.append(dict(
            k=k.reshape(nb, 1, A_KV_HEADS, A_HEAD), v=v.reshape(nb, 1, A_KV_HEADS, A_HEAD),
            s_re=s_new[:, :S_LANES].reshape(nb, S_GROUPS, S_STATE),
            s_im=s_new[:, S_LANES:].reshape(nb, S_GROUPS, S_STATE),
            rwkv=s_rwkv.reshape(nb, R_HEADS, R_HEAD, R_HEAD), shift=c,
            conv=jnp.stack([st_conv[l][:, 1], up], axis=1)))
    return x.reshape(nb, 1, D_MODEL), outs


def kernel(x_prompt, x_sample, cache_k, cache_v, page_table, state_ssm_re, state_ssm_im, state_rwkv, state_rwkv_shift, state_conv, ln_in_g, ln_in_b, w_in, ssm_a_re, ssm_a_im, ssm_log_dt, ssm_b_re, ssm_b_im, ssm_c_re, ssm_c_im, ssm_d, ssm_w_glu, ssm_b_glu, rwkv_mu, rwkv_w0, rwkv_w2, rwkv_a0, rwkv_a2, rwkv_g2, rwkv_k_k, rwkv_k_a, rwkv_r_k, rwkv_lnx_g, rwkv_lnx_b, proj_ssm, proj_rwkv, proj_attn, w_o, ln1_g, ln1_b, ffn_w_up, ffn_conv_w, ffn_conv_b, ffn_w_down, ln2_g, ln2_b):
    P = dict(ln_in_g=ln_in_g, ln_in_b=ln_in_b, w_in=w_in,
             ssm_a_re=ssm_a_re, ssm_a_im=ssm_a_im, ssm_log_dt=ssm_log_dt,
             ssm_b_re=ssm_b_re, ssm_b_im=ssm_b_im, ssm_c_re=ssm_c_re, ssm_c_im=ssm_c_im,
             ssm_d=ssm_d, ssm_w_glu=ssm_w_glu, ssm_b_glu=ssm_b_glu,
             rwkv_mu=rwkv_mu, rwkv_w0=rwkv_w0, rwkv_w2=rwkv_w2, rwkv_a0=rwkv_a0, rwkv_a2=rwkv_a2,
             rwkv_g2=rwkv_g2, rwkv_k_k=rwkv_k_k, rwkv_k_a=rwkv_k_a, rwkv_r_k=rwkv_r_k,
             rwkv_lnx_g=rwkv_lnx_g, rwkv_lnx_b=rwkv_lnx_b,
             proj_ssm=proj_ssm, proj_rwkv=proj_rwkv, proj_attn=proj_attn, w_o=w_o,
             ln1_g=ln1_g, ln1_b=ln1_b, ffn_w_up=ffn_w_up, ffn_conv_w=ffn_conv_w,
             ffn_conv_b=ffn_conv_b, ffn_w_down=ffn_w_down, ln2_g=ln2_g, ln2_b=ln2_b)
    layers = [_layer_params(l, P) for l in range(w_in.shape[0])]
    y_p, op = _prompt_trunk(x_prompt, P, layers)
    y_s, os_ = _sample_trunk(x_sample, P, layers, cache_k, cache_v, page_table, state_ssm_re, state_ssm_im,
                             state_rwkv, state_rwkv_shift, state_conv)
    st = lambda outs, key: jnp.stack([o[key] for o in outs])
    return (y_p, y_s, st(op, "k"), st(op, "v"), st(os_, "k"), st(os_, "v"),
            st(op, "s_re"), st(op, "s_im"), st(os_, "s_re"), st(os_, "s_im"),
            st(op, "rwkv"), st(os_, "rwkv"), st(op, "shift"), st(os_, "shift"),
            st(op, "conv"), st(os_, "conv"))
```

```python
import functools
import math

import jax
import jax.numpy as jnp
from jax import lax
from jax.experimental import pallas as pl
from jax.experimental.pallas import tpu as pltpu

F32 = jnp.float32
BF16 = jnp.bfloat16

D_MODEL = 1024
PAGE_SIZE = 128
S_GROUP = 16
S_GROUPS = 16
S_STATE = 64
S_WIDTH = S_GROUPS * S_GROUP
S_LANES = S_GROUPS * S_STATE
R_HEAD = 64
R_HEADS = 4
R_WIDTH = R_HEADS * R_HEAD
W_LORA = 32
A_LORA = 32
G_LORA = 64
N_LORA = W_LORA + A_LORA + G_LORA
R_IN = 3 * R_WIDTH + N_LORA
GN_EPS = 64e-5
A_HEAD = 64
A_HEADS = 8
A_KV_HEADS = 4
A_GROUP = A_HEADS // A_KV_HEADS
A_QW = A_HEADS * A_HEAD
A_KVW = A_KV_HEADS * A_HEAD
ROT_DIM = A_HEAD // 4
ROPE_THETA = 500000.0
MOBA_BLOCK = 256
MOBA_TOPK = 3
N_BRANCH = 3
O_SSM = N_BRANCH * D_MODEL
N_REST = S_WIDTH + R_IN + A_QW + 2 * A_KVW
D_FF = 2816
CONV_W = 3
DEPTH = 2
ALPHA = (2 * DEPTH) ** 0.25
LN_EPS = 1e-5
ATT_SCALE = A_HEAD ** -0.5

VMEM_LIMIT_BYTES = 56 * 1024 * 1024
LANES = 128
SUBLANES = 8

MOBA_ROWS = 16
RWKV_CHUNK = 64
NEG_INF = float("-inf")

_NN = (((1,), (0,)), ((), ()))
_NT = (((1,), (1,)), ((), ()))
_TN = (((0,), (0,)), ((), ()))


def _cparams(*sem):
    return pltpu.CompilerParams(dimension_semantics=sem or None, vmem_limit_bytes=VMEM_LIMIT_BYTES)


def _bdot(a, b, dims=_NN):
    return lax.dot_general(a.astype(BF16), b.astype(BF16), dims, preferred_element_type=F32)


def _dot3(a, b, dims=_NN):
    ah = a.astype(BF16)
    al = (a - ah.astype(F32)).astype(BF16)
    bh = b.astype(BF16)
    bl = (b - bh.astype(F32)).astype(BF16)
    f = lambda x, y: lax.dot_general(x, y, dims, preferred_element_type=F32)
    return f(ah, bh) + f(ah, bl) + f(al, bh)


def _ln(x, g, b):
    mu = jnp.mean(x, -1, keepdims=True)
    xc = x - mu
    var = jnp.mean(xc * xc, -1, keepdims=True)
    return xc * lax.rsqrt(var + LN_EPS) * g + b


def _head_sum(x, head):
    lane_head = lax.broadcasted_iota(jnp.int32, (1, x.shape[1]), 1) // head
    out = jnp.zeros_like(x)
    for h in range(x.shape[1] // head):
        m = lane_head == h
        s = jnp.sum(jnp.where(m, x, 0.0), -1, keepdims=True)
        out = jnp.where(m, s, out)
    return out


def _full(shape):
    n = len(shape)
    return pl.BlockSpec(shape, lambda *_: (0,) * n)


def _in_proj_kernel(pre_ln, attn_layouts, x_ref, g_ref, b_ref, w_ref, cos_ref, sa_ref, sb_ref,
                    u_ref, c_ref, k_ref, v_ref, *rest):
    x = x_ref[...]
    if pre_ln:
        x = _ln(x, g_ref[...], b_ref[...])
        rest[-1][...] = x
    h = _bdot(x, w_ref[...])
    u_ref[...] = h[:, :S_WIDTH]
    c_ref[...] = h[:, S_WIDTH:S_WIDTH + R_IN]
    cos, sa, sb = cos_ref[...], sa_ref[...], sb_ref[...]
    base = S_WIDTH + R_IN
    nq = A_QW // LANES
    per = LANES // A_HEAD
    for j in range((A_QW + A_KVW) // LANES):
        ch = h[:, base + LANES * j: base + LANES * (j + 1)]
        rot = (ch * cos + pltpu.roll(ch, LANES - ROT_DIM // 2, 1) * sa
               + pltpu.roll(ch, ROT_DIM // 2, 1) * sb)
        if j < nq:
            if attn_layouts:
                rest[0][LANES * j:LANES * (j + 1), :] = rot.T
            else:
                rest[0][:, LANES * j:LANES * (j + 1)] = rot
        else:
            k_ref[:, LANES * (j - nq):LANES * (j - nq + 1)] = rot
            if attn_layouts:
                for i in range(per):
                    rest[1][per * (j - nq) + i] = rot[:, A_HEAD * i:A_HEAD * (i + 1)]
    v = h[:, base + A_QW + A_KVW:]
    v_ref[...] = v
    if attn_layouts:
        for j in range(A_KVW // LANES):
            rest[2][LANES * j:LANES * (j + 1), :] = v[:, LANES * j:LANES * (j + 1)].T


def _in_proj(x, ln_g, ln_b, w, rope, tm, pos_tiles, pre_ln, attn_layouts):
    m = x.shape[0]
    row = lambda width: pl.BlockSpec((tm, width), lambda i: (i, 0))
    col = lambda width: pl.BlockSpec((width, tm), lambda i: (0, i))
    tab = pl.BlockSpec((tm, LANES), lambda i: (i % pos_tiles, 0))
    f32 = lambda *shape: jax.ShapeDtypeStruct(shape, F32)
    specs = [row(S_WIDTH), row(R_IN), row(A_KVW), row(A_KVW)]
    shapes = [f32(m, S_WIDTH), f32(m, R_IN), f32(m, A_KVW), f32(m, A_KVW)]
    if attn_layouts:
        specs += [col(A_QW), pl.BlockSpec((A_KV_HEADS, tm, A_HEAD), lambda i: (0, i, 0)), col(A_KVW)]
        shapes += [f32(A_QW, m), f32(A_KV_HEADS, m, A_HEAD), f32(A_KVW, m)]
    else:
        specs += [row(A_QW)]
        shapes += [f32(m, A_QW)]
    if pre_ln:
        specs += [row(D_MODEL)]
        shapes += [f32(m, D_MODEL)]
    return pl.pallas_call(
        functools.partial(_in_proj_kernel, pre_ln, attn_layouts),
        grid=(m // tm,),
        in_specs=[row(D_MODEL), _full((1, D_MODEL)), _full((1, D_MODEL)), _full((D_MODEL, N_REST)),
                  tab, tab, tab],
        out_specs=specs,
        out_shape=shapes,
        compiler_params=_cparams("parallel"),
        name="in_proj",
    )(x, ln_g, ln_b, w, *rope)


def _rope_tables(pos):
    half = ROT_DIM // 2
    inv = ROPE_THETA ** (-jnp.arange(half, dtype=F32) / half)
    ang = pos.astype(F32)[:, None] * inv[None, :]
    cos, sin = jnp.cos(ang), jnp.sin(ang)
    n = pos.shape[0]
    pad = jnp.zeros((n, A_HEAD - ROT_DIM), F32)
    zero = jnp.zeros((n, half), F32)
    cos_h = jnp.concatenate([cos, cos, pad + 1.0], -1)
    sa_h = jnp.concatenate([-sin, zero, pad], -1)
    sb_h = jnp.concatenate([zero, sin, pad], -1)
    rep = LANES // A_HEAD
    return tuple(jnp.tile(t, (1, rep)) for t in (cos_h, sa_h, sb_h))


def _s5_params(a_re, a_im, log_dt, b_re, b_im, c_re, c_im):
    dt = jnp.exp(log_dt)[:, None]
    lam_re, lam_im = a_re * dt, a_im * dt

    def power(k):
        mag = jnp.exp(lam_re * k)
        return (mag * jnp.cos(lam_im * k)).reshape(-1), (mag * jnp.sin(lam_im * k)).reshape(-1)

    abar_re, abar_im = power(1.0)
    den = (a_re * a_re + a_im * a_im).reshape(-1)
    ar, ai = a_re.reshape(-1), a_im.reshape(-1)
    em_re = abar_re - 1.0
    coef = jnp.stack([(em_re * ar + abar_im * ai) / den, (abar_im * ar - em_re * ai) / den])
    pw = jnp.stack([jnp.stack(x) for x in zip(*[power(float(k)) for k in range(1, SUBLANES + 1)])])
    dbl = jnp.stack([jnp.stack(x) for x in zip(*[power(float(k)) for k in (1, 2, 4)])])
    eye = jnp.eye(S_GROUPS, dtype=F32)
    wb = jnp.concatenate([jnp.einsum("gpc,gh->gchp", b, eye).reshape(S_WIDTH, S_LANES)
                          for b in (b_re, b_im)], axis=1)
    wc = jnp.concatenate([jnp.einsum("gcp,gh->gphc", c, eye).reshape(S_LANES, S_WIDTH)
                          for c in (c_re, -c_im)], axis=0)
    return dict(coef=coef, pw=pw, dbl=dbl, abar=jnp.stack([abar_re, abar_im]),
                wb=wb.astype(BF16), wc=wc.astype(BF16))


def _s5_glu(s_re, s_im, u, wc_ref, d_ref, wg_ref, bg_ref):
    s = jnp.concatenate([s_re, s_im], axis=1)
    y = _bdot(s, wc_ref[...]) + d_ref[...] * u
    z = jax.nn.gelu(y)
    return z * jax.nn.sigmoid(_bdot(z, wg_ref[...]) + bg_ref[...])


def _s5_kernel(u_ref, wb_ref, wc_ref, coef_ref, pw_ref, dbl_ref, d_ref, wg_ref, bg_ref, s0_ref,
               y_ref, sfin_ref, s_scr, carry_scr):
    j = pl.program_id(1)
    t = u_ref.shape[0]

    @pl.when(j == 0)
    def _():
        carry_scr[...] = s0_ref[0]

    u = u_ref[...]
    bu = _bdot(u, wb_ref[...])
    b_re, b_im = bu[:, :S_LANES], bu[:, S_LANES:]
    c_re, c_im = coef_ref[0:1, :], coef_ref[1:2, :]
    s_scr[0] = c_re * b_re - c_im * b_im
    s_scr[1] = c_re * b_im + c_im * b_re
    row = lax.broadcasted_iota(jnp.int32, (SUBLANES, S_LANES), 0)

    def body(g, carry):
        cr, ci = carry
        off = pl.multiple_of(g * SUBLANES, SUBLANES)
        sr = s_scr[0, pl.ds(off, SUBLANES), :]
        si = s_scr[1, pl.ds(off, SUBLANES), :]
        for n, d in enumerate((1, 2, 4)):
            ar, ai = dbl_ref[0, n:n + 1, :], dbl_ref[1, n:n + 1, :]
            pr = jnp.where(row >= d, pltpu.roll(sr, d, 0), 0.0)
            pi = jnp.where(row >= d, pltpu.roll(si, d, 0), 0.0)
            sr, si = sr + ar * pr - ai * pi, si + ar * pi + ai * pr
        p_re, p_im = pw_ref[0], pw_ref[1]
        sr, si = sr + p_re * cr - p_im * ci, si + p_re * ci + p_im * cr
        s_scr[0, pl.ds(off, SUBLANES), :] = sr
        s_scr[1, pl.ds(off, SUBLANES), :] = si
        return sr[SUBLANES - 1:SUBLANES, :], si[SUBLANES - 1:SUBLANES, :]

    cr, ci = lax.fori_loop(0, t // SUBLANES, body, (carry_scr[0:1, :], carry_scr[1:2, :]), unroll=2)
    carry_scr[0:1, :] = cr
    carry_scr[1:2, :] = ci
    y_ref[...] = _s5_glu(s_scr[0], s_scr[1], u, wc_ref, d_ref, wg_ref, bg_ref)

    @pl.when(j == pl.num_programs(1) - 1)
    def _():
        sfin_ref[0] = carry_scr[...]


def _s5_prompt(u, sp, d, w_glu, b_glu, s0, nb, seq, t):
    nt = seq // t
    return pl.pallas_call(
        _s5_kernel,
        grid=(nb, nt),
        in_specs=[pl.BlockSpec((t, S_WIDTH), lambda b, j: (b * nt + j, 0)),
                  _full((S_WIDTH, 2 * S_LANES)), _full((2 * S_LANES, S_WIDTH)), _full((2, S_LANES)),
                  _full((2, SUBLANES, S_LANES)), _full((2, 3, S_LANES)), _full((1, S_WIDTH)),
                  _full((S_WIDTH, S_WIDTH)), _full((1, S_WIDTH)),
                  pl.BlockSpec((1, 2, S_LANES), lambda b, j: (b, 0, 0))],
        out_specs=[pl.BlockSpec((t, S_WIDTH), lambda b, j: (b * nt + j, 0)),
                   pl.BlockSpec((1, 2, S_LANES), lambda b, j: (b, 0, 0))],
        out_shape=[jax.ShapeDtypeStruct((nb * seq, S_WIDTH), F32),
                   jax.ShapeDtypeStruct((nb, 2, S_LANES), F32)],
        scratch_shapes=[pltpu.VMEM((2, t, S_LANES), F32), pltpu.VMEM((2, S_LANES), F32)],
        compiler_params=_cparams("parallel", "arbitrary"),
        name="s5_scan",
    )(u, sp["wb"], sp["wc"], sp["coef"], sp["pw"], sp["dbl"], d, w_glu, b_glu, s0)


def _s5_step_kernel(u_ref, wb_ref, wc_ref, coef_ref, abar_ref, d_ref, wg_ref, bg_ref, s0_ref,
                    y_ref, s_ref):
    u = u_ref[...]
    bu = _bdot(u, wb_ref[...])
    b_re, b_im = bu[:, :S_LANES], bu[:, S_LANES:]
    c_re, c_im = coef_ref[0:1, :], coef_ref[1:2, :]
    a_re, a_im = abar_ref[0:1, :], abar_ref[1:2, :]
    s0_re, s0_im = s0_ref[:, :S_LANES], s0_ref[:, S_LANES:]
    s_re = c_re * b_re - c_im * b_im + (a_re * s0_re - a_im * s0_im)
    s_im = c_re * b_im + c_im * b_re + (a_re * s0_im + a_im * s0_re)
    s_ref[:, :S_LANES] = s_re
    s_ref[:, S_LANES:] = s_im
    y_ref[...] = _s5_glu(s_re, s_im, u, wc_ref, d_ref, wg_ref, bg_ref)


def _s5_step(u, sp, d, w_glu, b_glu, s0):
    n = u.shape[0]
    return pl.pallas_call(
        _s5_step_kernel,
        out_shape=[jax.ShapeDtypeStruct((n, S_WIDTH), F32), jax.ShapeDtypeStruct((n, 2 * S_LANES), F32)],
        compiler_params=_cparams(),
        name="s5_step",
    )(u, sp["wb"], sp["wc"], sp["coef"], sp["abar"], d, w_glu, b_glu, s0)


def _rwkv_pre_kernel(decode, c_ref, prev_ref, sh0_ref, mu_ref, w0_ref, w2_ref, a0_ref, a2_ref, g2_ref,
                     kk_ref, ka_ref, rk_ref,
                     r_o, lw_o, k_o, v_o, a_o, b_o, g_o, bonus_o):
    c = c_ref[...]
    if decode:
        prev = sh0_ref[...]
    else:
        j = pl.program_id(1)
        last = jnp.where(j == 0, sh0_ref[0], prev_ref[SUBLANES - 1:SUBLANES, :])
        row = lax.broadcasted_iota(jnp.int32, c.shape, 0)
        prev = jnp.where(row == 0, last, pltpu.roll(c, 1, 0))
    cf = c + (prev - c) * mu_ref[...]
    r = cf[:, :R_WIDTH]
    k = cf[:, R_WIDTH:2 * R_WIDTH]
    v = cf[:, 2 * R_WIDTH:3 * R_WIDTH]
    lora = cf[:, 3 * R_WIDTH:]
    w_log = -jax.nn.softplus(-(w0_ref[...] + _bdot(jnp.tanh(lora), w2_ref[...]))) - 0.5
    a = jax.nn.sigmoid(a0_ref[...] + _bdot(lora, a2_ref[...]))
    g = _bdot(jax.nn.sigmoid(lora), g2_ref[...])
    kk = k * kk_ref[...]
    kk = kk * lax.rsqrt(jnp.maximum(_head_sum(kk * kk, R_HEAD), 1e-24))
    k = k * (1.0 + (a - 1.0) * ka_ref[...])
    r_o[...] = r
    lw_o[...] = -jnp.exp(w_log)
    k_o[...] = k
    v_o[...] = v
    a_o[...] = -kk
    b_o[...] = kk * a
    g_o[...] = g
    bonus_o[...] = _head_sum(r * k * rk_ref[...], R_HEAD) * v


def _rwkv_weights(mu, w0, w2, a0, a2, g2, k_k, k_a, r_k):
    z = lambda n: jnp.zeros((n, R_WIDTH), F32)
    w2p = jnp.concatenate([w2, z(A_LORA + G_LORA)], 0).astype(BF16)
    a2p = jnp.concatenate([z(W_LORA), a2, z(G_LORA)], 0).astype(BF16)
    g2p = jnp.concatenate([z(W_LORA + A_LORA), g2], 0).astype(BF16)
    row = lambda x: x.reshape(1, -1)
    return [row(mu), row(w0), w2p, row(a0), a2p, g2p, row(k_k), row(k_a), row(r_k)]


def _rwkv_pre(c, sh0, wts, nb, seq, t, decode):
    m = c.shape[0]
    outs = [jax.ShapeDtypeStruct((m, R_WIDTH), F32)] * 8
    wspecs = [_full(w.shape) for w in wts]
    if decode:
        return pl.pallas_call(
            functools.partial(_rwkv_pre_kernel, True),
            grid=(1,),
            in_specs=[_full((m, R_IN)), _full((SUBLANES, R_IN)), _full((m, R_IN))] + wspecs,
            out_specs=[_full((m, R_WIDTH))] * 8,
            out_shape=outs,
            compiler_params=_cparams("arbitrary"),
            name="rwkv_pre_step",
        )(c, c[:SUBLANES], sh0, *wts)
    nt = seq // t
    per8 = t // SUBLANES
    return pl.pallas_call(
        functools.partial(_rwkv_pre_kernel, False),
        grid=(nb, nt),
        in_specs=[pl.BlockSpec((t, R_IN), lambda b, j: (b * nt + j, 0)),
                  pl.BlockSpec((SUBLANES, R_IN), lambda b, j: (jnp.maximum((b * nt + j) * per8 - 1, 0), 0)),
                  pl.BlockSpec((1, 1, R_IN), lambda b, j: (b, 0, 0))] + wspecs,
        out_specs=[pl.BlockSpec((t, R_WIDTH), lambda b, j: (b * nt + j, 0))] * 8,
        out_shape=outs,
        compiler_params=_cparams("parallel", "parallel"),
        name="rwkv_pre",
    )(c, c, sh0.reshape(nb, 1, R_IN), *wts)


def _rwkv_chunk_kernel(r_ref, lw_ref, k_ref, v_ref, a_ref, b_ref, y1_ref, y2_ref, g_ref, f_ref):
    t = RWKV_CHUNK
    n_chunks = r_ref.shape[0] // t
    row = lax.broadcasted_iota(jnp.int32, (t, R_WIDTH), 0)
    ri = lax.broadcasted_iota(jnp.int32, (t, t), 0)
    ci = lax.broadcasted_iota(jnp.int32, (t, t), 1)
    hi = lax.broadcasted_iota(jnp.int32, (R_HEAD, R_HEAD), 0)
    hj = lax.broadcasted_iota(jnp.int32, (R_HEAD, R_HEAD), 1)
    per_head = lambda x: [x[:, h * R_HEAD:(h + 1) * R_HEAD] for h in range(R_HEADS)]
    ah, rh, bh, kh, vh, be_t, ke_t, ee = ([] for _ in range(8))
    for c in range(n_chunks):
        rows = slice(c * t, (c + 1) * t)
        lw = lw_ref[rows, :]
        cw = lw
        d = 1
        while d < t:
            cw = cw + jnp.where(row >= d, pltpu.roll(cw, d, 0), 0.0)
            d *= 2
        cw_end = cw[t - 1:t, :]
        e_neg = jnp.exp(-cw)
        e_rem = jnp.exp(cw_end - cw)
        ah += per_head(a_ref[rows, :] * jnp.exp(cw - lw))
        rh += per_head(r_ref[rows, :] * jnp.exp(cw))
        bh += per_head(b_ref[rows, :] * e_neg)
        kh += per_head(k_ref[rows, :] * e_neg)
        vh += per_head(v_ref[rows, :])
        be_t += [x.T for x in per_head(b_ref[rows, :] * e_rem)]
        ke_t += [x.T for x in per_head(k_ref[rows, :] * e_rem)]
        ee += per_head(jnp.exp(cw_end))
    units = range(n_chunks * R_HEADS)
    a_ab = [jnp.where(ri > ci, _dot3(ah[u], bh[u], _NT), 0.0) for u in units]
    a_ak = [jnp.where(ri > ci, _dot3(ah[u], kh[u], _NT), 0.0) for u in units]
    a_rb = [jnp.where(ri >= ci, _dot3(rh[u], bh[u], _NT), 0.0) for u in units]
    a_rk = [jnp.where(ri >= ci, _dot3(rh[u], kh[u], _NT), 0.0) for u in units]
    akv = [_dot3(a_ak[u], vh[u]) for u in units]
    rkv = [_dot3(a_rk[u], vh[u]) for u in units]
    inv = [jnp.where(ri == ci, 1.0, 0.0) + a_ab[u] for u in units]
    apow = a_ab
    n = 2
    while n < t:
        apow = [_dot3(apow[u], apow[u]) for u in units]
        inv = [inv[u] + _dot3(apow[u], inv[u]) for u in units]
        n *= 2
    w1 = [_dot3(inv[u], ah[u]) for u in units]
    w2 = [_dot3(inv[u], akv[u]) for u in units]
    for u in units:
        c, h = divmod(u, R_HEADS)
        y1_ref[c, h] = rh[u] + _dot3(a_rb[u], w1[u])
        y2_ref[c, h] = _dot3(a_rb[u], w2[u]) + rkv[u]
        g_ref[c, h] = jnp.where(hi == hj, ee[u], 0.0) + _dot3(be_t[u], w1[u])
        f_ref[c, h] = _dot3(be_t[u], w2[u]) + _dot3(ke_t[u], vh[u])


RWKV_CHUNKS_PER_STEP = 2


def _rwkv_chunks(r, lw, k, v, a, b, t):
    m = r.shape[0]
    per = RWKV_CHUNKS_PER_STEP if (m // t) % RWKV_CHUNKS_PER_STEP == 0 else 1
    nc = m // t
    spec = pl.BlockSpec((per * t, R_WIDTH), lambda i: (i, 0))
    yspec = pl.BlockSpec((per, R_HEADS, t, R_HEAD), lambda i: (i, 0, 0, 0))
    gspec = pl.BlockSpec((per, R_HEADS, R_HEAD, R_HEAD), lambda i: (i, 0, 0, 0))
    return pl.pallas_call(
        _rwkv_chunk_kernel,
        grid=(nc // per,),
        in_specs=[spec] * 6,
        out_specs=[yspec, yspec, gspec, gspec],
        out_shape=[jax.ShapeDtypeStruct((nc, R_HEADS, t, R_HEAD), F32)] * 2
        + [jax.ShapeDtypeStruct((nc, R_HEADS, R_HEAD, R_HEAD), F32)] * 2,
        compiler_params=_cparams("parallel"),
        name="rwkv_chunk",
    )(r, lw, k, v, a, b)


def _rwkv_post(y, bonus, g, lng, lnb):
    m = _head_sum(y, R_HEAD) * (1.0 / R_HEAD)
    yc = y - m
    var = _head_sum(yc * yc, R_HEAD) * (1.0 / R_HEAD)
    return (yc * lax.rsqrt(var + GN_EPS) * lng + lnb + bonus) * g


def _rwkv_scan_kernel(y1_ref, y2_ref, g_ref, f_ref, bonus_ref, gate_ref, lng_ref, lnb_ref, h0_ref,
                      y_ref, hfin_ref, h_scr, y_scr):
    j = pl.program_id(1)

    @pl.when(j == 0)
    def _():
        h_scr[...] = h0_ref[0]

    t = RWKV_CHUNK
    heads = range(R_HEADS)
    states = [h_scr[h] for h in heads]
    for c in range(y1_ref.shape[0]):
        for h in heads:
            y_scr[c * t:(c + 1) * t, h * R_HEAD:(h + 1) * R_HEAD] = (
                _dot3(y1_ref[c, h], states[h]) + y2_ref[c, h])
        states = [_dot3(g_ref[c, h], states[h]) + f_ref[c, h] for h in heads]
    for h in heads:
        h_scr[h] = states[h]
    y_ref[...] = _rwkv_post(y_scr[...], bonus_ref[...], gate_ref[...], lng_ref[...], lnb_ref[...])

    @pl.when(j == pl.num_programs(1) - 1)
    def _():
        hfin_ref[0] = h_scr[...]


RWKV_SCAN_CHUNKS = 4


def _rwkv_scan(y1, y2, g, f, bonus, gate, lng, lnb, h0, nb, seq, t):
    per = RWKV_SCAN_CHUNKS if (seq // t) % RWKV_SCAN_CHUNKS == 0 else 1
    nc = seq // (t * per)
    yspec = pl.BlockSpec((per, R_HEADS, t, R_HEAD), lambda b, j: (b * nc + j, 0, 0, 0))
    gspec = pl.BlockSpec((per, R_HEADS, R_HEAD, R_HEAD), lambda b, j: (b * nc + j, 0, 0, 0))
    row = pl.BlockSpec((per * t, R_WIDTH), lambda b, j: (b * nc + j, 0))
    hspec = pl.BlockSpec((1, R_HEADS, R_HEAD, R_HEAD), lambda b, j: (b, 0, 0, 0))
    return pl.pallas_call(
        _rwkv_scan_kernel,
        grid=(nb, nc),
        in_specs=[yspec, yspec, gspec, gspec, row, row, _full((1, R_WIDTH)), _full((1, R_WIDTH)), hspec],
        out_specs=[row, hspec],
        out_shape=[jax.ShapeDtypeStruct((nb * seq, R_WIDTH), F32),
                   jax.ShapeDtypeStruct((nb, R_HEADS, R_HEAD, R_HEAD), F32)],
        scratch_shapes=[pltpu.VMEM((R_HEADS, R_HEAD, R_HEAD), F32), pltpu.VMEM((per * t, R_WIDTH), F32)],
        compiler_params=_cparams("parallel", "arbitrary"),
        name="rwkv_scan",
    )(y1, y2, g, f, bonus, gate, lng, lnb, h0)


RWKV_STEP_ROWS = 8


def _rwkv_step_kernel(r_ref, lw_ref, k_ref, v_ref, a_ref, b_ref, s_ref, y_ref, so_ref):
    ii = lax.broadcasted_iota(jnp.int32, (R_HEAD, R_HEAD), 0)
    jj = lax.broadcasted_iota(jnp.int32, (R_HEAD, R_HEAD), 1)
    eye = ii == jj
    for n in range(RWKV_STEP_ROWS):
        one = lambda ref: ref[n:n + 1, :]
        s = s_ref[n]
        sa = jnp.sum(s * one(a_ref), -1, keepdims=True)
        v_col = jnp.sum(jnp.where(eye, one(v_ref), 0.0), -1, keepdims=True)
        s_new = s * jnp.exp(one(lw_ref)) + sa * one(b_ref) + v_col * one(k_ref)
        y_col = jnp.sum(s_new * one(r_ref), -1, keepdims=True)
        so_ref[n] = s_new
        y_ref[n:n + 1, :] = jnp.sum(jnp.where(eye, y_col, 0.0), 0, keepdims=True)


def _rwkv_step(vecs, s0):
    n = s0.shape[0]
    vspec = pl.BlockSpec((RWKV_STEP_ROWS, R_HEAD), lambda i: (i, 0))
    sspec = pl.BlockSpec((RWKV_STEP_ROWS, R_HEAD, R_HEAD), lambda i: (i, 0, 0))
    return pl.pallas_call(
        _rwkv_step_kernel,
        grid=(n // RWKV_STEP_ROWS,),
        in_specs=[vspec] * 6 + [sspec],
        out_specs=[vspec, sspec],
        out_shape=[jax.ShapeDtypeStruct((n, R_HEAD), F32), jax.ShapeDtypeStruct((n, R_HEAD, R_HEAD), F32)],
        compiler_params=_cparams("parallel"),
        name="rwkv_step",
    )(*vecs, s0)


def _rwkv_post_kernel(y_ref, bonus_ref, gate_ref, lng_ref, lnb_ref, o_ref):
    o_ref[...] = _rwkv_post(y_ref[...], bonus_ref[...], gate_ref[...], lng_ref[...], lnb_ref[...])


def _rwkv_post_call(y, bonus, gate, lng, lnb):
    return pl.pallas_call(
        _rwkv_post_kernel,
        out_shape=jax.ShapeDtypeStruct(y.shape, F32),
        compiler_params=_cparams(),
        name="rwkv_post_step",
    )(y, bonus, gate, lng, lnb)


def _moba_kernel(qt_ref, k_ref, vt_ref, o_ref, km_scr, bias_scr):
    qi = pl.program_id(2)
    nblk = km_scr.shape[0]
    cols = A_GROUP * MOBA_BLOCK

    @pl.when(qi == 0)
    def _():
        km_scr[...] = jnp.mean(k_ref[0].reshape(nblk, MOBA_BLOCK, A_HEAD), axis=1)

    qt = jnp.concatenate([qt_ref[g * A_HEAD:(g + 1) * A_HEAD, :] for g in range(A_GROUP)], axis=1)
    qb = (qt * ATT_SCALE).astype(BF16)
    blk = lax.broadcasted_iota(jnp.int32, (nblk, cols), 0)
    gate = jnp.where(blk < qi, _bdot(km_scr[...], qb), NEG_INF)
    sel = jnp.zeros(gate.shape, jnp.bool_)
    for _ in range(MOBA_TOPK):
        m = jnp.max(gate, 0, keepdims=True)
        idx = jnp.min(jnp.where(gate == m, blk, nblk), 0, keepdims=True)
        pick = (blk == idx) & (m > NEG_INF)
        sel = sel | pick
        gate = jnp.where(pick, NEG_INF, gate)
    bias_scr[...] = jnp.where(sel, 0.0, NEG_INF)

    def scores(n):
        off = pl.multiple_of(n * MOBA_BLOCK, MOBA_BLOCK)
        return _bdot(k_ref[0, pl.ds(off, MOBA_BLOCK), :], qb)

    def values_t(n):
        off = pl.multiple_of(n * MOBA_BLOCK, MOBA_BLOCK)
        return vt_ref[:, pl.ds(off, MOBA_BLOCK)].astype(BF16)

    groups = MOBA_BLOCK // MOBA_ROWS

    def softmax_step(s, bias, m, l):
        rows, peak = [], None
        for r in range(groups):
            sr = s[r * MOBA_ROWS:(r + 1) * MOBA_ROWS, :] + bias(r)
            rows.append(sr)
            peak = sr if peak is None else jnp.maximum(peak, sr)
        m_new = jnp.maximum(m, jnp.max(peak, 0, keepdims=True))
        alpha = jnp.exp(m - m_new)
        total, ps = None, []
        for sr in rows:
            pr = jnp.exp(sr - m_new)
            total = pr if total is None else total + pr
            ps.append(pr.astype(BF16))
        l_new = alpha * l + jnp.sum(total, 0, keepdims=True)
        return m_new, l_new, alpha, jnp.concatenate(ps, axis=0)

    key = lax.broadcasted_iota(jnp.int32, (MOBA_ROWS, cols), 0)
    pos = lax.broadcasted_iota(jnp.int32, (MOBA_ROWS, cols), 1) % MOBA_BLOCK
    causal = lambda r: jnp.where(key + r * MOBA_ROWS <= pos, 0.0, NEG_INF)
    start = jnp.full((1, cols), NEG_INF, F32)
    m0, l0, _, p = softmax_step(scores(qi), causal, start, jnp.zeros((1, cols), F32))
    acc0 = _bdot(values_t(qi), p)

    def body(j, carry):
        m, l, acc, s_even, p_prev = carry
        n0 = 2 * j
        pv_prev = _bdot(values_t(jnp.maximum(n0 - 1, 0)), p_prev)
        s_odd = scores(n0 + 1)
        bias_even = bias_scr[pl.ds(n0, 1), :]
        m, l, alpha, p_even = softmax_step(s_even, lambda r: bias_even, m, l)
        acc = alpha * (acc + pv_prev) + _bdot(values_t(n0), p_even)
        s_even = scores(jnp.minimum(n0 + 2, nblk - 1))
        bias_odd = bias_scr[pl.ds(n0 + 1, 1), :]
        m, l, alpha, p_odd = softmax_step(s_odd, lambda r: bias_odd, m, l)
        return m, l, alpha * acc, s_even, p_odd

    trips = lax.shift_right_logical(qi + 1, 1)
    init = (m0, l0, acc0, scores(0), jnp.zeros((MOBA_BLOCK, cols), BF16))
    _, l, acc, _, p_last = lax.fori_loop(0, trips, body, init)
    acc = acc + _bdot(values_t(jnp.maximum(2 * trips - 1, 0)), p_last)
    o = acc / l
    for g in range(A_GROUP):
        o_ref[g * A_HEAD:(g + 1) * A_HEAD, :] = o[:, g * MOBA_BLOCK:(g + 1) * MOBA_BLOCK]


def _moba_prompt(qt, kh, vt, nb, seq):
    nblk = seq // MOBA_BLOCK
    qspec = pl.BlockSpec((A_GROUP * A_HEAD, MOBA_BLOCK), lambda b, h, i: (h, b * nblk + i))
    return pl.pallas_call(
        _moba_kernel,
        grid=(nb, A_KV_HEADS, nblk),
        in_specs=[qspec,
                  pl.BlockSpec((1, seq, A_HEAD), lambda b, h, i: (h, b, 0)),
                  pl.BlockSpec((A_HEAD, seq), lambda b, h, i: (h, b))],
        out_specs=qspec,
        out_shape=jax.ShapeDtypeStruct(qt.shape, F32),
        scratch_shapes=[pltpu.VMEM((nblk, A_HEAD), F32), pltpu.VMEM((nblk, A_GROUP * MOBA_BLOCK), F32)],
        compiler_params=_cparams("parallel", "parallel", "arbitrary"),
        name="moba_prompt",
    )(qt, kh, vt)


def _pages_t(cache):
    return cache.transpose(0, 1, 3, 4, 2).reshape(-1, A_KVW, PAGE_SIZE)


def _kmeans_kernel(n_in, pt_ref, *refs):
    pages, o_ref = refs[:n_in], refs[n_in]
    j = pl.program_id(1)
    per_blk = MOBA_BLOCK // PAGE_SIZE
    blocks = n_in // per_blk

    @pl.when(j == 0)
    def _():
        o_ref[...] = jnp.zeros(o_ref.shape, F32)

    out = o_ref[0]
    lane = lax.broadcasted_iota(jnp.int32, out.shape, 1)
    for i in range(blocks):
        tot = sum(pages[per_blk * i + p][0] for p in range(per_blk))
        mean = jnp.sum(tot, axis=1, keepdims=True) * (1.0 / MOBA_BLOCK)
        out = jnp.where(lane == j * blocks + i, mean, out)
    o_ref[0] = out


def _decode_kmeans(pages, page_table, layer, n_pool):
    nb, n_pages = page_table.shape
    n_in = min(16, n_pages)
    steps = n_pages // n_in
    per_blk = MOBA_BLOCK // PAGE_SIZE

    def page_spec(i):
        return pl.BlockSpec((1, A_KVW, PAGE_SIZE),
                            lambda b, j, pt: (layer * n_pool + pt[b * n_pages + j * n_in + i], 0, 0))

    return pl.pallas_call(
        functools.partial(_kmeans_kernel, n_in),
        grid_spec=pltpu.PrefetchScalarGridSpec(
            num_scalar_prefetch=1,
            grid=(nb, steps),
            in_specs=[page_spec(i) for i in range(n_in)],
            out_specs=pl.BlockSpec((1, A_KVW, n_pages // per_blk), lambda b, j, pt: (b, 0, 0)),
        ),
        out_shape=jax.ShapeDtypeStruct((nb, A_KVW, n_pages // per_blk), F32),
        compiler_params=_cparams("parallel", "arbitrary"),
        name="decode_kmeans",
    )(page_table.reshape(-1), *([pages] * n_in))


def _decode_select_kernel(q_ref, km_ref, o_ref):
    nblk = km_ref.shape[2]
    gate = _bdot(q_ref[0], km_ref[0])
    col = lax.broadcasted_iota(jnp.int32, gate.shape, 1)
    lane = lax.broadcasted_iota(jnp.int32, (A_HEADS, LANES), 1)
    out = jnp.zeros((A_HEADS, LANES), jnp.int32)
    for r in range(MOBA_TOPK):
        m = jnp.max(gate, -1, keepdims=True)
        idx = jnp.min(jnp.where(gate == m, col, nblk), -1, keepdims=True)
        out = jnp.where(lane == r, idx, out)
        gate = jnp.where(col == idx, NEG_INF, gate)
    o_ref[0] = out


def _decode_select(q_exp, kmeans):
    nb, _, nblk = kmeans.shape
    return pl.pallas_call(
        _decode_select_kernel,
        grid=(nb,),
        in_specs=[pl.BlockSpec((1, A_HEADS, A_KVW), lambda b: (b, 0, 0)),
                  pl.BlockSpec((1, A_KVW, nblk), lambda b: (b, 0, 0))],
        out_specs=pl.BlockSpec((1, A_HEADS, LANES), lambda b: (b, 0, 0)),
        out_shape=jax.ShapeDtypeStruct((nb, A_HEADS, LANES), jnp.int32),
        compiler_params=_cparams("parallel"),
        name="decode_select",
    )(q_exp, kmeans)


DEC_PAGES = MOBA_TOPK * (MOBA_BLOCK // PAGE_SIZE)


def _decode_attn_kernel(pg_ref, q_ref, kn_ref, vn_ref, *refs):
    k_pages, v_pages, o_ref = refs[:DEC_PAGES], refs[DEC_PAGES:2 * DEC_PAGES], refs[2 * DEC_PAGES]
    h = pl.program_id(1)
    q = q_ref[0, pl.ds(h, 1), :].astype(BF16)
    rb = lambda x: x.astype(BF16).astype(F32)
    s_self = jnp.sum(rb(q) * rb(kn_ref[0]), -1, keepdims=True) * ATT_SCALE
    scores = [_bdot(q, kp[0]) * ATT_SCALE for kp in k_pages]
    m = s_self
    for s in scores:
        m = jnp.maximum(m, jnp.max(s, -1, keepdims=True))
    p_self = jnp.exp(s_self - m)
    l = p_self
    acc = rb(p_self) * rb(vn_ref[0])
    for s, vp in zip(scores, v_pages):
        p = jnp.exp(s - m)
        l = l + jnp.sum(p, -1, keepdims=True)
        acc = acc + _bdot(p, vp[0], _NT)
    o_ref[0, pl.ds(h, 1), :] = acc / l


def _decode_attn(q_exp, k_new, v_new, kp, vp, page_ids, layer, n_pool):
    nb = q_exp.shape[0]

    def page_spec(i):
        return pl.BlockSpec(
            (1, A_KVW, PAGE_SIZE),
            lambda b, h, pg: (layer * n_pool + pg[(b * A_HEADS + h) * DEC_PAGES + i], 0, 0))

    bspec = pl.BlockSpec((1, A_HEADS, A_KVW), lambda b, h, pg: (b, 0, 0))
    nspec = pl.BlockSpec((1, 1, A_KVW), lambda b, h, pg: (b, 0, 0))
    return pl.pallas_call(
        _decode_attn_kernel,
        grid_spec=pltpu.PrefetchScalarGridSpec(
            num_scalar_prefetch=1,
            grid=(nb, A_HEADS),
            in_specs=[bspec, nspec, nspec] + [page_spec(i) for i in range(DEC_PAGES)] * 2,
            out_specs=bspec,
        ),
        out_shape=jax.ShapeDtypeStruct((nb, A_HEADS, A_KVW), F32),
        compiler_params=_cparams("parallel", "arbitrary"),
        name="decode_attn",
    )(page_ids.reshape(-1), q_exp, k_new.reshape(nb, 1, A_KVW), v_new.reshape(nb, 1, A_KVW),
      *([kp] * DEC_PAGES), *([vp] * DEC_PAGES))


def _moba_decode(q, k_new, v_new, k_pages, v_pages, n_pool, page_table, layer):
    nb = q.shape[0]
    onehot = (jnp.arange(A_KV_HEADS)[None, :] == (jnp.arange(A_HEADS) // A_GROUP)[:, None]).astype(F32)
    q_exp = (q.reshape(nb, A_HEADS, 1, A_HEAD) * onehot[None, :, :, None]).reshape(nb, A_HEADS, A_KVW)
    kmeans = _decode_kmeans(k_pages, page_table, layer, n_pool)
    blk = _decode_select(q_exp, kmeans)[:, :, :MOBA_TOPK]
    per_blk = MOBA_BLOCK // PAGE_SIZE
    pages_of = (blk[..., None] * per_blk + jnp.arange(per_blk)).reshape(nb, A_HEADS * DEC_PAGES)
    page_ids = jnp.take_along_axis(page_table, pages_of, axis=1)
    o = _decode_attn(q_exp, k_new, v_new, k_pages, v_pages, page_ids, layer, n_pool)
    o = o.reshape(nb, A_HEADS, A_KV_HEADS, A_HEAD)
    return jnp.einsum("bhkd,hk->bhd", o, onehot).reshape(nb, A_QW)


def _merge_kernel(ya_transposed, x_ref, ys_ref, yr_ref, ya_ref, wg_ref, ps_ref, pr_ref, pa_ref, wo_ref,
                  g_ref, b_ref, o_ref):
    x = x_ref[...]
    xb = x.astype(BF16)
    ya = ya_ref[...].T if ya_transposed else ya_ref[...]
    merged = None
    for i, (y, p_ref) in enumerate(((ys_ref[...], ps_ref), (yr_ref[...], pr_ref), (ya, pa_ref))):
        gate = jax.nn.sigmoid(_bdot(xb, wg_ref[:, i * D_MODEL:(i + 1) * D_MODEL]))
        term = gate * _bdot(y, p_ref[...])
        merged = term if merged is None else merged + term
    o_ref[...] = _ln(ALPHA * x + _bdot(merged, wo_ref[...]), g_ref[...], b_ref[...])


def _merge(x, ys, yr, ya, wg, ps, pr, pa, wo, g, b, tm, ya_transposed):
    m = x.shape[0]
    row = lambda width: pl.BlockSpec((tm, width), lambda i: (i, 0))
    ya_spec = pl.BlockSpec((A_QW, tm), lambda i: (0, i)) if ya_transposed else row(A_QW)
    return pl.pallas_call(
        functools.partial(_merge_kernel, ya_transposed),
        grid=(m // tm,),
        in_specs=[row(D_MODEL), row(S_WIDTH), row(R_WIDTH), ya_spec,
                  _full(wg.shape), _full(ps.shape), _full(pr.shape), _full(pa.shape), _full(wo.shape),
                  _full((1, D_MODEL)), _full((1, D_MODEL))],
        out_specs=row(D_MODEL),
        out_shape=jax.ShapeDtypeStruct((m, D_MODEL), F32),
        compiler_params=_cparams("parallel"),
        name="merge",
    )(x, ys, yr, ya, wg, ps, pr, pa, wo, g, b)


FFN_COLS = D_FF // 2


def _ffn_kernel(tiles_per_seq, x_ref, halo_ref, wup_ref, cw_ref, cb_ref, wdn_ref, g_ref, b_ref, o_ref):
    i = pl.program_id(0)
    x = x_ref[...]
    tm = x.shape[0]
    xe = jnp.concatenate([halo_ref[...], x], axis=0).astype(BF16)
    row = lax.broadcasted_iota(jnp.int32, (tm + SUBLANES, 1), 0)
    keep = jnp.logical_or(row >= SUBLANES, i % tiles_per_seq != 0)
    acc = jnp.zeros((tm, D_MODEL), F32)
    for c in range(D_FF // FFN_COLS):
        halves = []
        for off in (c * FFN_COLS, D_FF + c * FFN_COLS):
            up = jnp.where(keep, _bdot(xe, wup_ref[:, off:off + FFN_COLS]), 0.0)
            cv = cb_ref[:, off:off + FFN_COLS]
            for j in range(CONV_W):
                lo = SUBLANES - (CONV_W - 1) + j
                cv = cv + cw_ref[j:j + 1, off:off + FFN_COLS] * up[lo:lo + tm, :]
            halves.append(cv)
        hmid = jax.nn.gelu(halves[0]) * halves[1]
        acc = acc + _bdot(hmid, wdn_ref[c * FFN_COLS:(c + 1) * FFN_COLS, :])
    o_ref[...] = _ln(ALPHA * x + acc, g_ref[...], b_ref[...])


def _ffn(x, wup, cw, cb, wdn, g, b, seq, tm):
    m = x.shape[0]
    per8 = tm // SUBLANES
    return pl.pallas_call(
        functools.partial(_ffn_kernel, seq // tm),
        grid=(m // tm,),
        in_specs=[pl.BlockSpec((tm, D_MODEL), lambda i: (i, 0)),
                  pl.BlockSpec((SUBLANES, D_MODEL), lambda i: (jnp.maximum(i * per8 - 1, 0), 0)),
                  _full(wup.shape), _full(cw.shape), _full(cb.shape), _full(wdn.shape),
                  _full((1, D_MODEL)), _full((1, D_MODEL))],
        out_specs=pl.BlockSpec((tm, D_MODEL), lambda i: (i, 0)),
        out_shape=jax.ShapeDtypeStruct((m, D_MODEL), F32),
        compiler_params=_cparams("parallel"),
        name="ffn",
    )(x, x, wup, cw, cb, wdn, g, b)


def _ffn_step_kernel(x_ref, c0a_ref, c0b_ref, wup_ref, cw_ref, cb_ref, wdn_ref, g_ref, b_ref, o_ref, up_ref):
    x = x_ref[...]
    up = _bdot(x, wup_ref[...])
    up_ref[...] = up
    cv = cb_ref[...] + cw_ref[0:1, :] * c0a_ref[...] + cw_ref[1:2, :] * c0b_ref[...] + cw_ref[2:3, :] * up
    hmid = jax.nn.gelu(cv[:, :D_FF]) * cv[:, D_FF:]
    o_ref[...] = _ln(ALPHA * x + _bdot(hmid, wdn_ref[...]), g_ref[...], b_ref[...])


def _ffn_step(x, c0a, c0b, wup, cw, cb, wdn, g, b):
    n = x.shape[0]
    return pl.pallas_call(
        _ffn_step_kernel,
        out_shape=[jax.ShapeDtypeStruct((n, D_MODEL), F32), jax.ShapeDtypeStruct((n, 2 * D_FF), F32)],
        compiler_params=_cparams(),
        name="ffn_step",
    )(x, c0a, c0b, wup, cw, cb, wdn, g, b)


def _up_rows_kernel(x_ref, w_ref, o_ref):
    o_ref[...] = _bdot(x_ref[...], w_ref[...])


def _up_rows(x, w):
    return pl.pallas_call(
        _up_rows_kernel,
        out_shape=jax.ShapeDtypeStruct((x.shape[0], w.shape[1]), F32),
        compiler_params=_cparams(),
        name="ffn_up_tail",
    )(x, w)


def _layer_params(l, P):
    bf = lambda x: x.astype(BF16)
    row = lambda x: x.reshape(1, -1)
    w_in = P["w_in"][l]
    return dict(
        w_gate=bf(w_in[:, :O_SSM]), w_rest=bf(w_in[:, O_SSM:]),
        s5=_s5_params(P["ssm_a_re"][l], P["ssm_a_im"][l], P["ssm_log_dt"][l], P["ssm_b_re"][l],
                      P["ssm_b_im"][l], P["ssm_c_re"][l], P["ssm_c_im"][l]),
        ssm_d=row(P["ssm_d"][l]), w_glu=bf(P["ssm_w_glu"][l]), b_glu=row(P["ssm_b_glu"][l]),
        rwkv=_rwkv_weights(P["rwkv_mu"][l], P["rwkv_w0"][l], P["rwkv_w2"][l], P["rwkv_a0"][l],
                           P["rwkv_a2"][l], P["rwkv_g2"][l], P["rwkv_k_k"][l], P["rwkv_k_a"][l],
                           P["rwkv_r_k"][l]),
        lnx_g=row(P["rwkv_lnx_g"][l]), lnx_b=row(P["rwkv_lnx_b"][l]),
        proj_ssm=bf(P["proj_ssm"][l]), proj_rwkv=bf(P["proj_rwkv"][l]), proj_attn=bf(P["proj_attn"][l]),
        w_o=bf(P["w_o"][l]), ln1_g=row(P["ln1_g"][l]), ln1_b=row(P["ln1_b"][l]),
        w_up=bf(P["ffn_w_up"][l]), conv_w=P["ffn_conv_w"][l], conv_b=row(P["ffn_conv_b"][l]),
        w_down=bf(P["ffn_w_down"][l]), ln2_g=row(P["ln2_g"][l]), ln2_b=row(P["ln2_b"][l]),
    )


def _pick_tile(n, pref):
    t = min(pref, n)
    while n % t:
        t //= 2
    return t


def _prompt_trunk(x_prompt, P, layers):
    nb, seq, _ = x_prompt.shape
    m = nb * seq
    x = x_prompt.reshape(m, D_MODEL)
    tm = _pick_tile(seq, 512)
    rope = _rope_tables(jnp.arange(seq, dtype=jnp.int32))
    ln_g, ln_b = P["ln_in_g"].reshape(1, -1), P["ln_in_b"].reshape(1, -1)
    outs = []
    for l, lp in enumerate(layers):
        res = _in_proj(x, ln_g, ln_b, lp["w_rest"], rope, tm, seq // tm, pre_ln=(l == 0), attn_layouts=True)
        u, c, k, v, qt, kh, vt = res[:7]
        if l == 0:
            x = res[7]
        y_s, s_fin = _s5_prompt(u, lp["s5"], lp["ssm_d"], lp["w_glu"], lp["b_glu"],
                                jnp.zeros((nb, 2, S_LANES), F32), nb, seq, _pick_tile(seq, 256))
        r, lw, k2, vv, a, b, gate, bonus = _rwkv_pre(c, jnp.zeros((nb, R_IN), F32), lp["rwkv"], nb, seq,
                                                     _pick_tile(seq, 256), decode=False)
        y1, y2, g, f = _rwkv_chunks(r, lw, k2, vv, a, b, RWKV_CHUNK)
        y_r, h_fin = _rwkv_scan(y1, y2, g, f, bonus, gate, lp["lnx_g"], lp["lnx_b"],
                                jnp.zeros((nb, R_HEADS, R_HEAD, R_HEAD), F32), nb, seq, RWKV_CHUNK)
        y_at = _moba_prompt(qt, kh, vt, nb, seq)
        x = _merge(x, y_s, y_r, y_at, lp["w_gate"], lp["proj_ssm"], lp["proj_rwkv"], lp["proj_attn"],
                   lp["w_o"], lp["ln1_g"], lp["ln1_b"], _pick_tile(seq, 256), ya_transposed=True)
        tail = x.reshape(nb, seq, D_MODEL)[:, seq - (CONV_W - 1):].reshape(nb * (CONV_W - 1), D_MODEL)
        conv = _up_rows(tail, lp["w_up"]).reshape(nb, CONV_W - 1, 2 * D_FF)
        x = _ffn(x, lp["w_up"], lp["conv_w"], lp["conv_b"], lp["w_down"], lp["ln2_g"], lp["ln2_b"],
                 seq, _pick_tile(seq, 256))
        outs.append(dict(
            k=k.reshape(nb, seq, A_KV_HEADS, A_HEAD), v=v.reshape(nb, seq, A_KV_HEADS, A_HEAD),
            s_re=s_fin[:, 0].reshape(nb, S_GROUPS, S_STATE), s_im=s_fin[:, 1].reshape(nb, S_GROUPS, S_STATE),
            rwkv=jnp.swapaxes(h_fin, -1, -2), shift=c.reshape(nb, seq, R_IN)[:, -1], conv=conv))
    return x.reshape(nb, seq, D_MODEL), outs


def _sample_trunk(x_sample, P, layers, cache_k, cache_v, page_table, st_re, st_im, st_rwkv, st_shift, st_conv):
    nb = x_sample.shape[0]
    x = x_sample.reshape(nb, D_MODEL)
    past_len = page_table.shape[1] * PAGE_SIZE
    rope = _rope_tables(jnp.full((nb,), past_len, jnp.int32))
    ln_g, ln_b = P["ln_in_g"].reshape(1, -1), P["ln_in_b"].reshape(1, -1)
    k_pages, v_pages = _pages_t(cache_k), _pages_t(cache_v)
    outs = []
    for l, lp in enumerate(layers):
        res = _in_proj(x, ln_g, ln_b, lp["w_rest"], rope, nb, 1, pre_ln=(l == 0), attn_layouts=False)
        u, c, k, v, q = res[:5]
        if l == 0:
            x = res[5]
        s0 = jnp.concatenate([st_re[l].reshape(nb, S_LANES), st_im[l].reshape(nb, S_LANES)], axis=1)
        y_s, s_new = _s5_step(u, lp["s5"], lp["ssm_d"], lp["w_glu"], lp["b_glu"], s0)
        r, lw, k2, vv, a, b, gate, bonus = _rwkv_pre(c, st_shift[l], lp["rwkv"], nb, 1, nb, decode=True)
        per_head = lambda t: t.reshape(nb * R_HEADS, R_HEAD)
        y_rows, s_rwkv = _rwkv_step([per_head(t) for t in (r, lw, k2, vv, a, b)],
                                    st_rwkv[l].reshape(nb * R_HEADS, R_HEAD, R_HEAD))
        y_r = _rwkv_post_call(y_rows.reshape(nb, R_WIDTH), bonus, gate, lp["lnx_g"], lp["lnx_b"])
        y_a = _moba_decode(q, k, v, k_pages, v_pages, cache_k.shape[1], page_table, l)
        x = _merge(x, y_s, y_r, y_a, lp["w_gate"], lp["proj_ssm"], lp["proj_rwkv"], lp["proj_attn"],
                   lp["w_o"], lp["ln1_g"], lp["ln1_b"], nb, ya_transposed=False)
        x, up = _ffn_step(x, st_conv[l][:, 0], st_conv[l][:, 1], lp["w_up"], lp["conv_w"], lp["conv_b"],
                          lp["w_down"], lp["ln2_g"], lp["ln2_b"])
        outs.append(dict(
            k=k.reshape(nb, 1, A_KV_HEADS, A_HEAD), v=v.reshape(nb, 1, A_KV_HEADS, A_HEAD),
            s_re=s_new[:, :S_LANES].reshape(nb, S_GROUPS, S_STATE),
            s_im=s_new[:, S_LANES:].reshape(nb, S_GROUPS, S_STATE),
            rwkv=s_rwkv.reshape(nb, R_HEADS, R_HEAD, R_HEAD), shift=c,
            conv=jnp.stack([st_conv[l][:, 1], up], axis=1)))
    return x.reshape(nb, 1, D_MODEL), outs


def kernel(x_prompt, x_sample, cache_k, cache_v, page_table, state_ssm_re, state_ssm_im, state_rwkv, state_rwkv_shift, state_conv, ln_in_g, ln_in_b, w_in, ssm_a_re, ssm_a_im, ssm_log_dt, ssm_b_re, ssm_b_im, ssm_c_re, ssm_c_im, ssm_d, ssm_w_glu, ssm_b_glu, rwkv_mu, rwkv_w0, rwkv_w2, rwkv_a0, rwkv_a2, rwkv_g2, rwkv_k_k, rwkv_k_a, rwkv_r_k, rwkv_lnx_g, rwkv_lnx_b, proj_ssm, proj_rwkv, proj_attn, w_o, ln1_g, ln1_b, ffn_w_up, ffn_conv_w, ffn_conv_b, ffn_w_down, ln2_g, ln2_b):
    P = dict(ln_in_g=ln_in_g, ln_in_b=ln_in_b, w_in=w_in,
             ssm_a_re=ssm_a_re, ssm_a_im=ssm_a_im, ssm_log_dt=ssm_log_dt,
             ssm_b_re=ssm_b_re, ssm_b_im=ssm_b_im, ssm_c_re=ssm_c_re, ssm_c_im=ssm_c_im,
             ssm_d=ssm_d, ssm_w_glu=ssm_w_glu, ssm_b_glu=ssm_b_glu,
             rwkv_mu=rwkv_mu, rwkv_w0=rwkv_w0, rwkv_w2=rwkv_w2, rwkv_a0=rwkv_a0, rwkv_a2=rwkv_a2,
             rwkv_g2=rwkv_g2, rwkv_k_k=rwkv_k_k, rwkv_k_a=rwkv_k_a, rwkv_r_k=rwkv_r_k,
             rwkv_lnx_g=rwkv_lnx_g, rwkv_lnx_b=rwkv_lnx_b,
             proj_ssm=proj_ssm, proj_rwkv=proj_rwkv, proj_attn=proj_attn, w_o=w_o,
             ln1_g=ln1_g, ln1_b=ln1_b, ffn_w_up=ffn_w_up, ffn_conv_w=ffn_conv_w,
             ffn_conv_b=ffn_conv_b, ffn_w_down=ffn_w_down, ln2_g=ln2_g, ln2_b=ln2_b)
    layers = [_layer_params(l, P) for l in range(w_in.shape[0])]
    y_p, op = _prompt_trunk(x_prompt, P, layers)
    y_s, os_ = _sample_trunk(x_sample, P, layers, cache_k, cache_v, page_table, state_ssm_re, state_ssm_im,
                             state_rwkv, state_rwkv_shift, state_conv)
    st = lambda outs, key: jnp.stack([o[key] for o in outs])
    return (y_p, y_s, st(op, "k"), st(op, "v"), st(os_, "k"), st(os_, "v"),
            st(op, "s_re"), st(op, "s_im"), st(os_, "s_re"), st(os_, "s_im"),
            st(op, "rwkv"), st(os_, "rwkv"), st(op, "shift"), st(os_, "shift"),
            st(op, "conv"), st(os_, "conv"))
```

```python
import functools
import math

import jax
import jax.numpy as jnp
from jax import lax
from jax.experimental import pallas as pl
from jax.experimental.pallas import tpu as pltpu

F32 = jnp.float32
BF16 = jnp.bfloat16

D_MODEL = 1024
PAGE_SIZE = 128
S_GROUP = 16
S_GROUPS = 16
S_STATE = 64
S_WIDTH = S_GROUPS * S_GROUP
S_LANES = S_GROUPS * S_STATE
R_HEAD = 64
R_HEADS = 4
R_WIDTH = R_HEADS * R_HEAD
W_LORA = 32
A_LORA = 32
G_LORA = 64
N_LORA = W_LORA + A_LORA + G_LORA
R_IN = 3 * R_WIDTH + N_LORA
GN_EPS = 64e-5
A_HEAD = 64
A_HEADS = 8
A_KV_HEADS = 4
A_GROUP = A_HEADS // A_KV_HEADS
A_QW = A_HEADS * A_HEAD
A_KVW = A_KV_HEADS * A_HEAD
ROT_DIM = A_HEAD // 4
ROPE_THETA = 500000.0
MOBA_BLOCK = 256
MOBA_TOPK = 3
N_BRANCH = 3
O_SSM = N_BRANCH * D_MODEL
N_REST = S_WIDTH + R_IN + A_QW + 2 * A_KVW
D_FF = 2816
CONV_W = 3
DEPTH = 2
ALPHA = (2 * DEPTH) ** 0.25
LN_EPS = 1e-5
ATT_SCALE = A_HEAD ** -0.5
LOG2_E = math.log2(math.e)

VMEM_LIMIT_BYTES = 56 * 1024 * 1024
LANES = 128
SUBLANES = 8

MOBA_ROWS = 16
RWKV_CHUNK = 64
NEG_INF = float("-inf")

_NN = (((1,), (0,)), ((), ()))
_NT = (((1,), (1,)), ((), ()))
_TN = (((0,), (0,)), ((), ()))


def _cparams(*sem):
    return pltpu.CompilerParams(dimension_semantics=sem or None, vmem_limit_bytes=VMEM_LIMIT_BYTES)


def _bdot(a, b, dims=_NN):
    return lax.dot_general(a.astype(BF16), b.astype(BF16), dims, preferred_element_type=F32)


def _dot3(a, b, dims=_NN):
    ah = a.astype(BF16)
    al = (a - ah.astype(F32)).astype(BF16)
    bh = b.astype(BF16)
    bl = (b - bh.astype(F32)).astype(BF16)
    f = lambda x, y: lax.dot_general(x, y, dims, preferred_element_type=F32)
    return f(ah, bh) + f(ah, bl) + f(al, bh)


def _ln(x, g, b):
    mu = jnp.mean(x, -1, keepdims=True)
    xc = x - mu
    var = jnp.mean(xc * xc, -1, keepdims=True)
    return xc * lax.rsqrt(var + LN_EPS) * g + b


def _head_sum(x, head):
    lane_head = lax.broadcasted_iota(jnp.int32, (1, x.shape[1]), 1) // head
    out = jnp.zeros_like(x)
    for h in range(x.shape[1] // head):
        m = lane_head == h
        s = jnp.sum(jnp.where(m, x, 0.0), -1, keepdims=True)
        out = jnp.where(m, s, out)
    return out


def _full(shape):
    n = len(shape)
    return pl.BlockSpec(shape, lambda *_: (0,) * n)


def _resident(shape):
    n = len(shape)
    return pl.BlockSpec(shape, lambda *_: (0,) * n, pipeline_mode=pl.Buffered(1))


def _in_proj_kernel(pre_ln, attn_layouts, x_ref, g_ref, b_ref, w_ref, cos_ref, sa_ref, sb_ref,
                    u_ref, c_ref, k_ref, v_ref, *rest):
    x = x_ref[...]
    if pre_ln:
        x = _ln(x, g_ref[...], b_ref[...])
        rest[-1][...] = x
    h = _bdot(x, w_ref[...])
    u_ref[...] = h[:, :S_WIDTH]
    c_ref[...] = h[:, S_WIDTH:S_WIDTH + R_IN]
    cos, sa, sb = cos_ref[...], sa_ref[...], sb_ref[...]
    base = S_WIDTH + R_IN
    nq = A_QW // LANES
    per = LANES // A_HEAD
    for j in range((A_QW + A_KVW) // LANES):
        ch = h[:, base + LANES * j: base + LANES * (j + 1)]
        rot = (ch * cos + pltpu.roll(ch, LANES - ROT_DIM // 2, 1) * sa
               + pltpu.roll(ch, ROT_DIM // 2, 1) * sb)
        if j < nq:
            if attn_layouts:
                rest[0][LANES * j:LANES * (j + 1), :] = rot.T
            else:
                rest[0][:, LANES * j:LANES * (j + 1)] = rot
        else:
            k_ref[:, LANES * (j - nq):LANES * (j - nq + 1)] = rot
            if attn_layouts:
                for i in range(per):
                    rest[1][per * (j - nq) + i] = rot[:, A_HEAD * i:A_HEAD * (i + 1)]
    v = h[:, base + A_QW + A_KVW:]
    v_ref[...] = v
    if attn_layouts:
        for j in range(A_KVW // LANES):
            rest[2][LANES * j:LANES * (j + 1), :] = v[:, LANES * j:LANES * (j + 1)].T


def _in_proj(x, ln_g, ln_b, w, rope, tm, pos_tiles, pre_ln, attn_layouts):
    m = x.shape[0]
    row = lambda width: pl.BlockSpec((tm, width), lambda i: (i, 0))
    col = lambda width: pl.BlockSpec((width, tm), lambda i: (0, i))
    tab = pl.BlockSpec((tm, LANES), lambda i: (i % pos_tiles, 0))
    f32 = lambda *shape: jax.ShapeDtypeStruct(shape, F32)
    specs = [row(S_WIDTH), row(R_IN), row(A_KVW), row(A_KVW)]
    shapes = [f32(m, S_WIDTH), f32(m, R_IN), f32(m, A_KVW), f32(m, A_KVW)]
    if attn_layouts:
        specs += [col(A_QW), pl.BlockSpec((A_KV_HEADS, tm, A_HEAD), lambda i: (0, i, 0)), col(A_KVW)]
        shapes += [f32(A_QW, m), f32(A_KV_HEADS, m, A_HEAD), f32(A_KVW, m)]
    else:
        specs += [row(A_QW)]
        shapes += [f32(m, A_QW)]
    if pre_ln:
        specs += [row(D_MODEL)]
        shapes += [f32(m, D_MODEL)]
    return pl.pallas_call(
        functools.partial(_in_proj_kernel, pre_ln, attn_layouts),
        grid=(m // tm,),
        in_specs=[row(D_MODEL), _full((1, D_MODEL)), _full((1, D_MODEL)), _resident((D_MODEL, N_REST)),
                  tab, tab, tab],
        out_specs=specs,
        out_shape=shapes,
        compiler_params=_cparams("parallel"),
        name="in_proj",
    )(x, ln_g, ln_b, w, *rope)


def _rope_tables(pos):
    half = ROT_DIM // 2
    inv = ROPE_THETA ** (-jnp.arange(half, dtype=F32) / half)
    ang = pos.astype(F32)[:, None] * inv[None, :]
    cos, sin = jnp.cos(ang), jnp.sin(ang)
    n = pos.shape[0]
    pad = jnp.zeros((n, A_HEAD - ROT_DIM), F32)
    zero = jnp.zeros((n, half), F32)
    cos_h = jnp.concatenate([cos, cos, pad + 1.0], -1)
    sa_h = jnp.concatenate([-sin, zero, pad], -1)
    sb_h = jnp.concatenate([zero, sin, pad], -1)
    rep = LANES // A_HEAD
    return tuple(jnp.tile(t, (1, rep)) for t in (cos_h, sa_h, sb_h))


def _s5_params(a_re, a_im, log_dt, b_re, b_im, c_re, c_im):
    dt = jnp.exp(log_dt)[:, None]
    lam_re, lam_im = a_re * dt, a_im * dt

    def power(k):
        mag = jnp.exp(lam_re * k)
        return (mag * jnp.cos(lam_im * k)).reshape(-1), (mag * jnp.sin(lam_im * k)).reshape(-1)

    abar_re, abar_im = power(1.0)
    den = (a_re * a_re + a_im * a_im).reshape(-1)
    ar, ai = a_re.reshape(-1), a_im.reshape(-1)
    em_re = abar_re - 1.0
    coef = jnp.stack([(em_re * ar + abar_im * ai) / den, (abar_im * ar - em_re * ai) / den])
    pw = jnp.stack([jnp.stack(x) for x in zip(*[power(float(k)) for k in range(1, SUBLANES + 1)])])
    dbl = jnp.stack([jnp.stack(x) for x in zip(*[power(float(k)) for k in (1, 2, 4)])])
    eye = jnp.eye(S_GROUPS, dtype=F32)
    wb = jnp.concatenate([jnp.einsum("gpc,gh->gchp", b, eye).reshape(S_WIDTH, S_LANES)
                          for b in (b_re, b_im)], axis=1)
    wc = jnp.concatenate([jnp.einsum("gcp,gh->gphc", c, eye).reshape(S_LANES, S_WIDTH)
                          for c in (c_re, -c_im)], axis=0)
    return dict(coef=coef, pw=pw, dbl=dbl, abar=jnp.stack([abar_re, abar_im]),
                wb=wb.astype(BF16), wc=wc.astype(BF16))


def _s5_glu(s_re, s_im, u, wc_ref, d_ref, wg_ref, bg_ref):
    s = jnp.concatenate([s_re, s_im], axis=1)
    y = _bdot(s, wc_ref[...]) + d_ref[...] * u
    z = jax.nn.gelu(y)
    return z * jax.nn.sigmoid(_bdot(z, wg_ref[...]) + bg_ref[...])


def _s5_kernel(u_ref, wb_ref, wc_ref, coef_ref, pw_ref, dbl_ref, d_ref, wg_ref, bg_ref, s0_ref,
               y_ref, sfin_ref, s_scr, carry_scr):
    j = pl.program_id(1)
    t = u_ref.shape[0]

    @pl.when(j == 0)
    def _():
        carry_scr[...] = s0_ref[0]

    u = u_ref[...]
    bu = _bdot(u, wb_ref[...])
    b_re, b_im = bu[:, :S_LANES], bu[:, S_LANES:]
    c_re, c_im = coef_ref[0:1, :], coef_ref[1:2, :]
    s_scr[0] = c_re * b_re - c_im * b_im
    s_scr[1] = c_re * b_im + c_im * b_re
    row = lax.broadcasted_iota(jnp.int32, (SUBLANES, S_LANES), 0)

    def body(g, carry):
        cr, ci = carry
        off = pl.multiple_of(g * SUBLANES, SUBLANES)
        sr = s_scr[0, pl.ds(off, SUBLANES), :]
        si = s_scr[1, pl.ds(off, SUBLANES), :]
        for n, d in enumerate((1, 2, 4)):
            ar, ai = dbl_ref[0, n:n + 1, :], dbl_ref[1, n:n + 1, :]
            pr = jnp.where(row >= d, pltpu.roll(sr, d, 0), 0.0)
            pi = jnp.where(row >= d, pltpu.roll(si, d, 0), 0.0)
            sr, si = sr + ar * pr - ai * pi, si + ar * pi + ai * pr
        p_re, p_im = pw_ref[0], pw_ref[1]
        sr, si = sr + p_re * cr - p_im * ci, si + p_re * ci + p_im * cr
        s_scr[0, pl.ds(off, SUBLANES), :] = sr
        s_scr[1, pl.ds(off, SUBLANES), :] = si
        return sr[SUBLANES - 1:SUBLANES, :], si[SUBLANES - 1:SUBLANES, :]

    cr, ci = lax.fori_loop(0, t // SUBLANES, body, (carry_scr[0:1, :], carry_scr[1:2, :]), unroll=2)
    carry_scr[0:1, :] = cr
    carry_scr[1:2, :] = ci
    y_ref[...] = _s5_glu(s_scr[0], s_scr[1], u, wc_ref, d_ref, wg_ref, bg_ref)

    @pl.when(j == pl.num_programs(1) - 1)
    def _():
        sfin_ref[0] = carry_scr[...]


def _s5_prompt(u, sp, d, w_glu, b_glu, s0, nb, seq, t):
    nt = seq // t
    return pl.pallas_call(
        _s5_kernel,
        grid=(nb, nt),
        in_specs=[pl.BlockSpec((t, S_WIDTH), lambda b, j: (b * nt + j, 0)),
                  _full((S_WIDTH, 2 * S_LANES)), _full((2 * S_LANES, S_WIDTH)), _full((2, S_LANES)),
                  _full((2, SUBLANES, S_LANES)), _full((2, 3, S_LANES)), _full((1, S_WIDTH)),
                  _full((S_WIDTH, S_WIDTH)), _full((1, S_WIDTH)),
                  pl.BlockSpec((1, 2, S_LANES), lambda b, j: (b, 0, 0))],
        out_specs=[pl.BlockSpec((t, S_WIDTH), lambda b, j: (b * nt + j, 0)),
                   pl.BlockSpec((1, 2, S_LANES), lambda b, j: (b, 0, 0))],
        out_shape=[jax.ShapeDtypeStruct((nb * seq, S_WIDTH), F32),
                   jax.ShapeDtypeStruct((nb, 2, S_LANES), F32)],
        scratch_shapes=[pltpu.VMEM((2, t, S_LANES), F32), pltpu.VMEM((2, S_LANES), F32)],
        compiler_params=_cparams("parallel", "arbitrary"),
        name="s5_scan",
    )(u, sp["wb"], sp["wc"], sp["coef"], sp["pw"], sp["dbl"], d, w_glu, b_glu, s0)


def _s5_step_kernel(u_ref, wb_ref, wc_ref, coef_ref, abar_ref, d_ref, wg_ref, bg_ref, s0_ref,
                    y_ref, s_ref):
    u = u_ref[...]
    bu = _bdot(u, wb_ref[...])
    b_re, b_im = bu[:, :S_LANES], bu[:, S_LANES:]
    c_re, c_im = coef_ref[0:1, :], coef_ref[1:2, :]
    a_re, a_im = abar_ref[0:1, :], abar_ref[1:2, :]
    s0_re, s0_im = s0_ref[:, :S_LANES], s0_ref[:, S_LANES:]
    s_re = c_re * b_re - c_im * b_im + (a_re * s0_re - a_im * s0_im)
    s_im = c_re * b_im + c_im * b_re + (a_re * s0_im + a_im * s0_re)
    s_ref[:, :S_LANES] = s_re
    s_ref[:, S_LANES:] = s_im
    y_ref[...] = _s5_glu(s_re, s_im, u, wc_ref, d_ref, wg_ref, bg_ref)


def _s5_step(u, sp, d, w_glu, b_glu, s0):
    n = u.shape[0]
    return pl.pallas_call(
        _s5_step_kernel,
        out_shape=[jax.ShapeDtypeStruct((n, S_WIDTH), F32), jax.ShapeDtypeStruct((n, 2 * S_LANES), F32)],
        compiler_params=_cparams(),
        name="s5_step",
    )(u, sp["wb"], sp["wc"], sp["coef"], sp["abar"], d, w_glu, b_glu, s0)


def _rwkv_pre_kernel(decode, c_ref, prev_ref, sh0_ref, mu_ref, w0_ref, w2_ref, a0_ref, a2_ref, g2_ref,
                     kk_ref, ka_ref, rk_ref,
                     r_o, lw_o, k_o, v_o, a_o, b_o, g_o, bonus_o):
    c = c_ref[...]
    if decode:
        prev = sh0_ref[...]
    else:
        j = pl.program_id(1)
        last = jnp.where(j == 0, sh0_ref[0], prev_ref[SUBLANES - 1:SUBLANES, :])
        row = lax.broadcasted_iota(jnp.int32, c.shape, 0)
        prev = jnp.where(row == 0, last, pltpu.roll(c, 1, 0))
    cf = c + (prev - c) * mu_ref[...]
    r = cf[:, :R_WIDTH]
    k = cf[:, R_WIDTH:2 * R_WIDTH]
    v = cf[:, 2 * R_WIDTH:3 * R_WIDTH]
    lora = cf[:, 3 * R_WIDTH:]
    w_log = -jax.nn.softplus(-(w0_ref[...] + _bdot(jnp.tanh(lora), w2_ref[...]))) - 0.5
    a = jax.nn.sigmoid(a0_ref[...] + _bdot(lora, a2_ref[...]))
    g = _bdot(jax.nn.sigmoid(lora), g2_ref[...])
    kk = k * kk_ref[...]
    kk = kk * lax.rsqrt(jnp.maximum(_head_sum(kk * kk, R_HEAD), 1e-24))
    k = k * (1.0 + (a - 1.0) * ka_ref[...])
    r_o[...] = r
    lw_o[...] = -jnp.exp(w_log)
    k_o[...] = k
    v_o[...] = v
    a_o[...] = -kk
    b_o[...] = kk * a
    g_o[...] = g
    bonus_o[...] = _head_sum(r * k * rk_ref[...], R_HEAD) * v


def _rwkv_weights(mu, w0, w2, a0, a2, g2, k_k, k_a, r_k):
    z = lambda n: jnp.zeros((n, R_WIDTH), F32)
    w2p = jnp.concatenate([w2, z(A_LORA + G_LORA)], 0).astype(BF16)
    a2p = jnp.concatenate([z(W_LORA), a2, z(G_LORA)], 0).astype(BF16)
    g2p = jnp.concatenate([z(W_LORA + A_LORA), g2], 0).astype(BF16)
    row = lambda x: x.reshape(1, -1)
    return [row(mu), row(w0), w2p, row(a0), a2p, g2p, row(k_k), row(k_a), row(r_k)]


def _rwkv_pre(c, sh0, wts, nb, seq, t, decode):
    m = c.shape[0]
    outs = [jax.ShapeDtypeStruct((m, R_WIDTH), F32)] * 8
    wspecs = [_full(w.shape) for w in wts]
    if decode:
        return pl.pallas_call(
            functools.partial(_rwkv_pre_kernel, True),
            grid=(1,),
            in_specs=[_full((m, R_IN)), _full((SUBLANES, R_IN)), _full((m, R_IN))] + wspecs,
            out_specs=[_full((m, R_WIDTH))] * 8,
            out_shape=outs,
            compiler_params=_cparams("arbitrary"),
            name="rwkv_pre_step",
        )(c, c[:SUBLANES], sh0, *wts)
    nt = seq // t
    per8 = t // SUBLANES
    return pl.pallas_call(
        functools.partial(_rwkv_pre_kernel, False),
        grid=(nb, nt),
        in_specs=[pl.BlockSpec((t, R_IN), lambda b, j: (b * nt + j, 0)),
                  pl.BlockSpec((SUBLANES, R_IN), lambda b, j: (jnp.maximum((b * nt + j) * per8 - 1, 0), 0)),
                  pl.BlockSpec((1, 1, R_IN), lambda b, j: (b, 0, 0))] + wspecs,
        out_specs=[pl.BlockSpec((t, R_WIDTH), lambda b, j: (b * nt + j, 0))] * 8,
        out_shape=outs,
        compiler_params=_cparams("parallel", "parallel"),
        name="rwkv_pre",
    )(c, c, sh0.reshape(nb, 1, R_IN), *wts)


def _rwkv_chunk_kernel(r_ref, lw_ref, k_ref, v_ref, a_ref, b_ref, y1_ref, y2_ref, g_ref, f_ref):
    t = RWKV_CHUNK
    n_chunks = r_ref.shape[0] // t
    row = lax.broadcasted_iota(jnp.int32, (t, R_WIDTH), 0)
    ri = lax.broadcasted_iota(jnp.int32, (t, t), 0)
    ci = lax.broadcasted_iota(jnp.int32, (t, t), 1)
    hi = lax.broadcasted_iota(jnp.int32, (R_HEAD, R_HEAD), 0)
    hj = lax.broadcasted_iota(jnp.int32, (R_HEAD, R_HEAD), 1)
    per_head = lambda x: [x[:, h * R_HEAD:(h + 1) * R_HEAD] for h in range(R_HEADS)]
    ah, rh, bh, kh, vh, be_t, ke_t, ee = ([] for _ in range(8))
    for c in range(n_chunks):
        rows = slice(c * t, (c + 1) * t)
        lw = lw_ref[rows, :]
        cw = lw
        d = 1
        while d < t:
            cw = cw + jnp.where(row >= d, pltpu.roll(cw, d, 0), 0.0)
            d *= 2
        cw_end = cw[t - 1:t, :]
        e_neg = jnp.exp(-cw)
        e_rem = jnp.exp(cw_end - cw)
        ah += per_head(a_ref[rows, :] * jnp.exp(cw - lw))
        rh += per_head(r_ref[rows, :] * jnp.exp(cw))
        bh += per_head(b_ref[rows, :] * e_neg)
        kh += per_head(k_ref[rows, :] * e_neg)
        vh += per_head(v_ref[rows, :])
        be_t += [x.T for x in per_head(b_ref[rows, :] * e_rem)]
        ke_t += [x.T for x in per_head(k_ref[rows, :] * e_rem)]
        ee += per_head(jnp.exp(cw_end))
    units = range(n_chunks * R_HEADS)
    a_ab = [jnp.where(ri > ci, _dot3(ah[u], bh[u], _NT), 0.0) for u in units]
    a_ak = [jnp.where(ri > ci, _dot3(ah[u], kh[u], _NT), 0.0) for u in units]
    a_rb = [jnp.where(ri >= ci, _dot3(rh[u], bh[u], _NT), 0.0) for u in units]
    a_rk = [jnp.where(ri >= ci, _dot3(rh[u], kh[u], _NT), 0.0) for u in units]
    akv = [_dot3(a_ak[u], vh[u]) for u in units]
    rkv = [_dot3(a_rk[u], vh[u]) for u in units]
    inv = [jnp.where(ri == ci, 1.0, 0.0) + a_ab[u] for u in units]
    apow = a_ab
    n = 2
    while n < t:
        apow = [_dot3(apow[u], apow[u]) for u in units]
        inv = [inv[u] + _dot3(apow[u], inv[u]) for u in units]
        n *= 2
    w1 = [_dot3(inv[u], ah[u]) for u in units]
    w2 = [_dot3(inv[u], akv[u]) for u in units]
    for u in units:
        c, h = divmod(u, R_HEADS)
        y1_ref[c, h] = rh[u] + _dot3(a_rb[u], w1[u])
        y2_ref[c, h] = _dot3(a_rb[u], w2[u]) + rkv[u]
        g_ref[c, h] = jnp.where(hi == hj, ee[u], 0.0) + _dot3(be_t[u], w1[u])
        f_ref[c, h] = _dot3(be_t[u], w2[u]) + _dot3(ke_t[u], vh[u])


RWKV_CHUNKS_PER_STEP = 2


def _rwkv_chunks(r, lw, k, v, a, b, t):
    m = r.shape[0]
    per = RWKV_CHUNKS_PER_STEP if (m // t) % RWKV_CHUNKS_PER_STEP == 0 else 1
    nc = m // t
    spec = pl.BlockSpec((per * t, R_WIDTH), lambda i: (i, 0))
    yspec = pl.BlockSpec((per, R_HEADS, t, R_HEAD), lambda i: (i, 0, 0, 0))
    gspec = pl.BlockSpec((per, R_HEADS, R_HEAD, R_HEAD), lambda i: (i, 0, 0, 0))
    return pl.pallas_call(
        _rwkv_chunk_kernel,
        grid=(nc // per,),
        in_specs=[spec] * 6,
        out_specs=[yspec, yspec, gspec, gspec],
        out_shape=[jax.ShapeDtypeStruct((nc, R_HEADS, t, R_HEAD), F32)] * 2
        + [jax.ShapeDtypeStruct((nc, R_HEADS, R_HEAD, R_HEAD), F32)] * 2,
        compiler_params=_cparams("parallel"),
        name="rwkv_chunk",
    )(r, lw, k, v, a, b)


def _rwkv_post(y, bonus, g, lng, lnb):
    m = _head_sum(y, R_HEAD) * (1.0 / R_HEAD)
    yc = y - m
    var = _head_sum(yc * yc, R_HEAD) * (1.0 / R_HEAD)
    return (yc * lax.rsqrt(var + GN_EPS) * lng + lnb + bonus) * g


def _rwkv_scan_kernel(y1_ref, y2_ref, g_ref, f_ref, bonus_ref, gate_ref, lng_ref, lnb_ref, h0_ref,
                      y_ref, hfin_ref, h_scr, y_scr):
    j = pl.program_id(1)

    @pl.when(j == 0)
    def _():
        h_scr[...] = h0_ref[0]

    t = RWKV_CHUNK
    heads = range(R_HEADS)
    states = [h_scr[h] for h in heads]
    for c in range(y1_ref.shape[0]):
        for h in heads:
            y_scr[c * t:(c + 1) * t, h * R_HEAD:(h + 1) * R_HEAD] = (
                _dot3(y1_ref[c, h], states[h]) + y2_ref[c, h])
        states = [_dot3(g_ref[c, h], states[h]) + f_ref[c, h] for h in heads]
    for h in heads:
        h_scr[h] = states[h]
    y_ref[...] = _rwkv_post(y_scr[...], bonus_ref[...], gate_ref[...], lng_ref[...], lnb_ref[...])

    @pl.when(j == pl.num_programs(1) - 1)
    def _():
        hfin_ref[0] = h_scr[...]


RWKV_SCAN_CHUNKS = 4


def _rwkv_scan(y1, y2, g, f, bonus, gate, lng, lnb, h0, nb, seq, t):
    per = RWKV_SCAN_CHUNKS if (seq // t) % RWKV_SCAN_CHUNKS == 0 else 1
    nc = seq // (t * per)
    yspec = pl.BlockSpec((per, R_HEADS, t, R_HEAD), lambda b, j: (b * nc + j, 0, 0, 0))
    gspec = pl.BlockSpec((per, R_HEADS, R_HEAD, R_HEAD), lambda b, j: (b * nc + j, 0, 0, 0))
    row = pl.BlockSpec((per * t, R_WIDTH), lambda b, j: (b * nc + j, 0))
    hspec = pl.BlockSpec((1, R_HEADS, R_HEAD, R_HEAD), lambda b, j: (b, 0, 0, 0))
    return pl.pallas_call(
        _rwkv_scan_kernel,
        grid=(nb, nc),
        in_specs=[yspec, yspec, gspec, gspec, row, row, _full((1, R_WIDTH)), _full((1, R_WIDTH)), hspec],
        out_specs=[row, hspec],
        out_shape=[jax.ShapeDtypeStruct((nb * seq, R_WIDTH), F32),
                   jax.ShapeDtypeStruct((nb, R_HEADS, R_HEAD, R_HEAD), F32)],
        scratch_shapes=[pltpu.VMEM((R_HEADS, R_HEAD, R_HEAD), F32), pltpu.VMEM((per * t, R_WIDTH), F32)],
        compiler_params=_cparams("parallel", "arbitrary"),
        name="rwkv_scan",
    )(y1, y2, g, f, bonus, gate, lng, lnb, h0)


RWKV_STEP_ROWS = 8


def _rwkv_step_kernel(r_ref, lw_ref, k_ref, v_ref, a_ref, b_ref, s_ref, y_ref, so_ref):
    ii = lax.broadcasted_iota(jnp.int32, (R_HEAD, R_HEAD), 0)
    jj = lax.broadcasted_iota(jnp.int32, (R_HEAD, R_HEAD), 1)
    eye = ii == jj
    for n in range(RWKV_STEP_ROWS):
        one = lambda ref: ref[n:n + 1, :]
        s = s_ref[n]
        sa = jnp.sum(s * one(a_ref), -1, keepdims=True)
        v_col = jnp.sum(jnp.where(eye, one(v_ref), 0.0), -1, keepdims=True)
        s_new = s * jnp.exp(one(lw_ref)) + sa * one(b_ref) + v_col * one(k_ref)
        y_col = jnp.sum(s_new * one(r_ref), -1, keepdims=True)
        so_ref[n] = s_new
        y_ref[n:n + 1, :] = jnp.sum(jnp.where(eye, y_col, 0.0), 0, keepdims=True)


def _rwkv_step(vecs, s0):
    n = s0.shape[0]
    vspec = pl.BlockSpec((RWKV_STEP_ROWS, R_HEAD), lambda i: (i, 0))
    sspec = pl.BlockSpec((RWKV_STEP_ROWS, R_HEAD, R_HEAD), lambda i: (i, 0, 0))
    return pl.pallas_call(
        _rwkv_step_kernel,
        grid=(n // RWKV_STEP_ROWS,),
        in_specs=[vspec] * 6 + [sspec],
        out_specs=[vspec, sspec],
        out_shape=[jax.ShapeDtypeStruct((n, R_HEAD), F32), jax.ShapeDtypeStruct((n, R_HEAD, R_HEAD), F32)],
        compiler_params=_cparams("parallel"),
        name="rwkv_step",
    )(*vecs, s0)


def _rwkv_post_kernel(y_ref, bonus_ref, gate_ref, lng_ref, lnb_ref, o_ref):
    o_ref[...] = _rwkv_post(y_ref[...], bonus_ref[...], gate_ref[...], lng_ref[...], lnb_ref[...])


def _rwkv_post_call(y, bonus, gate, lng, lnb):
    return pl.pallas_call(
        _rwkv_post_kernel,
        out_shape=jax.ShapeDtypeStruct(y.shape, F32),
        compiler_params=_cparams(),
        name="rwkv_post_step",
    )(y, bonus, gate, lng, lnb)


def _moba_kernel(qt_ref, k_ref, vt_ref, o_ref, km_scr, bias_scr, s_even, s_odd, p_odd):
    qi = pl.program_id(2)
    nblk = km_scr.shape[0]
    cols = A_GROUP * MOBA_BLOCK

    @pl.when(qi == 0)
    def _():
        km_scr[...] = jnp.mean(k_ref[0].reshape(nblk, MOBA_BLOCK, A_HEAD), axis=1)

    qt = jnp.concatenate([qt_ref[g * A_HEAD:(g + 1) * A_HEAD, :] for g in range(A_GROUP)], axis=1)
    gate_q = (qt * ATT_SCALE).astype(BF16)
    qb = (qt * (ATT_SCALE * LOG2_E)).astype(BF16)
    blk = lax.broadcasted_iota(jnp.int32, (nblk, cols), 0)
    gate = jnp.where(blk < qi, _bdot(km_scr[...], gate_q), NEG_INF)
    sel = jnp.zeros(gate.shape, jnp.bool_)
    for _ in range(MOBA_TOPK):
        m = jnp.max(gate, 0, keepdims=True)
        idx = jnp.min(jnp.where(gate == m, blk, nblk), 0, keepdims=True)
        pick = (blk == idx) & (m > NEG_INF)
        sel = sel | pick
        gate = jnp.where(pick, NEG_INF, gate)
    bias_scr[...] = jnp.where(sel, 0.0, NEG_INF)

    def scores(n):
        off = pl.multiple_of(n * MOBA_BLOCK, MOBA_BLOCK)
        return _bdot(k_ref[0, pl.ds(off, MOBA_BLOCK), :], qb)

    ones_rows = jnp.ones((MOBA_ROWS, MOBA_BLOCK), BF16)

    def values_t(n):
        off = pl.multiple_of(n * MOBA_BLOCK, MOBA_BLOCK)
        return jnp.concatenate([vt_ref[:, pl.ds(off, MOBA_BLOCK)].astype(BF16), ones_rows], axis=0)

    groups = MOBA_BLOCK // MOBA_ROWS

    def softmax_step(s, m, elem_bias=None, col_bias=None):
        def group(r):
            sr = s[r * MOBA_ROWS:(r + 1) * MOBA_ROWS, :]
            return sr if elem_bias is None else sr + elem_bias(r)

        peak = group(0)
        for r in range(1, groups):
            peak = jnp.maximum(peak, group(r))
        top = jnp.max(peak, 0, keepdims=True)
        if col_bias is not None:
            top = top + col_bias
        m_new = jnp.maximum(m, top)
        alpha = jnp.exp2(m - m_new)
        shift = m_new if col_bias is None else m_new - col_bias
        p = jnp.concatenate([jnp.exp2(group(r) - shift).astype(BF16) for r in range(groups)], axis=0)
        return m_new, alpha, p

    key = lax.broadcasted_iota(jnp.int32, (MOBA_ROWS, cols), 0)
    pos = lax.broadcasted_iota(jnp.int32, (MOBA_ROWS, cols), 1) % MOBA_BLOCK
    causal = lambda r: jnp.where(key + r * MOBA_ROWS <= pos, 0.0, NEG_INF)
    s_even[...] = scores(qi)
    m0, _, p = softmax_step(s_even, jnp.full((1, cols), NEG_INF, F32), elem_bias=causal)
    acc0 = _bdot(values_t(qi), p)
    s_even[...] = scores(0)
    p_odd[...] = jnp.zeros(p_odd.shape, BF16)

    def body(j, carry):
        m, acc = carry
        n0 = 2 * j
        pv_prev = _bdot(values_t(jnp.maximum(n0 - 1, 0)), p_odd[...])
        s_odd[...] = scores(n0 + 1)
        m, alpha, p_even = softmax_step(s_even, m, col_bias=bias_scr[pl.ds(n0, 1), :])
        acc = alpha * (acc + pv_prev) + _bdot(values_t(n0), p_even)
        s_even[...] = scores(jnp.minimum(n0 + 2, nblk - 1))
        m, alpha, p = softmax_step(s_odd, m, col_bias=bias_scr[pl.ds(n0 + 1, 1), :])
        p_odd[...] = p
        return m, alpha * acc

    trips = lax.shift_right_logical(qi + 1, 1)
    _, acc = lax.fori_loop(0, trips, body, (m0, acc0))
    acc = acc + _bdot(values_t(jnp.maximum(2 * trips - 1, 0)), p_odd[...])
    o = acc[:A_HEAD, :] / acc[A_HEAD:A_HEAD + 1, :]
    for g in range(A_GROUP):
        o_ref[g * A_HEAD:(g + 1) * A_HEAD, :] = o[:, g * MOBA_BLOCK:(g + 1) * MOBA_BLOCK]


def _moba_prompt(qt, kh, vt, nb, seq):
    nblk = seq // MOBA_BLOCK
    qspec = pl.BlockSpec((A_GROUP * A_HEAD, MOBA_BLOCK), lambda b, h, i: (h, b * nblk + i))
    return pl.pallas_call(
        _moba_kernel,
        grid=(nb, A_KV_HEADS, nblk),
        in_specs=[qspec,
                  pl.BlockSpec((1, seq, A_HEAD), lambda b, h, i: (h, b, 0)),
                  pl.BlockSpec((A_HEAD, seq), lambda b, h, i: (h, b))],
        out_specs=qspec,
        out_shape=jax.ShapeDtypeStruct(qt.shape, F32),
        scratch_shapes=[pltpu.VMEM((nblk, A_HEAD), F32), pltpu.VMEM((nblk, A_GROUP * MOBA_BLOCK), F32),
                        pltpu.VMEM((MOBA_BLOCK, A_GROUP * MOBA_BLOCK), F32),
                        pltpu.VMEM((MOBA_BLOCK, A_GROUP * MOBA_BLOCK), F32),
                        pltpu.VMEM((MOBA_BLOCK, A_GROUP * MOBA_BLOCK), BF16)],
        compiler_params=_cparams("parallel", "parallel", "arbitrary"),
        name="moba_prompt",
    )(qt, kh, vt)


def _pages_t(cache):
    return cache.transpose(0, 1, 3, 4, 2).reshape(-1, A_KVW, PAGE_SIZE)


def _kmeans_kernel(n_in, pt_ref, *refs):
    pages, o_ref = refs[:n_in], refs[n_in]
    j = pl.program_id(1)
    per_blk = MOBA_BLOCK // PAGE_SIZE
    blocks = n_in // per_blk

    @pl.when(j == 0)
    def _():
        o_ref[...] = jnp.zeros(o_ref.shape, F32)

    out = o_ref[0]
    lane = lax.broadcasted_iota(jnp.int32, out.shape, 1)
    for i in range(blocks):
        tot = sum(pages[per_blk * i + p][0] for p in range(per_blk))
        mean = jnp.sum(tot, axis=1, keepdims=True) * (1.0 / MOBA_BLOCK)
        out = jnp.where(lane == j * blocks + i, mean, out)
    o_ref[0] = out


def _decode_kmeans(pages, page_table, layer, n_pool):
    nb, n_pages = page_table.shape
    n_in = min(16, n_pages)
    steps = n_pages // n_in
    per_blk = MOBA_BLOCK // PAGE_SIZE

    def page_spec(i):
        return pl.BlockSpec((1, A_KVW, PAGE_SIZE),
                            lambda b, j, pt: (layer * n_pool + pt[b * n_pages + j * n_in + i], 0, 0))

    return pl.pallas_call(
        functools.partial(_kmeans_kernel, n_in),
        grid_spec=pltpu.PrefetchScalarGridSpec(
            num_scalar_prefetch=1,
            grid=(nb, steps),
            in_specs=[page_spec(i) for i in range(n_in)],
            out_specs=pl.BlockSpec((1, A_KVW, n_pages // per_blk), lambda b, j, pt: (b, 0, 0)),
        ),
        out_shape=jax.ShapeDtypeStruct((nb, A_KVW, n_pages // per_blk), F32),
        compiler_params=_cparams("parallel", "arbitrary"),
        name="decode_kmeans",
    )(page_table.reshape(-1), *([pages] * n_in))


def _decode_select_kernel(q_ref, km_ref, o_ref):
    nblk = km_ref.shape[2]
    gate = _bdot(q_ref[0], km_ref[0])
    col = lax.broadcasted_iota(jnp.int32, gate.shape, 1)
    lane = lax.broadcasted_iota(jnp.int32, (A_HEADS, LANES), 1)
    out = jnp.zeros((A_HEADS, LANES), jnp.int32)
    for r in range(MOBA_TOPK):
        m = jnp.max(gate, -1, keepdims=True)
        idx = jnp.min(jnp.where(gate == m, col, nblk), -1, keepdims=True)
        out = jnp.where(lane == r, idx, out)
        gate = jnp.where(col == idx, NEG_INF, gate)
    o_ref[0] = out


def _decode_select(q_exp, kmeans):
    nb, _, nblk = kmeans.shape
    return pl.pallas_call(
        _decode_select_kernel,
        grid=(nb,),
        in_specs=[pl.BlockSpec((1, A_HEADS, A_KVW), lambda b: (b, 0, 0)),
                  pl.BlockSpec((1, A_KVW, nblk), lambda b: (b, 0, 0))],
        out_specs=pl.BlockSpec((1, A_HEADS, LANES), lambda b: (b, 0, 0)),
        out_shape=jax.ShapeDtypeStruct((nb, A_HEADS, LANES), jnp.int32),
        compiler_params=_cparams("parallel"),
        name="decode_select",
    )(q_exp, kmeans)


DEC_PAGES = MOBA_TOPK * (MOBA_BLOCK // PAGE_SIZE)


def _decode_attn_kernel(pg_ref, q_ref, kn_ref, vn_ref, *refs):
    k_pages, v_pages, o_ref = refs[:DEC_PAGES], refs[DEC_PAGES:2 * DEC_PAGES], refs[2 * DEC_PAGES]
    h = pl.program_id(1)
    q = q_ref[0, pl.ds(h, 1), :].astype(BF16)
    rb = lambda x: x.astype(BF16).astype(F32)
    s_self = jnp.sum(rb(q) * rb(kn_ref[0]), -1, keepdims=True) * ATT_SCALE
    scores = [_bdot(q, kp[0]) * ATT_SCALE for kp in k_pages]
    m = s_self
    for s in scores:
        m = jnp.maximum(m, jnp.max(s, -1, keepdims=True))
    p_self = jnp.exp(s_self - m)
    l = p_self
    acc = rb(p_self) * rb(vn_ref[0])
    for s, vp in zip(scores, v_pages):
        p = jnp.exp(s - m)
        l = l + jnp.sum(p, -1, keepdims=True)
        acc = acc + _bdot(p, vp[0], _NT)
    o_ref[0, pl.ds(h, 1), :] = acc / l


def _decode_attn(q_exp, k_new, v_new, kp, vp, page_ids, layer, n_pool):
    nb = q_exp.shape[0]

    def page_spec(i):
        return pl.BlockSpec(
            (1, A_KVW, PAGE_SIZE),
            lambda b, h, pg: (layer * n_pool + pg[(b * A_HEADS + h) * DEC_PAGES + i], 0, 0))

    bspec = pl.BlockSpec((1, A_HEADS, A_KVW), lambda b, h, pg: (b, 0, 0))
    nspec = pl.BlockSpec((1, 1, A_KVW), lambda b, h, pg: (b, 0, 0))
    return pl.pallas_call(
        _decode_attn_kernel,
        grid_spec=pltpu.PrefetchScalarGridSpec(
            num_scalar_prefetch=1,
            grid=(nb, A_HEADS),
            in_specs=[bspec, nspec, nspec] + [page_spec(i) for i in range(DEC_PAGES)] * 2,
            out_specs=bspec,
        ),
        out_shape=jax.ShapeDtypeStruct((nb, A_HEADS, A_KVW), F32),
        compiler_params=_cparams("parallel", "arbitrary"),
        name="decode_attn",
    )(page_ids.reshape(-1), q_exp, k_new.reshape(nb, 1, A_KVW), v_new.reshape(nb, 1, A_KVW),
      *([kp] * DEC_PAGES), *([vp] * DEC_PAGES))


def _moba_decode(q, k_new, v_new, k_pages, v_pages, n_pool, page_table, layer):
    nb = q.shape[0]
    onehot = (jnp.arange(A_KV_HEADS)[None, :] == (jnp.arange(A_HEADS) // A_GROUP)[:, None]).astype(F32)
    q_exp = (q.reshape(nb, A_HEADS, 1, A_HEAD) * onehot[None, :, :, None]).reshape(nb, A_HEADS, A_KVW)
    kmeans = _decode_kmeans(k_pages, page_table, layer, n_pool)
    blk = _decode_select(q_exp, kmeans)[:, :, :MOBA_TOPK]
    per_blk = MOBA_BLOCK // PAGE_SIZE
    pages_of = (blk[..., None] * per_blk + jnp.arange(per_blk)).reshape(nb, A_HEADS * DEC_PAGES)
    page_ids = jnp.take_along_axis(page_table, pages_of, axis=1)
    o = _decode_attn(q_exp, k_new, v_new, k_pages, v_pages, page_ids, layer, n_pool)
    o = o.reshape(nb, A_HEADS, A_KV_HEADS, A_HEAD)
    return jnp.einsum("bhkd,hk->bhd", o, onehot).reshape(nb, A_QW)


def _merge_kernel(ya_transposed, x_ref, ys_ref, yr_ref, ya_ref, wg_ref, ps_ref, pr_ref, pa_ref, wo_ref,
                  g_ref, b_ref, o_ref):
    x = x_ref[...]
    xb = x.astype(BF16)
    ya = ya_ref[...].T if ya_transposed else ya_ref[...]
    merged = None
    for i, (y, p_ref) in enumerate(((ys_ref[...], ps_ref), (yr_ref[...], pr_ref), (ya, pa_ref))):
        gate = jax.nn.sigmoid(_bdot(xb, wg_ref[:, i * D_MODEL:(i + 1) * D_MODEL]))
        term = gate * _bdot(y, p_ref[...])
        merged = term if merged is None else merged + term
    o_ref[...] = _ln(ALPHA * x + _bdot(merged, wo_ref[...]), g_ref[...], b_ref[...])


def _merge(x, ys, yr, ya, wg, ps, pr, pa, wo, g, b, tm, ya_transposed):
    m = x.shape[0]
    row = lambda width: pl.BlockSpec((tm, width), lambda i: (i, 0))
    ya_spec = pl.BlockSpec((A_QW, tm), lambda i: (0, i)) if ya_transposed else row(A_QW)
    return pl.pallas_call(
        functools.partial(_merge_kernel, ya_transposed),
        grid=(m // tm,),
        in_specs=[row(D_MODEL), row(S_WIDTH), row(R_WIDTH), ya_spec,
                  _resident(wg.shape), _resident(ps.shape), _resident(pr.shape), _resident(pa.shape),
                  _resident(wo.shape),
                  _full((1, D_MODEL)), _full((1, D_MODEL))],
        out_specs=row(D_MODEL),
        out_shape=jax.ShapeDtypeStruct((m, D_MODEL), F32),
        compiler_params=_cparams("parallel"),
        name="merge",
    )(x, ys, yr, ya, wg, ps, pr, pa, wo, g, b)


FFN_COLS = D_FF // 2


def _ffn_kernel(tiles_per_seq, x_ref, halo_ref, wup_ref, cw_ref, cb_ref, wdn_ref, g_ref, b_ref, o_ref):
    i = pl.program_id(0)
    x = x_ref[...]
    tm = x.shape[0]
    xe = jnp.concatenate([halo_ref[...], x], axis=0).astype(BF16)
    row = lax.broadcasted_iota(jnp.int32, (tm + SUBLANES, 1), 0)
    keep = jnp.logical_or(row >= SUBLANES, i % tiles_per_seq != 0)
    acc = jnp.zeros((tm, D_MODEL), F32)
    for c in range(D_FF // FFN_COLS):
        halves = []
        for off in (c * FFN_COLS, D_FF + c * FFN_COLS):
            up = jnp.where(keep, _bdot(xe, wup_ref[:, off:off + FFN_COLS]), 0.0)
            cv = cb_ref[:, off:off + FFN_COLS]
            for j in range(CONV_W):
                lo = SUBLANES - (CONV_W - 1) + j
                cv = cv + cw_ref[j:j + 1, off:off + FFN_COLS] * up[lo:lo + tm, :]
            halves.append(cv)
        hmid = jax.nn.gelu(halves[0]) * halves[1]
        acc = acc + _bdot(hmid, wdn_ref[c * FFN_COLS:(c + 1) * FFN_COLS, :])
    o_ref[...] = _ln(ALPHA * x + acc, g_ref[...], b_ref[...])


def _ffn(x, wup, cw, cb, wdn, g, b, seq, tm):
    m = x.shape[0]
    per8 = tm // SUBLANES
    return pl.pallas_call(
        functools.partial(_ffn_kernel, seq // tm),
        grid=(m // tm,),
        in_specs=[pl.BlockSpec((tm, D_MODEL), lambda i: (i, 0)),
                  pl.BlockSpec((SUBLANES, D_MODEL), lambda i: (jnp.maximum(i * per8 - 1, 0), 0)),
                  _resident(wup.shape), _full(cw.shape), _full(cb.shape), _resident(wdn.shape),
                  _full((1, D_MODEL)), _full((1, D_MODEL))],
        out_specs=pl.BlockSpec((tm, D_MODEL), lambda i: (i, 0)),
        out_shape=jax.ShapeDtypeStruct((m, D_MODEL), F32),
        compiler_params=_cparams("parallel"),
        name="ffn",
    )(x, x, wup, cw, cb, wdn, g, b)


def _ffn_step_kernel(x_ref, c0a_ref, c0b_ref, wup_ref, cw_ref, cb_ref, wdn_ref, g_ref, b_ref, o_ref, up_ref):
    x = x_ref[...]
    up = _bdot(x, wup_ref[...])
    up_ref[...] = up
    cv = cb_ref[...] + cw_ref[0:1, :] * c0a_ref[...] + cw_ref[1:2, :] * c0b_ref[...] + cw_ref[2:3, :] * up
    hmid = jax.nn.gelu(cv[:, :D_FF]) * cv[:, D_FF:]
    o_ref[...] = _ln(ALPHA * x + _bdot(hmid, wdn_ref[...]), g_ref[...], b_ref[...])


def _ffn_step(x, c0a, c0b, wup, cw, cb, wdn, g, b):
    n = x.shape[0]
    return pl.pallas_call(
        _ffn_step_kernel,
        out_shape=[jax.ShapeDtypeStruct((n, D_MODEL), F32), jax.ShapeDtypeStruct((n, 2 * D_FF), F32)],
        compiler_params=_cparams(),
        name="ffn_step",
    )(x, c0a, c0b, wup, cw, cb, wdn, g, b)


def _up_rows_kernel(x_ref, w_ref, o_ref):
    o_ref[...] = _bdot(x_ref[...], w_ref[...])


def _up_rows(x, w):
    return pl.pallas_call(
        _up_rows_kernel,
        out_shape=jax.ShapeDtypeStruct((x.shape[0], w.shape[1]), F32),
        compiler_params=_cparams(),
        name="ffn_up_tail",
    )(x, w)


def _layer_params(l, P):
    bf = lambda x: x.astype(BF16)
    row = lambda x: x.reshape(1, -1)
    w_in = P["w_in"][l]
    return dict(
        w_gate=bf(w_in[:, :O_SSM]), w_rest=bf(w_in[:, O_SSM:]),
        s5=_s5_params(P["ssm_a_re"][l], P["ssm_a_im"][l], P["ssm_log_dt"][l], P["ssm_b_re"][l],
                      P["ssm_b_im"][l], P["ssm_c_re"][l], P["ssm_c_im"][l]),
        ssm_d=row(P["ssm_d"][l]), w_glu=bf(P["ssm_w_glu"][l]), b_glu=row(P["ssm_b_glu"][l]),
        rwkv=_rwkv_weights(P["rwkv_mu"][l], P["rwkv_w0"][l], P["rwkv_w2"][l], P["rwkv_a0"][l],
                           P["rwkv_a2"][l], P["rwkv_g2"][l], P["rwkv_k_k"][l], P["rwkv_k_a"][l],
                           P["rwkv_r_k"][l]),
        lnx_g=row(P["rwkv_lnx_g"][l]), lnx_b=row(P["rwkv_lnx_b"][l]),
        proj_ssm=bf(P["proj_ssm"][l]), proj_rwkv=bf(P["proj_rwkv"][l]), proj_attn=bf(P["proj_attn"][l]),
        w_o=bf(P["w_o"][l]), ln1_g=row(P["ln1_g"][l]), ln1_b=row(P["ln1_b"][l]),
        w_up=bf(P["ffn_w_up"][l]), conv_w=P["ffn_conv_w"][l], conv_b=row(P["ffn_conv_b"][l]),
        w_down=bf(P["ffn_w_down"][l]), ln2_g=row(P["ln2_g"][l]), ln2_b=row(P["ln2_b"][l]),
    )


def _pick_tile(n, pref):
    t = min(pref, n)
    while n % t:
        t //= 2
    return t


def _prompt_trunk(x_prompt, P, layers):
    nb, seq, _ = x_prompt.shape
    m = nb * seq
    x = x_prompt.reshape(m, D_MODEL)
    tm = _pick_tile(seq, 512)
    rope = _rope_tables(jnp.arange(seq, dtype=jnp.int32))
    ln_g, ln_b = P["ln_in_g"].reshape(1, -1), P["ln_in_b"].reshape(1, -1)
    outs = []
    for l, lp in enumerate(layers):
        res = _in_proj(x, ln_g, ln_b, lp["w_rest"], rope, tm, seq // tm, pre_ln=(l == 0), attn_layouts=True)
        u, c, k, v, qt, kh, vt = res[:7]
        if l == 0:
            x = res[7]
        y_s, s_fin = _s5_prompt(u, lp["s5"], lp["ssm_d"], lp["w_glu"], lp["b_glu"],
                                jnp.zeros((nb, 2, S_LANES), F32), nb, seq, _pick_tile(seq, 256))
        r, lw, k2, vv, a, b, gate, bonus = _rwkv_pre(c, jnp.zeros((nb, R_IN), F32), lp["rwkv"], nb, seq,
                                                     _pick_tile(seq, 256), decode=False)
        y1, y2, g, f = _rwkv_chunks(r, lw, k2, vv, a, b, RWKV_CHUNK)
        y_r, h_fin = _rwkv_scan(y1, y2, g, f, bonus, gate, lp["lnx_g"], lp["lnx_b"],
                                jnp.zeros((nb, R_HEADS, R_HEAD, R_HEAD), F32), nb, seq, RWKV_CHUNK)
        y_at = _moba_prompt(qt, kh, vt, nb, seq)
        x = _merge(x, y_s, y_r, y_at, lp["w_gate"], lp["proj_ssm"], lp["proj_rwkv"], lp["proj_attn"],
                   lp["w_o"], lp["ln1_g"], lp["ln1_b"], _pick_tile(seq, 512), ya_transposed=True)
        tail = x.reshape(nb, seq, D_MODEL)[:, seq - (CONV_W - 1):].reshape(nb * (CONV_W - 1), D_MODEL)
        conv = _up_rows(tail, lp["w_up"]).reshape(nb, CONV_W - 1, 2 * D_FF)
        x = _ffn(x, lp["w_up"], lp["conv_w"], lp["conv_b"], lp["w_down"], lp["ln2_g"], lp["ln2_b"],
                 seq, _pick_tile(seq, 512))
        outs.append(dict(
            k=k.reshape(nb, seq, A_KV_HEADS, A_HEAD), v=v.reshape(nb, seq, A_KV_HEADS, A_HEAD),
            s_re=s_fin[:, 0].reshape(nb, S_GROUPS, S_STATE), s_im=s_fin[:, 1].reshape(nb, S_GROUPS, S_STATE),
            rwkv=jnp.swapaxes(h_fin, -1, -2), shift=c.reshape(nb, seq, R_IN)[:, -1], conv=conv))
    return x.reshape(nb, seq, D_MODEL), outs


def _sample_trunk(x_sample, P, layers, cache_k, cache_v, page_table, st_re, st_im, st_rwkv, st_shift, st_conv):
    nb = x_sample.shape[0]
    x = x_sample.reshape(nb, D_MODEL)
    past_len = page_table.shape[1] * PAGE_SIZE
    rope = _rope_tables(jnp.full((nb,), past_len, jnp.int32))
    ln_g, ln_b = P["ln_in_g"].reshape(1, -1), P["ln_in_b"].reshape(1, -1)
    k_pages, v_pages = _pages_t(cache_k), _pages_t(cache_v)
    outs = []
    for l, lp in enumerate(layers):
        res = _in_proj(x, ln_g, ln_b, lp["w_rest"], rope, nb, 1, pre_ln=(l == 0), attn_layouts=False)
        u, c, k, v, q = res[:5]
        if l == 0:
            x = res[5]
        s0 = jnp.concatenate([st_re[l].reshape(nb, S_LANES), st_im[l].reshape(nb, S_LANES)], axis=1)
        y_s, s_new = _s5_step(u, lp["s5"], lp["ssm_d"], lp["w_glu"], lp["b_glu"], s0)
        r, lw, k2, vv, a, b, gate, bonus = _rwkv_pre(c, st_shift[l], lp["rwkv"], nb, 1, nb, decode=True)
        per_head = lambda t: t.reshape(nb * R_HEADS, R_HEAD)
        y_rows, s_rwkv = _rwkv_step([per_head(t) for t in (r, lw, k2, vv, a, b)],
                                    st_rwkv[l].reshape(nb * R_HEADS, R_HEAD, R_HEAD))
        y_r = _rwkv_post_call(y_rows.reshape(nb, R_WIDTH), bonus, gate, lp["lnx_g"], lp["lnx_b"])
        y_a = _moba_decode(q, k, v, k_pages, v_pages, cache_k.shape[1], page_table, l)
        x = _merge(x, y_s, y_r, y_a, lp["w_gate"], lp["proj_ssm"], lp["proj_rwkv"], lp["proj_attn"],
                   lp["w_o"], lp["ln1_g"], lp["ln1_b"], nb, ya_transposed=False)
        x, up = _ffn_step(x, st_conv[l][:, 0], st_conv[l][:, 1], lp["w_up"], lp["conv_w"], lp["conv_b"],
                          lp["w_down"], lp["ln2_g"], lp["ln2_b"])
        outs.append(dict(
            k=k.reshape(nb, 1, A_KV_HEADS, A_HEAD), v=v.reshape(nb, 1, A_KV_HEADS, A_HEAD),
            s_re=s_new[:, :S_LANES].reshape(nb, S_GROUPS, S_STATE),
            s_im=s_new[:, S_LANES:].reshape(nb, S_GROUPS, S_STATE),
            rwkv=s_rwkv.reshape(nb, R_HEADS, R_HEAD, R_HEAD), shift=c,
            conv=jnp.stack([st_conv[l][:, 1], up], axis=1)))
    return x.reshape(nb, 1, D_MODEL), outs


def kernel(x_prompt, x_sample, cache_k, cache_v, page_table, state_ssm_re, state_ssm_im, state_rwkv, state_rwkv_shift, state_conv, ln_in_g, ln_in_b, w_in, ssm_a_re, ssm_a_im, ssm_log_dt, ssm_b_re, ssm_b_im, ssm_c_re, ssm_c_im, ssm_d, ssm_w_glu, ssm_b_glu, rwkv_mu, rwkv_w0, rwkv_w2, rwkv_a0, rwkv_a2, rwkv_g2, rwkv_k_k, rwkv_k_a, rwkv_r_k, rwkv_lnx_g, rwkv_lnx_b, proj_ssm, proj_rwkv, proj_attn, w_o, ln1_g, ln1_b, ffn_w_up, ffn_conv_w, ffn_conv_b, ffn_w_down, ln2_g, ln2_b):
    P = dict(ln_in_g=ln_in_g, ln_in_b=ln_in_b, w_in=w_in,
             ssm_a_re=ssm_a_re, ssm_a_im=ssm_a_im, ssm_log_dt=ssm_log_dt,
             ssm_b_re=ssm_b_re, ssm_b_im=ssm_b_im, ssm_c_re=ssm_c_re, ssm_c_im=ssm_c_im,
             ssm_d=ssm_d, ssm_w_glu=ssm_w_glu, ssm_b_glu=ssm_b_glu,
             rwkv_mu=rwkv_mu, rwkv_w0=rwkv_w0, rwkv_w2=rwkv_w2, rwkv_a0=rwkv_a0, rwkv_a2=rwkv_a2,
             rwkv_g2=rwkv_g2, rwkv_k_k=rwkv_k_k, rwkv_k_a=rwkv_k_a, rwkv_r_k=rwkv_r_k,
             rwkv_lnx_g=rwkv_lnx_g, rwkv_lnx_b=rwkv_lnx_b,
             proj_ssm=proj_ssm, proj_rwkv=proj_rwkv, proj_attn=proj_attn, w_o=w_o,
             ln1_g=ln1_g, ln1_b=ln1_b, ffn_w_up=ffn_w_up, ffn_conv_w=ffn_conv_w,
             ffn_conv_b=ffn_conv_b, ffn_w_down=ffn_w_down, ln2_g=ln2_g, ln2_b=ln2_b)
    layers = [_layer_params(l, P) for l in range(w_in.shape[0])]
    y_p, op = _prompt_trunk(x_prompt, P, layers)
    y_s, os_ = _sample_trunk(x_sample, P, layers, cache_k, cache_v, page_table, state_ssm_re, state_ssm_im,
                             state_rwkv, state_rwkv_shift, state_conv)
    st = lambda outs, key: jnp.stack([o[key] for o in outs])
    return (y_p, y_s, st(op, "k"), st(op, "v"), st(os_, "k"), st(os_, "v"),
            st(op, "s_re"), st(op, "s_im"), st(os_, "s_re"), st(os_, "s_im"),
            st(op, "rwkv"), st(os_, "rwkv"), st(op, "shift"), st(os_, "shift"),
            st(op, "conv"), st(os_, "conv"))
```

```python
import functools
import math

import jax
import jax.numpy as jnp
from jax import lax
from jax.experimental import pallas as pl
from jax.experimental.pallas import tpu as pltpu

F32 = jnp.float32
BF16 = jnp.bfloat16

D_MODEL = 1024
PAGE_SIZE = 128
S_GROUP = 16
S_GROUPS = 16
S_STATE = 64
S_WIDTH = S_GROUPS * S_GROUP
S_LANES = S_GROUPS * S_STATE
R_HEAD = 64
R_HEADS = 4
R_WIDTH = R_HEADS * R_HEAD
W_LORA = 32
A_LORA = 32
G_LORA = 64
N_LORA = W_LORA + A_LORA + G_LORA
R_IN = 3 * R_WIDTH + N_LORA
GN_EPS = 64e-5
A_HEAD = 64
A_HEADS = 8
A_KV_HEADS = 4
A_GROUP = A_HEADS // A_KV_HEADS
A_QW = A_HEADS * A_HEAD
A_KVW = A_KV_HEADS * A_HEAD
ROT_DIM = A_HEAD // 4
ROPE_THETA = 500000.0
MOBA_BLOCK = 256
MOBA_TOPK = 3
N_BRANCH = 3
O_SSM = N_BRANCH * D_MODEL
N_REST = S_WIDTH + R_IN + A_QW + 2 * A_KVW
D_FF = 2816
CONV_W = 3
DEPTH = 2
ALPHA = (2 * DEPTH) ** 0.25
LN_EPS = 1e-5
ATT_SCALE = A_HEAD ** -0.5
LOG2_E = math.log2(math.e)

VMEM_LIMIT_BYTES = 56 * 1024 * 1024
LANES = 128
SUBLANES = 8

MOBA_ROWS = 16
RWKV_CHUNK = 64
NEG_INF = float("-inf")
MAX_FLOOR = -1e30

_NN = (((1,), (0,)), ((), ()))
_NT = (((1,), (1,)), ((), ()))
_TN = (((0,), (0,)), ((), ()))


def _cparams(*sem):
    return pltpu.CompilerParams(dimension_semantics=sem or None, vmem_limit_bytes=VMEM_LIMIT_BYTES)


def _bdot(a, b, dims=_NN):
    return lax.dot_general(a.astype(BF16), b.astype(BF16), dims, preferred_element_type=F32)


def _dot3(a, b, dims=_NN):
    ah = a.astype(BF16)
    al = (a - ah.astype(F32)).astype(BF16)
    bh = b.astype(BF16)
    bl = (b - bh.astype(F32)).astype(BF16)
    f = lambda x, y: lax.dot_general(x, y, dims, preferred_element_type=F32)
    return f(ah, bh) + f(ah, bl) + f(al, bh)


def _ln(x, g, b):
    mu = jnp.mean(x, -1, keepdims=True)
    xc = x - mu
    var = jnp.mean(xc * xc, -1, keepdims=True)
    return xc * lax.rsqrt(var + LN_EPS) * g + b


def _head_sum(x, head):
    lane_head = lax.broadcasted_iota(jnp.int32, (1, x.shape[1]), 1) // head
    out = jnp.zeros_like(x)
    for h in range(x.shape[1] // head):
        m = lane_head == h
        s = jnp.sum(jnp.where(m, x, 0.0), -1, keepdims=True)
        out = jnp.where(m, s, out)
    return out


def _full(shape):
    n = len(shape)
    return pl.BlockSpec(shape, lambda *_: (0,) * n)


def _resident(shape):
    n = len(shape)
    return pl.BlockSpec(shape, lambda *_: (0,) * n, pipeline_mode=pl.Buffered(1))


def _in_proj_kernel(pre_ln, attn_layouts, x_ref, g_ref, b_ref, w_ref, cos_ref, sa_ref, sb_ref,
                    u_ref, c_ref, k_ref, v_ref, *rest):
    x = x_ref[...]
    if pre_ln:
        x = _ln(x, g_ref[...], b_ref[...])
        rest[-1][...] = x
    h = _bdot(x, w_ref[...])
    u_ref[...] = h[:, :S_WIDTH]
    c_ref[...] = h[:, S_WIDTH:S_WIDTH + R_IN]
    cos, sa, sb = cos_ref[...], sa_ref[...], sb_ref[...]
    base = S_WIDTH + R_IN
    nq = A_QW // LANES
    per = LANES // A_HEAD
    for j in range((A_QW + A_KVW) // LANES):
        ch = h[:, base + LANES * j: base + LANES * (j + 1)]
        rot = (ch * cos + pltpu.roll(ch, LANES - ROT_DIM // 2, 1) * sa
               + pltpu.roll(ch, ROT_DIM // 2, 1) * sb)
        if j < nq:
            if attn_layouts:
                rest[0][LANES * j:LANES * (j + 1), :] = rot.T
            else:
                rest[0][:, LANES * j:LANES * (j + 1)] = rot
        else:
            k_ref[:, LANES * (j - nq):LANES * (j - nq + 1)] = rot
            if attn_layouts:
                for i in range(per):
                    rest[1][per * (j - nq) + i] = rot[:, A_HEAD * i:A_HEAD * (i + 1)]
    v = h[:, base + A_QW + A_KVW:]
    v_ref[...] = v
    if attn_layouts:
        for j in range(A_KVW // LANES):
            rest[2][LANES * j:LANES * (j + 1), :] = v[:, LANES * j:LANES * (j + 1)].T


def _in_proj(x, ln_g, ln_b, w, rope, tm, pos_tiles, pre_ln, attn_layouts):
    m = x.shape[0]
    row = lambda width: pl.BlockSpec((tm, width), lambda i: (i, 0))
    col = lambda width: pl.BlockSpec((width, tm), lambda i: (0, i))
    tab = pl.BlockSpec((tm, LANES), lambda i: (i % pos_tiles, 0))
    f32 = lambda *shape: jax.ShapeDtypeStruct(shape, F32)
    specs = [row(S_WIDTH), row(R_IN), row(A_KVW), row(A_KVW)]
    shapes = [f32(m, S_WIDTH), f32(m, R_IN), f32(m, A_KVW), f32(m, A_KVW)]
    if attn_layouts:
        specs += [col(A_QW), pl.BlockSpec((A_KV_HEADS, tm, A_HEAD), lambda i: (0, i, 0)), col(A_KVW)]
        shapes += [f32(A_QW, m), f32(A_KV_HEADS, m, A_HEAD), f32(A_KVW, m)]
    else:
        specs += [row(A_QW)]
        shapes += [f32(m, A_QW)]
    if pre_ln:
        specs += [row(D_MODEL)]
        shapes += [f32(m, D_MODEL)]
    return pl.pallas_call(
        functools.partial(_in_proj_kernel, pre_ln, attn_layouts),
        grid=(m // tm,),
        in_specs=[row(D_MODEL), _full((1, D_MODEL)), _full((1, D_MODEL)), _resident((D_MODEL, N_REST)),
                  tab, tab, tab],
        out_specs=specs,
        out_shape=shapes,
        compiler_params=_cparams("parallel"),
        name="in_proj",
    )(x, ln_g, ln_b, w, *rope)


def _rope_tables(pos):
    half = ROT_DIM // 2
    inv = ROPE_THETA ** (-jnp.arange(half, dtype=F32) / half)
    ang = pos.astype(F32)[:, None] * inv[None, :]
    cos, sin = jnp.cos(ang), jnp.sin(ang)
    n = pos.shape[0]
    pad = jnp.zeros((n, A_HEAD - ROT_DIM), F32)
    zero = jnp.zeros((n, half), F32)
    cos_h = jnp.concatenate([cos, cos, pad + 1.0], -1)
    sa_h = jnp.concatenate([-sin, zero, pad], -1)
    sb_h = jnp.concatenate([zero, sin, pad], -1)
    rep = LANES // A_HEAD
    return tuple(jnp.tile(t, (1, rep)) for t in (cos_h, sa_h, sb_h))


def _s5_params(a_re, a_im, log_dt, b_re, b_im, c_re, c_im):
    dt = jnp.exp(log_dt)[:, None]
    lam_re, lam_im = a_re * dt, a_im * dt

    def power(k):
        mag = jnp.exp(lam_re * k)
        return (mag * jnp.cos(lam_im * k)).reshape(-1), (mag * jnp.sin(lam_im * k)).reshape(-1)

    abar_re, abar_im = power(1.0)
    den = (a_re * a_re + a_im * a_im).reshape(-1)
    ar, ai = a_re.reshape(-1), a_im.reshape(-1)
    em_re = abar_re - 1.0
    coef = jnp.stack([(em_re * ar + abar_im * ai) / den, (abar_im * ar - em_re * ai) / den])
    pw = jnp.stack([jnp.stack(x) for x in zip(*[power(float(k)) for k in range(1, SUBLANES + 1)])])
    dbl = jnp.stack([jnp.stack(x) for x in zip(*[power(float(k)) for k in (1, 2, 4)])])
    eye = jnp.eye(S_GROUPS, dtype=F32)
    wb = jnp.concatenate([jnp.einsum("gpc,gh->gchp", b, eye).reshape(S_WIDTH, S_LANES)
                          for b in (b_re, b_im)], axis=1)
    wc = jnp.concatenate([jnp.einsum("gcp,gh->gphc", c, eye).reshape(S_LANES, S_WIDTH)
                          for c in (c_re, -c_im)], axis=0)
    return dict(coef=coef, pw=pw, dbl=dbl, abar=jnp.stack([abar_re, abar_im]),
                wb=wb.astype(BF16), wc=wc.astype(BF16))


def _s5_glu(s_re, s_im, u, wc_ref, d_ref, wg_ref, bg_ref):
    s = jnp.concatenate([s_re, s_im], axis=1)
    y = _bdot(s, wc_ref[...]) + d_ref[...] * u
    z = jax.nn.gelu(y)
    return z * jax.nn.sigmoid(_bdot(z, wg_ref[...]) + bg_ref[...])


def _s5_kernel(u_ref, wb_ref, wc_ref, coef_ref, pw_ref, dbl_ref, d_ref, wg_ref, bg_ref, s0_ref,
               y_ref, sfin_ref, s_scr, carry_scr):
    j = pl.program_id(1)
    t = u_ref.shape[0]

    @pl.when(j == 0)
    def _():
        carry_scr[...] = s0_ref[0]

    u = u_ref[...]
    bu = _bdot(u, wb_ref[...])
    b_re, b_im = bu[:, :S_LANES], bu[:, S_LANES:]
    c_re, c_im = coef_ref[0:1, :], coef_ref[1:2, :]
    s_scr[0] = c_re * b_re - c_im * b_im
    s_scr[1] = c_re * b_im + c_im * b_re
    row = lax.broadcasted_iota(jnp.int32, (SUBLANES, S_LANES), 0)

    def body(g, carry):
        cr, ci = carry
        off = pl.multiple_of(g * SUBLANES, SUBLANES)
        sr = s_scr[0, pl.ds(off, SUBLANES), :]
        si = s_scr[1, pl.ds(off, SUBLANES), :]
        for n, d in enumerate((1, 2, 4)):
            ar, ai = dbl_ref[0, n:n + 1, :], dbl_ref[1, n:n + 1, :]
            pr = jnp.where(row >= d, pltpu.roll(sr, d, 0), 0.0)
            pi = jnp.where(row >= d, pltpu.roll(si, d, 0), 0.0)
            sr, si = sr + ar * pr - ai * pi, si + ar * pi + ai * pr
        p_re, p_im = pw_ref[0], pw_ref[1]
        sr, si = sr + p_re * cr - p_im * ci, si + p_re * ci + p_im * cr
        s_scr[0, pl.ds(off, SUBLANES), :] = sr
        s_scr[1, pl.ds(off, SUBLANES), :] = si
        return sr[SUBLANES - 1:SUBLANES, :], si[SUBLANES - 1:SUBLANES, :]

    cr, ci = lax.fori_loop(0, t // SUBLANES, body, (carry_scr[0:1, :], carry_scr[1:2, :]), unroll=2)
    carry_scr[0:1, :] = cr
    carry_scr[1:2, :] = ci
    y_ref[...] = _s5_glu(s_scr[0], s_scr[1], u, wc_ref, d_ref, wg_ref, bg_ref)

    @pl.when(j == pl.num_programs(1) - 1)
    def _():
        sfin_ref[0] = carry_scr[...]


def _s5_prompt(u, sp, d, w_glu, b_glu, s0, nb, seq, t):
    nt = seq // t
    return pl.pallas_call(
        _s5_kernel,
        grid=(nb, nt),
        in_specs=[pl.BlockSpec((t, S_WIDTH), lambda b, j: (b * nt + j, 0)),
                  _full((S_WIDTH, 2 * S_LANES)), _full((2 * S_LANES, S_WIDTH)), _full((2, S_LANES)),
                  _full((2, SUBLANES, S_LANES)), _full((2, 3, S_LANES)), _full((1, S_WIDTH)),
                  _full((S_WIDTH, S_WIDTH)), _full((1, S_WIDTH)),
                  pl.BlockSpec((1, 2, S_LANES), lambda b, j: (b, 0, 0))],
        out_specs=[pl.BlockSpec((t, S_WIDTH), lambda b, j: (b * nt + j, 0)),
                   pl.BlockSpec((1, 2, S_LANES), lambda b, j: (b, 0, 0))],
        out_shape=[jax.ShapeDtypeStruct((nb * seq, S_WIDTH), F32),
                   jax.ShapeDtypeStruct((nb, 2, S_LANES), F32)],
        scratch_shapes=[pltpu.VMEM((2, t, S_LANES), F32), pltpu.VMEM((2, S_LANES), F32)],
        compiler_params=_cparams("parallel", "arbitrary"),
        name="s5_scan",
    )(u, sp["wb"], sp["wc"], sp["coef"], sp["pw"], sp["dbl"], d, w_glu, b_glu, s0)


def _s5_step_kernel(u_ref, wb_ref, wc_ref, coef_ref, abar_ref, d_ref, wg_ref, bg_ref, s0_ref,
                    y_ref, s_ref):
    u = u_ref[...]
    bu = _bdot(u, wb_ref[...])
    b_re, b_im = bu[:, :S_LANES], bu[:, S_LANES:]
    c_re, c_im = coef_ref[0:1, :], coef_ref[1:2, :]
    a_re, a_im = abar_ref[0:1, :], abar_ref[1:2, :]
    s0_re, s0_im = s0_ref[:, :S_LANES], s0_ref[:, S_LANES:]
    s_re = c_re * b_re - c_im * b_im + (a_re * s0_re - a_im * s0_im)
    s_im = c_re * b_im + c_im * b_re + (a_re * s0_im + a_im * s0_re)
    s_ref[:, :S_LANES] = s_re
    s_ref[:, S_LANES:] = s_im
    y_ref[...] = _s5_glu(s_re, s_im, u, wc_ref, d_ref, wg_ref, bg_ref)


def _s5_step(u, sp, d, w_glu, b_glu, s0):
    n = u.shape[0]
    return pl.pallas_call(
        _s5_step_kernel,
        out_shape=[jax.ShapeDtypeStruct((n, S_WIDTH), F32), jax.ShapeDtypeStruct((n, 2 * S_LANES), F32)],
        compiler_params=_cparams(),
        name="s5_step",
    )(u, sp["wb"], sp["wc"], sp["coef"], sp["abar"], d, w_glu, b_glu, s0)


def _rwkv_pre_kernel(decode, c_ref, prev_ref, sh0_ref, mu_ref, w0_ref, w2_ref, a0_ref, a2_ref, g2_ref,
                     kk_ref, ka_ref, rk_ref,
                     r_o, lw_o, k_o, v_o, a_o, b_o, g_o, bonus_o):
    c = c_ref[...]
    if decode:
        prev = sh0_ref[...]
    else:
        j = pl.program_id(1)
        last = jnp.where(j == 0, sh0_ref[0], prev_ref[SUBLANES - 1:SUBLANES, :])
        row = lax.broadcasted_iota(jnp.int32, c.shape, 0)
        prev = jnp.where(row == 0, last, pltpu.roll(c, 1, 0))
    cf = c + (prev - c) * mu_ref[...]
    r = cf[:, :R_WIDTH]
    k = cf[:, R_WIDTH:2 * R_WIDTH]
    v = cf[:, 2 * R_WIDTH:3 * R_WIDTH]
    lora = cf[:, 3 * R_WIDTH:]
    w_log = -jax.nn.softplus(-(w0_ref[...] + _bdot(jnp.tanh(lora), w2_ref[...]))) - 0.5
    a = jax.nn.sigmoid(a0_ref[...] + _bdot(lora, a2_ref[...]))
    g = _bdot(jax.nn.sigmoid(lora), g2_ref[...])
    kk = k * kk_ref[...]
    kk = kk * lax.rsqrt(jnp.maximum(_head_sum(kk * kk, R_HEAD), 1e-24))
    k = k * (1.0 + (a - 1.0) * ka_ref[...])
    r_o[...] = r
    lw_o[...] = -jnp.exp(w_log)
    k_o[...] = k
    v_o[...] = v
    a_o[...] = -kk
    b_o[...] = kk * a
    g_o[...] = g
    bonus_o[...] = _head_sum(r * k * rk_ref[...], R_HEAD) * v


def _rwkv_weights(mu, w0, w2, a0, a2, g2, k_k, k_a, r_k):
    z = lambda n: jnp.zeros((n, R_WIDTH), F32)
    w2p = jnp.concatenate([w2, z(A_LORA + G_LORA)], 0).astype(BF16)
    a2p = jnp.concatenate([z(W_LORA), a2, z(G_LORA)], 0).astype(BF16)
    g2p = jnp.concatenate([z(W_LORA + A_LORA), g2], 0).astype(BF16)
    row = lambda x: x.reshape(1, -1)
    return [row(mu), row(w0), w2p, row(a0), a2p, g2p, row(k_k), row(k_a), row(r_k)]


def _rwkv_pre(c, sh0, wts, nb, seq, t, decode):
    m = c.shape[0]
    outs = [jax.ShapeDtypeStruct((m, R_WIDTH), F32)] * 8
    wspecs = [_full(w.shape) for w in wts]
    if decode:
        return pl.pallas_call(
            functools.partial(_rwkv_pre_kernel, True),
            grid=(1,),
            in_specs=[_full((m, R_IN)), _full((SUBLANES, R_IN)), _full((m, R_IN))] + wspecs,
            out_specs=[_full((m, R_WIDTH))] * 8,
            out_shape=outs,
            compiler_params=_cparams("arbitrary"),
            name="rwkv_pre_step",
        )(c, c[:SUBLANES], sh0, *wts)
    nt = seq // t
    per8 = t // SUBLANES
    return pl.pallas_call(
        functools.partial(_rwkv_pre_kernel, False),
        grid=(nb, nt),
        in_specs=[pl.BlockSpec((t, R_IN), lambda b, j: (b * nt + j, 0)),
                  pl.BlockSpec((SUBLANES, R_IN), lambda b, j: (jnp.maximum((b * nt + j) * per8 - 1, 0), 0)),
                  pl.BlockSpec((1, 1, R_IN), lambda b, j: (b, 0, 0))] + wspecs,
        out_specs=[pl.BlockSpec((t, R_WIDTH), lambda b, j: (b * nt + j, 0))] * 8,
        out_shape=outs,
        compiler_params=_cparams("parallel", "parallel"),
        name="rwkv_pre",
    )(c, c, sh0.reshape(nb, 1, R_IN), *wts)


def _rwkv_chunk_kernel(r_ref, lw_ref, k_ref, v_ref, a_ref, b_ref, y1_ref, y2_ref, g_ref, f_ref):
    t = RWKV_CHUNK
    n_chunks = r_ref.shape[0] // t
    row = lax.broadcasted_iota(jnp.int32, (t, R_WIDTH), 0)
    ri = lax.broadcasted_iota(jnp.int32, (t, t), 0)
    ci = lax.broadcasted_iota(jnp.int32, (t, t), 1)
    hi = lax.broadcasted_iota(jnp.int32, (R_HEAD, R_HEAD), 0)
    hj = lax.broadcasted_iota(jnp.int32, (R_HEAD, R_HEAD), 1)
    per_head = lambda x: [x[:, h * R_HEAD:(h + 1) * R_HEAD] for h in range(R_HEADS)]
    ah, rh, bh, kh, vh, be_t, ke_t, ee = ([] for _ in range(8))
    for c in range(n_chunks):
        rows = slice(c * t, (c + 1) * t)
        lw = lw_ref[rows, :]
        cw = lw
        d = 1
        while d < t:
            cw = cw + jnp.where(row >= d, pltpu.roll(cw, d, 0), 0.0)
            d *= 2
        cw_end = cw[t - 1:t, :]
        e_neg = jnp.exp(-cw)
        e_rem = jnp.exp(cw_end - cw)
        ah += per_head(a_ref[rows, :] * jnp.exp(cw - lw))
        rh += per_head(r_ref[rows, :] * jnp.exp(cw))
        bh += per_head(b_ref[rows, :] * e_neg)
        kh += per_head(k_ref[rows, :] * e_neg)
        vh += per_head(v_ref[rows, :])
        be_t += [x.T for x in per_head(b_ref[rows, :] * e_rem)]
        ke_t += [x.T for x in per_head(k_ref[rows, :] * e_rem)]
        ee += per_head(jnp.exp(cw_end))
    units = range(n_chunks * R_HEADS)
    a_ab = [jnp.where(ri > ci, _dot3(ah[u], bh[u], _NT), 0.0) for u in units]
    a_ak = [jnp.where(ri > ci, _dot3(ah[u], kh[u], _NT), 0.0) for u in units]
    a_rb = [jnp.where(ri >= ci, _dot3(rh[u], bh[u], _NT), 0.0) for u in units]
    a_rk = [jnp.where(ri >= ci, _dot3(rh[u], kh[u], _NT), 0.0) for u in units]
    akv = [_dot3(a_ak[u], vh[u]) for u in units]
    rkv = [_dot3(a_rk[u], vh[u]) for u in units]
    inv = [jnp.where(ri == ci, 1.0, 0.0) + a_ab[u] for u in units]
    apow = a_ab
    n = 2
    while n < t:
        apow = [_dot3(apow[u], apow[u]) for u in units]
        inv = [inv[u] + _dot3(apow[u], inv[u]) for u in units]
        n *= 2
    w1 = [_dot3(inv[u], ah[u]) for u in units]
    w2 = [_dot3(inv[u], akv[u]) for u in units]
    for u in units:
        c, h = divmod(u, R_HEADS)
        y1_ref[c, h] = rh[u] + _dot3(a_rb[u], w1[u])
        y2_ref[c, h] = _dot3(a_rb[u], w2[u]) + rkv[u]
        g_ref[c, h] = jnp.where(hi == hj, ee[u], 0.0) + _dot3(be_t[u], w1[u])
        f_ref[c, h] = _dot3(be_t[u], w2[u]) + _dot3(ke_t[u], vh[u])


RWKV_CHUNKS_PER_STEP = 2


def _rwkv_chunks(r, lw, k, v, a, b, t):
    m = r.shape[0]
    per = RWKV_CHUNKS_PER_STEP if (m // t) % RWKV_CHUNKS_PER_STEP == 0 else 1
    nc = m // t
    spec = pl.BlockSpec((per * t, R_WIDTH), lambda i: (i, 0))
    yspec = pl.BlockSpec((per, R_HEADS, t, R_HEAD), lambda i: (i, 0, 0, 0))
    gspec = pl.BlockSpec((per, R_HEADS, R_HEAD, R_HEAD), lambda i: (i, 0, 0, 0))
    return pl.pallas_call(
        _rwkv_chunk_kernel,
        grid=(nc // per,),
        in_specs=[spec] * 6,
        out_specs=[yspec, yspec, gspec, gspec],
        out_shape=[jax.ShapeDtypeStruct((nc, R_HEADS, t, R_HEAD), F32)] * 2
        + [jax.ShapeDtypeStruct((nc, R_HEADS, R_HEAD, R_HEAD), F32)] * 2,
        compiler_params=_cparams("parallel"),
        name="rwkv_chunk",
    )(r, lw, k, v, a, b)


def _rwkv_post(y, bonus, g, lng, lnb):
    m = _head_sum(y, R_HEAD) * (1.0 / R_HEAD)
    yc = y - m
    var = _head_sum(yc * yc, R_HEAD) * (1.0 / R_HEAD)
    return (yc * lax.rsqrt(var + GN_EPS) * lng + lnb + bonus) * g


def _rwkv_scan_kernel(y1_ref, y2_ref, g_ref, f_ref, bonus_ref, gate_ref, lng_ref, lnb_ref, h0_ref,
                      y_ref, hfin_ref, h_scr, y_scr):
    j = pl.program_id(1)

    @pl.when(j == 0)
    def _():
        h_scr[...] = h0_ref[0]

    t = RWKV_CHUNK
    heads = range(R_HEADS)
    states = [h_scr[h] for h in heads]
    for c in range(y1_ref.shape[0]):
        for h in heads:
            y_scr[c * t:(c + 1) * t, h * R_HEAD:(h + 1) * R_HEAD] = (
                _dot3(y1_ref[c, h], states[h]) + y2_ref[c, h])
        states = [_dot3(g_ref[c, h], states[h]) + f_ref[c, h] for h in heads]
    for h in heads:
        h_scr[h] = states[h]
    y_ref[...] = _rwkv_post(y_scr[...], bonus_ref[...], gate_ref[...], lng_ref[...], lnb_ref[...])

    @pl.when(j == pl.num_programs(1) - 1)
    def _():
        hfin_ref[0] = h_scr[...]


RWKV_SCAN_CHUNKS = 4


def _rwkv_scan(y1, y2, g, f, bonus, gate, lng, lnb, h0, nb, seq, t):
    per = RWKV_SCAN_CHUNKS if (seq // t) % RWKV_SCAN_CHUNKS == 0 else 1
    nc = seq // (t * per)
    yspec = pl.BlockSpec((per, R_HEADS, t, R_HEAD), lambda b, j: (b * nc + j, 0, 0, 0))
    gspec = pl.BlockSpec((per, R_HEADS, R_HEAD, R_HEAD), lambda b, j: (b * nc + j, 0, 0, 0))
    row = pl.BlockSpec((per * t, R_WIDTH), lambda b, j: (b * nc + j, 0))
    hspec = pl.BlockSpec((1, R_HEADS, R_HEAD, R_HEAD), lambda b, j: (b, 0, 0, 0))
    return pl.pallas_call(
        _rwkv_scan_kernel,
        grid=(nb, nc),
        in_specs=[yspec, yspec, gspec, gspec, row, row, _full((1, R_WIDTH)), _full((1, R_WIDTH)), hspec],
        out_specs=[row, hspec],
        out_shape=[jax.ShapeDtypeStruct((nb * seq, R_WIDTH), F32),
                   jax.ShapeDtypeStruct((nb, R_HEADS, R_HEAD, R_HEAD), F32)],
        scratch_shapes=[pltpu.VMEM((R_HEADS, R_HEAD, R_HEAD), F32), pltpu.VMEM((per * t, R_WIDTH), F32)],
        compiler_params=_cparams("parallel", "arbitrary"),
        name="rwkv_scan",
    )(y1, y2, g, f, bonus, gate, lng, lnb, h0)


RWKV_STEP_ROWS = 8


def _rwkv_step_kernel(r_ref, lw_ref, k_ref, v_ref, a_ref, b_ref, s_ref, y_ref, so_ref):
    ii = lax.broadcasted_iota(jnp.int32, (R_HEAD, R_HEAD), 0)
    jj = lax.broadcasted_iota(jnp.int32, (R_HEAD, R_HEAD), 1)
    eye = ii == jj
    for n in range(RWKV_STEP_ROWS):
        one = lambda ref: ref[n:n + 1, :]
        s = s_ref[n]
        sa = jnp.sum(s * one(a_ref), -1, keepdims=True)
        v_col = jnp.sum(jnp.where(eye, one(v_ref), 0.0), -1, keepdims=True)
        s_new = s * jnp.exp(one(lw_ref)) + sa * one(b_ref) + v_col * one(k_ref)
        y_col = jnp.sum(s_new * one(r_ref), -1, keepdims=True)
        so_ref[n] = s_new
        y_ref[n:n + 1, :] = jnp.sum(jnp.where(eye, y_col, 0.0), 0, keepdims=True)


def _rwkv_step(vecs, s0):
    n = s0.shape[0]
    vspec = pl.BlockSpec((RWKV_STEP_ROWS, R_HEAD), lambda i: (i, 0))
    sspec = pl.BlockSpec((RWKV_STEP_ROWS, R_HEAD, R_HEAD), lambda i: (i, 0, 0))
    return pl.pallas_call(
        _rwkv_step_kernel,
        grid=(n // RWKV_STEP_ROWS,),
        in_specs=[vspec] * 6 + [sspec],
        out_specs=[vspec, sspec],
        out_shape=[jax.ShapeDtypeStruct((n, R_HEAD), F32), jax.ShapeDtypeStruct((n, R_HEAD, R_HEAD), F32)],
        compiler_params=_cparams("parallel"),
        name="rwkv_step",
    )(*vecs, s0)


def _rwkv_post_kernel(y_ref, bonus_ref, gate_ref, lng_ref, lnb_ref, o_ref):
    o_ref[...] = _rwkv_post(y_ref[...], bonus_ref[...], gate_ref[...], lng_ref[...], lnb_ref[...])


def _rwkv_post_call(y, bonus, gate, lng, lnb):
    return pl.pallas_call(
        _rwkv_post_kernel,
        out_shape=jax.ShapeDtypeStruct(y.shape, F32),
        compiler_params=_cparams(),
        name="rwkv_post_step",
    )(y, bonus, gate, lng, lnb)


def _moba_kernel(qt_ref, k_ref, vt_ref, o_ref, km_scr, bias_scr, s_own, s_even, s_odd, p_odd):
    qi = pl.program_id(2)
    nblk = km_scr.shape[0]
    cols = A_GROUP * MOBA_BLOCK

    @pl.when(qi == 0)
    def _():
        km_scr[...] = jnp.mean(k_ref[0].reshape(nblk, MOBA_BLOCK, A_HEAD), axis=1)

    qt = jnp.concatenate([qt_ref[g * A_HEAD:(g + 1) * A_HEAD, :] for g in range(A_GROUP)], axis=1)
    gate_q = (qt * ATT_SCALE).astype(BF16)
    qb = (qt * (ATT_SCALE * LOG2_E)).astype(BF16)

    def scores(n):
        off = pl.multiple_of(n * MOBA_BLOCK, MOBA_BLOCK)
        return _bdot(k_ref[0, pl.ds(off, MOBA_BLOCK), :], qb)

    s_own[...] = scores(qi)
    s_even[...] = scores(0)
    blk = lax.broadcasted_iota(jnp.int32, (nblk, cols), 0)
    gate = jnp.where(blk < qi, _bdot(km_scr[...], gate_q), NEG_INF)
    sel = jnp.zeros(gate.shape, jnp.bool_)
    for _ in range(MOBA_TOPK):
        m = jnp.max(gate, 0, keepdims=True)
        idx = jnp.min(jnp.where(gate == m, blk, nblk), 0, keepdims=True)
        pick = (blk == idx) & (m > NEG_INF)
        sel = sel | pick
        gate = jnp.where(pick, NEG_INF, gate)
    bias_scr[...] = jnp.where(sel, 0.0, NEG_INF)

    ones_rows = jnp.ones((MOBA_ROWS, MOBA_BLOCK), BF16)

    def values_t(n):
        off = pl.multiple_of(n * MOBA_BLOCK, MOBA_BLOCK)
        return jnp.concatenate([vt_ref[:, pl.ds(off, MOBA_BLOCK)].astype(BF16), ones_rows], axis=0)

    groups = MOBA_BLOCK // MOBA_ROWS

    def softmax_step(s, m, elem_bias=None, col_bias=None):
        def group(r):
            sr = s[r * MOBA_ROWS:(r + 1) * MOBA_ROWS, :]
            return sr if elem_bias is None else sr + elem_bias(r)

        peak = group(0)
        for r in range(1, groups):
            peak = jnp.maximum(peak, group(r))
        top = jnp.max(peak, 0, keepdims=True)
        if col_bias is not None:
            top = top + col_bias
        m_new = jnp.maximum(m, top)
        alpha = jnp.exp2(m - m_new)
        shift = m_new if col_bias is None else m_new - col_bias
        p = jnp.concatenate([jnp.exp2(group(r) - shift).astype(BF16) for r in range(groups)], axis=0)
        return m_new, alpha, p

    p_odd[...] = jnp.zeros(p_odd.shape, BF16)

    def body(j, carry):
        m, acc = carry
        n0 = 2 * j
        pv_prev = _bdot(values_t(jnp.maximum(n0 - 1, 0)), p_odd[...])
        s_odd[...] = scores(n0 + 1)
        m, alpha, p_even = softmax_step(s_even, m, col_bias=bias_scr[pl.ds(n0, 1), :])
        acc = alpha * (acc + pv_prev) + _bdot(values_t(n0), p_even)
        s_even[...] = scores(jnp.minimum(n0 + 2, nblk - 1))
        m, alpha, p = softmax_step(s_odd, m, col_bias=bias_scr[pl.ds(n0 + 1, 1), :])
        p_odd[...] = p
        return m, alpha * acc

    trips = lax.shift_right_logical(qi + 1, 1)
    init = (jnp.full((1, cols), MAX_FLOOR, F32), jnp.zeros((A_HEAD + MOBA_ROWS, cols), F32))
    m, acc = lax.fori_loop(0, trips, body, init)
    pv_prev = _bdot(values_t(jnp.maximum(2 * trips - 1, 0)), p_odd[...])
    key = lax.broadcasted_iota(jnp.int32, (MOBA_ROWS, cols), 0)
    pos = lax.broadcasted_iota(jnp.int32, (MOBA_ROWS, cols), 1) % MOBA_BLOCK
    causal = lambda r: jnp.where(key + r * MOBA_ROWS <= pos, 0.0, NEG_INF)
    _, alpha, p = softmax_step(s_own, m, elem_bias=causal)
    acc = alpha * (acc + pv_prev) + _bdot(values_t(qi), p)
    o = acc[:A_HEAD, :] / acc[A_HEAD:A_HEAD + 1, :]
    for g in range(A_GROUP):
        o_ref[g * A_HEAD:(g + 1) * A_HEAD, :] = o[:, g * MOBA_BLOCK:(g + 1) * MOBA_BLOCK]


def _moba_prompt(qt, kh, vt, nb, seq):
    nblk = seq // MOBA_BLOCK
    qspec = pl.BlockSpec((A_GROUP * A_HEAD, MOBA_BLOCK), lambda b, h, i: (h, b * nblk + i))
    return pl.pallas_call(
        _moba_kernel,
        grid=(nb, A_KV_HEADS, nblk),
        in_specs=[qspec,
                  pl.BlockSpec((1, seq, A_HEAD), lambda b, h, i: (h, b, 0)),
                  pl.BlockSpec((A_HEAD, seq), lambda b, h, i: (h, b))],
        out_specs=qspec,
        out_shape=jax.ShapeDtypeStruct(qt.shape, F32),
        scratch_shapes=[pltpu.VMEM((nblk, A_HEAD), F32), pltpu.VMEM((nblk, A_GROUP * MOBA_BLOCK), F32),
                        pltpu.VMEM((MOBA_BLOCK, A_GROUP * MOBA_BLOCK), F32),
                        pltpu.VMEM((MOBA_BLOCK, A_GROUP * MOBA_BLOCK), F32),
                        pltpu.VMEM((MOBA_BLOCK, A_GROUP * MOBA_BLOCK), F32),
                        pltpu.VMEM((MOBA_BLOCK, A_GROUP * MOBA_BLOCK), BF16)],
        compiler_params=_cparams("parallel", "parallel", "arbitrary"),
        name="moba_prompt",
    )(qt, kh, vt)


def _pages_t(cache):
    return cache.transpose(0, 1, 3, 4, 2).reshape(-1, A_KVW, PAGE_SIZE)


def _kmeans_kernel(n_in, pt_ref, *refs):
    pages, o_ref = refs[:n_in], refs[n_in]
    j = pl.program_id(1)
    per_blk = MOBA_BLOCK // PAGE_SIZE
    blocks = n_in // per_blk

    @pl.when(j == 0)
    def _():
        o_ref[...] = jnp.zeros(o_ref.shape, F32)

    out = o_ref[0]
    lane = lax.broadcasted_iota(jnp.int32, out.shape, 1)
    for i in range(blocks):
        tot = sum(pages[per_blk * i + p][0] for p in range(per_blk))
        mean = jnp.sum(tot, axis=1, keepdims=True) * (1.0 / MOBA_BLOCK)
        out = jnp.where(lane == j * blocks + i, mean, out)
    o_ref[0] = out


DEC_KMEANS_PAGES = 32


def _decode_kmeans(pages, page_table, layer, n_pool):
    nb, n_pages = page_table.shape
    n_in = min(DEC_KMEANS_PAGES, n_pages)
    steps = n_pages // n_in
    per_blk = MOBA_BLOCK // PAGE_SIZE

    def page_spec(i):
        return pl.BlockSpec((1, A_KVW, PAGE_SIZE),
                            lambda b, j, pt: (layer * n_pool + pt[b * n_pages + j * n_in + i], 0, 0))

    return pl.pallas_call(
        functools.partial(_kmeans_kernel, n_in),
        grid_spec=pltpu.PrefetchScalarGridSpec(
            num_scalar_prefetch=1,
            grid=(nb, steps),
            in_specs=[page_spec(i) for i in range(n_in)],
            out_specs=pl.BlockSpec((1, A_KVW, n_pages // per_blk), lambda b, j, pt: (b, 0, 0)),
        ),
        out_shape=jax.ShapeDtypeStruct((nb, A_KVW, n_pages // per_blk), F32),
        compiler_params=_cparams("parallel", "arbitrary"),
        name="decode_kmeans",
    )(page_table.reshape(-1), *([pages] * n_in))


def _decode_select_kernel(q_ref, km_ref, o_ref):
    nblk = km_ref.shape[2]
    gate = _bdot(q_ref[0], km_ref[0])
    col = lax.broadcasted_iota(jnp.int32, gate.shape, 1)
    lane = lax.broadcasted_iota(jnp.int32, (A_HEADS, LANES), 1)
    out = jnp.zeros((A_HEADS, LANES), jnp.int32)
    for r in range(MOBA_TOPK):
        m = jnp.max(gate, -1, keepdims=True)
        idx = jnp.min(jnp.where(gate == m, col, nblk), -1, keepdims=True)
        out = jnp.where(lane == r, idx, out)
        gate = jnp.where(col == idx, NEG_INF, gate)
    o_ref[0] = out


def _decode_select(q_exp, kmeans):
    nb, _, nblk = kmeans.shape
    return pl.pallas_call(
        _decode_select_kernel,
        grid=(nb,),
        in_specs=[pl.BlockSpec((1, A_HEADS, A_KVW), lambda b: (b, 0, 0)),
                  pl.BlockSpec((1, A_KVW, nblk), lambda b: (b, 0, 0))],
        out_specs=pl.BlockSpec((1, A_HEADS, LANES), lambda b: (b, 0, 0)),
        out_shape=jax.ShapeDtypeStruct((nb, A_HEADS, LANES), jnp.int32),
        compiler_params=_cparams("parallel"),
        name="decode_select",
    )(q_exp, kmeans)


DEC_PAGES = MOBA_TOPK * (MOBA_BLOCK // PAGE_SIZE)


def _decode_attn_kernel(pg_ref, q_ref, kn_ref, vn_ref, *refs):
    k_pages, v_pages, o_ref = refs[:DEC_PAGES], refs[DEC_PAGES:2 * DEC_PAGES], refs[2 * DEC_PAGES]
    h = pl.program_id(1)
    q = q_ref[0, pl.ds(h, 1), :].astype(BF16)
    rb = lambda x: x.astype(BF16).astype(F32)
    s_self = jnp.sum(rb(q) * rb(kn_ref[0]), -1, keepdims=True) * ATT_SCALE
    scores = [_bdot(q, kp[0]) * ATT_SCALE for kp in k_pages]
    m = s_self
    for s in scores:
        m = jnp.maximum(m, jnp.max(s, -1, keepdims=True))
    p_self = jnp.exp(s_self - m)
    l = p_self
    acc = rb(p_self) * rb(vn_ref[0])
    for s, vp in zip(scores, v_pages):
        p = jnp.exp(s - m)
        l = l + jnp.sum(p, -1, keepdims=True)
        acc = acc + _bdot(p, vp[0], _NT)
    o_ref[0, pl.ds(h, 1), :] = acc / l


def _decode_attn(q_exp, k_new, v_new, kp, vp, page_ids, layer, n_pool):
    nb = q_exp.shape[0]

    def page_spec(i):
        return pl.BlockSpec(
            (1, A_KVW, PAGE_SIZE),
            lambda b, h, pg: (layer * n_pool + pg[(b * A_HEADS + h) * DEC_PAGES + i], 0, 0))

    bspec = pl.BlockSpec((1, A_HEADS, A_KVW), lambda b, h, pg: (b, 0, 0))
    nspec = pl.BlockSpec((1, 1, A_KVW), lambda b, h, pg: (b, 0, 0))
    return pl.pallas_call(
        _decode_attn_kernel,
        grid_spec=pltpu.PrefetchScalarGridSpec(
            num_scalar_prefetch=1,
            grid=(nb, A_HEADS),
            in_specs=[bspec, nspec, nspec] + [page_spec(i) for i in range(DEC_PAGES)] * 2,
            out_specs=bspec,
        ),
        out_shape=jax.ShapeDtypeStruct((nb, A_HEADS, A_KVW), F32),
        compiler_params=_cparams("parallel", "arbitrary"),
        name="decode_attn",
    )(page_ids.reshape(-1), q_exp, k_new.reshape(nb, 1, A_KVW), v_new.reshape(nb, 1, A_KVW),
      *([kp] * DEC_PAGES), *([vp] * DEC_PAGES))


def _moba_decode(q, k_new, v_new, k_pages, v_pages, n_pool, page_table, layer):
    nb = q.shape[0]
    onehot = (jnp.arange(A_KV_HEADS)[None, :] == (jnp.arange(A_HEADS) // A_GROUP)[:, None]).astype(F32)
    q_exp = (q.reshape(nb, A_HEADS, 1, A_HEAD) * onehot[None, :, :, None]).reshape(nb, A_HEADS, A_KVW)
    kmeans = _decode_kmeans(k_pages, page_table, layer, n_pool)
    blk = _decode_select(q_exp, kmeans)[:, :, :MOBA_TOPK]
    per_blk = MOBA_BLOCK // PAGE_SIZE
    pages_of = (blk[..., None] * per_blk + jnp.arange(per_blk)).reshape(nb, A_HEADS * DEC_PAGES)
    page_ids = jnp.take_along_axis(page_table, pages_of, axis=1)
    o = _decode_attn(q_exp, k_new, v_new, k_pages, v_pages, page_ids, layer, n_pool)
    o = o.reshape(nb, A_HEADS, A_KV_HEADS, A_HEAD)
    return jnp.einsum("bhkd,hk->bhd", o, onehot).reshape(nb, A_QW)


def _merge_kernel(ya_transposed, x_ref, ys_ref, yr_ref, ya_ref, wg_ref, ps_ref, pr_ref, pa_ref, wo_ref,
                  g_ref, b_ref, o_ref):
    x = x_ref[...]
    xb = x.astype(BF16)
    ya = ya_ref[...].T if ya_transposed else ya_ref[...]
    merged = None
    for i, (y, p_ref) in enumerate(((ys_ref[...], ps_ref), (yr_ref[...], pr_ref), (ya, pa_ref))):
        gate = jax.nn.sigmoid(_bdot(xb, wg_ref[:, i * D_MODEL:(i + 1) * D_MODEL]))
        term = gate * _bdot(y, p_ref[...])
        merged = term if merged is None else merged + term
    o_ref[...] = _ln(ALPHA * x + _bdot(merged, wo_ref[...]), g_ref[...], b_ref[...])


def _merge(x, ys, yr, ya, wg, ps, pr, pa, wo, g, b, tm, ya_transposed):
    m = x.shape[0]
    row = lambda width: pl.BlockSpec((tm, width), lambda i: (i, 0))
    ya_spec = pl.BlockSpec((A_QW, tm), lambda i: (0, i)) if ya_transposed else row(A_QW)
    return pl.pallas_call(
        functools.partial(_merge_kernel, ya_transposed),
        grid=(m // tm,),
        in_specs=[row(D_MODEL), row(S_WIDTH), row(R_WIDTH), ya_spec,
                  _resident(wg.shape), _resident(ps.shape), _resident(pr.shape), _resident(pa.shape),
                  _resident(wo.shape),
                  _full((1, D_MODEL)), _full((1, D_MODEL))],
        out_specs=row(D_MODEL),
        out_shape=jax.ShapeDtypeStruct((m, D_MODEL), F32),
        compiler_params=_cparams("parallel"),
        name="merge",
    )(x, ys, yr, ya, wg, ps, pr, pa, wo, g, b)


FFN_COLS = D_FF // 2


def _ffn_kernel(tiles_per_seq, x_ref, halo_ref, wup_ref, cw_ref, cb_ref, wdn_ref, g_ref, b_ref, o_ref):
    i = pl.program_id(0)
    x = x_ref[...]
    tm = x.shape[0]
    xe = jnp.concatenate([halo_ref[...], x], axis=0).astype(BF16)
    row = lax.broadcasted_iota(jnp.int32, (tm + SUBLANES, 1), 0)
    keep = jnp.logical_or(row >= SUBLANES, i % tiles_per_seq != 0)
    acc = jnp.zeros((tm, D_MODEL), F32)
    for c in range(D_FF // FFN_COLS):
        halves = []
        for off in (c * FFN_COLS, D_FF + c * FFN_COLS):
            up = jnp.where(keep, _bdot(xe, wup_ref[:, off:off + FFN_COLS]), 0.0)
            cv = cb_ref[:, off:off + FFN_COLS]
            for j in range(CONV_W):
                lo = SUBLANES - (CONV_W - 1) + j
                cv = cv + cw_ref[j:j + 1, off:off + FFN_COLS] * up[lo:lo + tm, :]
            halves.append(cv)
        hmid = jax.nn.gelu(halves[0]) * halves[1]
        acc = acc + _bdot(hmid, wdn_ref[c * FFN_COLS:(c + 1) * FFN_COLS, :])
    o_ref[...] = _ln(ALPHA * x + acc, g_ref[...], b_ref[...])


def _ffn(x, wup, cw, cb, wdn, g, b, seq, tm):
    m = x.shape[0]
    per8 = tm // SUBLANES
    return pl.pallas_call(
        functools.partial(_ffn_kernel, seq // tm),
        grid=(m // tm,),
        in_specs=[pl.BlockSpec((tm, D_MODEL), lambda i: (i, 0)),
                  pl.BlockSpec((SUBLANES, D_MODEL), lambda i: (jnp.maximum(i * per8 - 1, 0), 0)),
                  _resident(wup.shape), _full(cw.shape), _full(cb.shape), _resident(wdn.shape),
                  _full((1, D_MODEL)), _full((1, D_MODEL))],
        out_specs=pl.BlockSpec((tm, D_MODEL), lambda i: (i, 0)),
        out_shape=jax.ShapeDtypeStruct((m, D_MODEL), F32),
        compiler_params=_cparams("parallel"),
        name="ffn",
    )(x, x, wup, cw, cb, wdn, g, b)


def _ffn_step_kernel(x_ref, c0a_ref, c0b_ref, wup_ref, cw_ref, cb_ref, wdn_ref, g_ref, b_ref, o_ref, up_ref):
    x = x_ref[...]
    up = _bdot(x, wup_ref[...])
    up_ref[...] = up
    cv = cb_ref[...] + cw_ref[0:1, :] * c0a_ref[...] + cw_ref[1:2, :] * c0b_ref[...] + cw_ref[2:3, :] * up
    hmid = jax.nn.gelu(cv[:, :D_FF]) * cv[:, D_FF:]
    o_ref[...] = _ln(ALPHA * x + _bdot(hmid, wdn_ref[...]), g_ref[...], b_ref[...])


def _ffn_step(x, c0a, c0b, wup, cw, cb, wdn, g, b):
    n = x.shape[0]
    return pl.pallas_call(
        _ffn_step_kernel,
        out_shape=[jax.ShapeDtypeStruct((n, D_MODEL), F32), jax.ShapeDtypeStruct((n, 2 * D_FF), F32)],
        compiler_params=_cparams(),
        name="ffn_step",
    )(x, c0a, c0b, wup, cw, cb, wdn, g, b)


def _up_rows_kernel(x_ref, w_ref, o_ref):
    o_ref[...] = _bdot(x_ref[...], w_ref[...])


def _up_rows(x, w):
    return pl.pallas_call(
        _up_rows_kernel,
        out_shape=jax.ShapeDtypeStruct((x.shape[0], w.shape[1]), F32),
        compiler_params=_cparams(),
        name="ffn_up_tail",
    )(x, w)


def _layer_params(l, P):
    bf = lambda x: x.astype(BF16)
    row = lambda x: x.reshape(1, -1)
    w_in = P["w_in"][l]
    return dict(
        w_gate=bf(w_in[:, :O_SSM]), w_rest=bf(w_in[:, O_SSM:]),
        s5=_s5_params(P["ssm_a_re"][l], P["ssm_a_im"][l], P["ssm_log_dt"][l], P["ssm_b_re"][l],
                      P["ssm_b_im"][l], P["ssm_c_re"][l], P["ssm_c_im"][l]),
        ssm_d=row(P["ssm_d"][l]), w_glu=bf(P["ssm_w_glu"][l]), b_glu=row(P["ssm_b_glu"][l]),
        rwkv=_rwkv_weights(P["rwkv_mu"][l], P["rwkv_w0"][l], P["rwkv_w2"][l], P["rwkv_a0"][l],
                           P["rwkv_a2"][l], P["rwkv_g2"][l], P["rwkv_k_k"][l], P["rwkv_k_a"][l],
                           P["rwkv_r_k"][l]),
        lnx_g=row(P["rwkv_lnx_g"][l]), lnx_b=row(P["rwkv_lnx_b"][l]),
        proj_ssm=bf(P["proj_ssm"][l]), proj_rwkv=bf(P["proj_rwkv"][l]), proj_attn=bf(P["proj_attn"][l]),
        w_o=bf(P["w_o"][l]), ln1_g=row(P["ln1_g"][l]), ln1_b=row(P["ln1_b"][l]),
        w_up=bf(P["ffn_w_up"][l]), conv_w=P["ffn_conv_w"][l], conv_b=row(P["ffn_conv_b"][l]),
        w_down=bf(P["ffn_w_down"][l]), ln2_g=row(P["ln2_g"][l]), ln2_b=row(P["ln2_b"][l]),
    )


def _pick_tile(n, pref):
    t = min(pref, n)
    while n % t:
        t //= 2
    return t


def _prompt_trunk(x_prompt, P, layers):
    nb, seq, _ = x_prompt.shape
    m = nb * seq
    x = x_prompt.reshape(m, D_MODEL)
    tm = _pick_tile(seq, 512)
    rope = _rope_tables(jnp.arange(seq, dtype=jnp.int32))
    ln_g, ln_b = P["ln_in_g"].reshape(1, -1), P["ln_in_b"].reshape(1, -1)
    outs = []
    for l, lp in enumerate(layers):
        res = _in_proj(x, ln_g, ln_b, lp["w_rest"], rope, tm, seq // tm, pre_ln=(l == 0), attn_layouts=True)
        u, c, k, v, qt, kh, vt = res[:7]
        if l == 0:
            x = res[7]
        y_s, s_fin = _s5_prompt(u, lp["s5"], lp["ssm_d"], lp["w_glu"], lp["b_glu"],
                                jnp.zeros((nb, 2, S_LANES), F32), nb, seq, _pick_tile(seq, 256))
        r, lw, k2, vv, a, b, gate, bonus = _rwkv_pre(c, jnp.zeros((nb, R_IN), F32), lp["rwkv"], nb, seq,
                                                     _pick_tile(seq, 256), decode=False)
        y1, y2, g, f = _rwkv_chunks(r, lw, k2, vv, a, b, RWKV_CHUNK)
        y_r, h_fin = _rwkv_scan(y1, y2, g, f, bonus, gate, lp["lnx_g"], lp["lnx_b"],
                                jnp.zeros((nb, R_HEADS, R_HEAD, R_HEAD), F32), nb, seq, RWKV_CHUNK)
        y_at = _moba_prompt(qt, kh, vt, nb, seq)
        x = _merge(x, y_s, y_r, y_at, lp["w_gate"], lp["proj_ssm"], lp["proj_rwkv"], lp["proj_attn"],
                   lp["w_o"], lp["ln1_g"], lp["ln1_b"], _pick_tile(seq, 512), ya_transposed=True)
        tail = x.reshape(nb, seq, D_MODEL)[:, seq - (CONV_W - 1):].reshape(nb * (CONV_W - 1), D_MODEL)
        conv = _up_rows(tail, lp["w_up"]).reshape(nb, CONV_W - 1, 2 * D_FF)
        x = _ffn(x, lp["w_up"], lp["conv_w"], lp["conv_b"], lp["w_down"], lp["ln2_g"], lp["ln2_b"],
                 seq, _pick_tile(seq, 512))
        outs.append(dict(
            k=k.reshape(nb, seq, A_KV_HEADS, A_HEAD), v=v.reshape(nb, seq, A_KV_HEADS, A_HEAD),
            s_re=s_fin[:, 0].reshape(nb, S_GROUPS, S_STATE), s_im=s_fin[:, 1].reshape(nb, S_GROUPS, S_STATE),
            rwkv=jnp.swapaxes(h_fin, -1, -2), shift=c.reshape(nb, seq, R_IN)[:, -1], conv=conv))
    return x.reshape(nb, seq, D_MODEL), outs


def _sample_trunk(x_sample, P, layers, cache_k, cache_v, page_table, st_re, st_im, st_rwkv, st_shift, st_conv):
    nb = x_sample.shape[0]
    x = x_sample.reshape(nb, D_MODEL)
    past_len = page_table.shape[1] * PAGE_SIZE
    rope = _rope_tables(jnp.full((nb,), past_len, jnp.int32))
    ln_g, ln_b = P["ln_in_g"].reshape(1, -1), P["ln_in_b"].reshape(1, -1)
    k_pages, v_pages = _pages_t(cache_k), _pages_t(cache_v)
    outs = []
    for l, lp in enumerate(layers):
        res = _in_proj(x, ln_g, ln_b, lp["w_rest"], rope, nb, 1, pre_ln=(l == 0), attn_layouts=False)
        u, c, k, v, q = res[:5]
        if l == 0:
            x = res[5]
        s0 = jnp.concatenate([st_re[l].reshape(nb, S_LANES), st_im[l].reshape(nb, S_LANES)], axis=1)
        y_s, s_new = _s5_step(u, lp["s5"], lp["ssm_d"], lp["w_glu"], lp["b_glu"], s0)
        r, lw, k2, vv, a, b, gate, bonus = _rwkv_pre(c, st_shift[l], lp["rwkv"], nb, 1, nb, decode=True)
        per_head = lambda t: t.reshape(nb * R_HEADS, R_HEAD)
        y_rows, s_rwkv = _rwkv_step([per_head(t) for t in (r, lw, k2, vv, a, b)],
                                    st_rwkv[l].reshape(nb * R_HEADS, R_HEAD, R_HEAD))
        y_r = _rwkv_post_call(y_rows.reshape(nb, R_WIDTH), bonus, gate, lp["lnx_g"], lp["lnx_b"])
        y_a = _moba_decode(q, k, v, k_pages, v_pages, cache_k.shape[1], page_table, l)
        x = _merge(x, y_s, y_r, y_a, lp["w_gate"], lp["proj_ssm"], lp["proj_rwkv"], lp["proj_attn"],
                   lp["w_o"], lp["ln1_g"], lp["ln1_b"], nb, ya_transposed=False)
        x, up = _ffn_step(x, st_conv[l][:, 0], st_conv[l][:, 1], lp["w_up"], lp["conv_w"], lp["conv_b"],
                          lp["w_down"], lp["ln2_g"], lp["ln2_b"])
        outs.append(dict(
            k=k.reshape(nb, 1, A_KV_HEADS, A_HEAD), v=v.reshape(nb, 1, A_KV_HEADS, A_HEAD),
            s_re=s_new[:, :S_LANES].reshape(nb, S_GROUPS, S_STATE),
            s_im=s_new[:, S_LANES:].reshape(nb, S_GROUPS, S_STATE),
            rwkv=s_rwkv.reshape(nb, R_HEADS, R_HEAD, R_HEAD), shift=c,
            conv=jnp.stack([st_conv[l][:, 1], up], axis=1)))
    return x.reshape(nb, 1, D_MODEL), outs


def kernel(x_prompt, x_sample, cache_k, cache_v, page_table, state_ssm_re, state_ssm_im, state_rwkv, state_rwkv_shift, state_conv, ln_in_g, ln_in_b, w_in, ssm_a_re, ssm_a_im, ssm_log_dt, ssm_b_re, ssm_b_im, ssm_c_re, ssm_c_im, ssm_d, ssm_w_glu, ssm_b_glu, rwkv_mu, rwkv_w0, rwkv_w2, rwkv_a0, rwkv_a2, rwkv_g2, rwkv_k_k, rwkv_k_a, rwkv_r_k, rwkv_lnx_g, rwkv_lnx_b, proj_ssm, proj_rwkv, proj_attn, w_o, ln1_g, ln1_b, ffn_w_up, ffn_conv_w, ffn_conv_b, ffn_w_down, ln2_g, ln2_b):
    P = dict(ln_in_g=ln_in_g, ln_in_b=ln_in_b, w_in=w_in,
             ssm_a_re=ssm_a_re, ssm_a_im=ssm_a_im, ssm_log_dt=ssm_log_dt,
             ssm_b_re=ssm_b_re, ssm_b_im=ssm_b_im, ssm_c_re=ssm_c_re, ssm_c_im=ssm_c_im,
             ssm_d=ssm_d, ssm_w_glu=ssm_w_glu, ssm_b_glu=ssm_b_glu,
             rwkv_mu=rwkv_mu, rwkv_w0=rwkv_w0, rwkv_w2=rwkv_w2, rwkv_a0=rwkv_a0, rwkv_a2=rwkv_a2,
             rwkv_g2=rwkv_g2, rwkv_k_k=rwkv_k_k, rwkv_k_a=rwkv_k_a, rwkv_r_k=rwkv_r_k,
             rwkv_lnx_g=rwkv_lnx_g, rwkv_lnx_b=rwkv_lnx_b,
             proj_ssm=proj_ssm, proj_rwkv=proj_rwkv, proj_attn=proj_attn, w_o=w_o,
             ln1_g=ln1_g, ln1_b=ln1_b, ffn_w_up=ffn_w_up, ffn_conv_w=ffn_conv_w,
             ffn_conv_b=ffn_conv_b, ffn_w_down=ffn_w_down, ln2_g=ln2_g, ln2_b=ln2_b)
    layers = [_layer_params(l, P) for l in range(w_in.shape[0])]
    y_p, op = _prompt_trunk(x_prompt, P, layers)
    y_s, os_ = _sample_trunk(x_sample, P, layers, cache_k, cache_v, page_table, state_ssm_re, state_ssm_im,
                             state_rwkv, state_rwkv_shift, state_conv)
    st = lambda outs, key: jnp.stack([o[key] for o in outs])
    return (y_p, y_s, st(op, "k"), st(op, "v"), st(os_, "k"), st(os_, "v"),
            st(op, "s_re"), st(op, "s_im"), st(os_, "s_re"), st(os_, "s_im"),
            st(op, "rwkv"), st(os_, "rwkv"), st(op, "shift"), st(os_, "shift"),
            st(op, "conv"), st(os_, "conv"))
```

```python
import functools
import math

import jax
import jax.numpy as jnp
from jax import lax
from jax.experimental import pallas as pl
from jax.experimental.pallas import tpu as pltpu

F32 = jnp.float32
BF16 = jnp.bfloat16

D_MODEL = 1024
PAGE_SIZE = 128
S_GROUP = 16
S_GROUPS = 16
S_STATE = 64
S_WIDTH = S_GROUPS * S_GROUP
S_LANES = S_GROUPS * S_STATE
R_HEAD = 64
R_HEADS = 4
R_WIDTH = R_HEADS * R_HEAD
W_LORA = 32
A_LORA = 32
G_LORA = 64
N_LORA = W_LORA + A_LORA + G_LORA
R_IN = 3 * R_WIDTH + N_LORA
GN_EPS = 64e-5
A_HEAD = 64
A_HEADS = 8
A_KV_HEADS = 4
A_GROUP = A_HEADS // A_KV_HEADS
A_QW = A_HEADS * A_HEAD
A_KVW = A_KV_HEADS * A_HEAD
ROT_DIM = A_HEAD // 4
ROPE_THETA = 500000.0
MOBA_BLOCK = 256
MOBA_TOPK = 3
N_BRANCH = 3
O_SSM = N_BRANCH * D_MODEL
N_REST = S_WIDTH + R_IN + A_QW + 2 * A_KVW
D_FF = 2816
CONV_W = 3
DEPTH = 2
ALPHA = (2 * DEPTH) ** 0.25
LN_EPS = 1e-5
ATT_SCALE = A_HEAD ** -0.5
LOG2_E = math.log2(math.e)

VMEM_LIMIT_BYTES = 56 * 1024 * 1024
LANES = 128
SUBLANES = 8

MOBA_ROWS = 16
RWKV_CHUNK = 64
NEG_INF = float("-inf")
MAX_FLOOR = -1e30

_NN = (((1,), (0,)), ((), ()))
_NT = (((1,), (1,)), ((), ()))
_TN = (((0,), (0,)), ((), ()))


def _cparams(*sem):
    return pltpu.CompilerParams(dimension_semantics=sem or None, vmem_limit_bytes=VMEM_LIMIT_BYTES)


def _bdot(a, b, dims=_NN):
    return lax.dot_general(a.astype(BF16), b.astype(BF16), dims, preferred_element_type=F32)


def _dot3(a, b, dims=_NN):
    ah = a.astype(BF16)
    al = (a - ah.astype(F32)).astype(BF16)
    bh = b.astype(BF16)
    bl = (b - bh.astype(F32)).astype(BF16)
    f = lambda x, y: lax.dot_general(x, y, dims, preferred_element_type=F32)
    return f(ah, bh) + f(ah, bl) + f(al, bh)


def _ln(x, g, b):
    mu = jnp.mean(x, -1, keepdims=True)
    xc = x - mu
    var = jnp.mean(xc * xc, -1, keepdims=True)
    return xc * lax.rsqrt(var + LN_EPS) * g + b


def _head_sum(x, head):
    lane_head = lax.broadcasted_iota(jnp.int32, (1, x.shape[1]), 1) // head
    out = jnp.zeros_like(x)
    for h in range(x.shape[1] // head):
        m = lane_head == h
        s = jnp.sum(jnp.where(m, x, 0.0), -1, keepdims=True)
        out = jnp.where(m, s, out)
    return out


def _full(shape):
    n = len(shape)
    return pl.BlockSpec(shape, lambda *_: (0,) * n)


def _resident(shape):
    n = len(shape)
    return pl.BlockSpec(shape, lambda *_: (0,) * n, pipeline_mode=pl.Buffered(1))


def _in_proj_kernel(pre_ln, attn_layouts, x_ref, g_ref, b_ref, w_ref, cos_ref, sa_ref, sb_ref,
                    u_ref, c_ref, k_ref, v_ref, *rest):
    x = x_ref[...]
    if pre_ln:
        x = _ln(x, g_ref[...], b_ref[...])
        rest[-1][...] = x
    h = _bdot(x, w_ref[...])
    u_ref[...] = h[:, :S_WIDTH]
    c_ref[...] = h[:, S_WIDTH:S_WIDTH + R_IN]
    cos, sa, sb = cos_ref[...], sa_ref[...], sb_ref[...]
    base = S_WIDTH + R_IN
    nq = A_QW // LANES
    per = LANES // A_HEAD
    for j in range((A_QW + A_KVW) // LANES):
        ch = h[:, base + LANES * j: base + LANES * (j + 1)]
        rot = (ch * cos + pltpu.roll(ch, LANES - ROT_DIM // 2, 1) * sa
               + pltpu.roll(ch, ROT_DIM // 2, 1) * sb)
        if j < nq:
            if attn_layouts:
                rest[0][LANES * j:LANES * (j + 1), :] = rot.T
            else:
                rest[0][:, LANES * j:LANES * (j + 1)] = rot
        else:
            k_ref[:, LANES * (j - nq):LANES * (j - nq + 1)] = rot
            if attn_layouts:
                for i in range(per):
                    rest[1][per * (j - nq) + i] = rot[:, A_HEAD * i:A_HEAD * (i + 1)]
    v = h[:, base + A_QW + A_KVW:]
    v_ref[...] = v
    if attn_layouts:
        for j in range(A_KVW // LANES):
            rest[2][LANES * j:LANES * (j + 1), :] = v[:, LANES * j:LANES * (j + 1)].T


def _in_proj(x, ln_g, ln_b, w, rope, tm, pos_tiles, pre_ln, attn_layouts):
    m = x.shape[0]
    row = lambda width: pl.BlockSpec((tm, width), lambda i: (i, 0))
    col = lambda width: pl.BlockSpec((width, tm), lambda i: (0, i))
    tab = pl.BlockSpec((tm, LANES), lambda i: (i % pos_tiles, 0))
    f32 = lambda *shape: jax.ShapeDtypeStruct(shape, F32)
    specs = [row(S_WIDTH), row(R_IN), row(A_KVW), row(A_KVW)]
    shapes = [f32(m, S_WIDTH), f32(m, R_IN), f32(m, A_KVW), f32(m, A_KVW)]
    if attn_layouts:
        specs += [col(A_QW), pl.BlockSpec((A_KV_HEADS, tm, A_HEAD), lambda i: (0, i, 0)), col(A_KVW)]
        shapes += [f32(A_QW, m), f32(A_KV_HEADS, m, A_HEAD), f32(A_KVW, m)]
    else:
        specs += [row(A_QW)]
        shapes += [f32(m, A_QW)]
    if pre_ln:
        specs += [row(D_MODEL)]
        shapes += [f32(m, D_MODEL)]
    return pl.pallas_call(
        functools.partial(_in_proj_kernel, pre_ln, attn_layouts),
        grid=(m // tm,),
        in_specs=[row(D_MODEL), _full((1, D_MODEL)), _full((1, D_MODEL)), _resident((D_MODEL, N_REST)),
                  tab, tab, tab],
        out_specs=specs,
        out_shape=shapes,
        compiler_params=_cparams("parallel"),
        name="in_proj",
    )(x, ln_g, ln_b, w, *rope)


def _rope_tables(pos):
    half = ROT_DIM // 2
    inv = ROPE_THETA ** (-jnp.arange(half, dtype=F32) / half)
    ang = pos.astype(F32)[:, None] * inv[None, :]
    cos, sin = jnp.cos(ang), jnp.sin(ang)
    n = pos.shape[0]
    pad = jnp.zeros((n, A_HEAD - ROT_DIM), F32)
    zero = jnp.zeros((n, half), F32)
    cos_h = jnp.concatenate([cos, cos, pad + 1.0], -1)
    sa_h = jnp.concatenate([-sin, zero, pad], -1)
    sb_h = jnp.concatenate([zero, sin, pad], -1)
    rep = LANES // A_HEAD
    return tuple(jnp.tile(t, (1, rep)) for t in (cos_h, sa_h, sb_h))


def _s5_params(a_re, a_im, log_dt, b_re, b_im, c_re, c_im):
    dt = jnp.exp(log_dt)[:, None]
    lam_re, lam_im = a_re * dt, a_im * dt

    def power(k):
        mag = jnp.exp(lam_re * k)
        return (mag * jnp.cos(lam_im * k)).reshape(-1), (mag * jnp.sin(lam_im * k)).reshape(-1)

    abar_re, abar_im = power(1.0)
    den = (a_re * a_re + a_im * a_im).reshape(-1)
    ar, ai = a_re.reshape(-1), a_im.reshape(-1)
    em_re = abar_re - 1.0
    coef = jnp.stack([(em_re * ar + abar_im * ai) / den, (abar_im * ar - em_re * ai) / den])
    pw = jnp.stack([jnp.stack(x) for x in zip(*[power(float(k)) for k in range(1, SUBLANES + 1)])])
    rows = jnp.arange(SUBLANES)[:, None]
    dbl = jnp.stack([jnp.stack([jnp.where(rows >= d, part[None, :], 0.0) for part in power(float(d))])
                     for d in (1, 2, 4)], axis=1)
    eye = jnp.eye(S_GROUPS, dtype=F32)
    wb = jnp.concatenate([jnp.einsum("gpc,gh->gchp", b, eye).reshape(S_WIDTH, S_LANES)
                          for b in (b_re, b_im)], axis=1)
    wc = jnp.concatenate([jnp.einsum("gcp,gh->gphc", c, eye).reshape(S_LANES, S_WIDTH)
                          for c in (c_re, -c_im)], axis=0)
    return dict(coef=coef, pw=pw, dbl=dbl, abar=jnp.stack([abar_re, abar_im]),
                wb=wb.astype(BF16), wc=wc.astype(BF16))


def _s5_glu(s_re, s_im, u, wc_ref, d_ref, wg_ref, bg_ref):
    s = jnp.concatenate([s_re, s_im], axis=1)
    y = _bdot(s, wc_ref[...]) + d_ref[...] * u
    z = jax.nn.gelu(y)
    return z * jax.nn.sigmoid(_bdot(z, wg_ref[...]) + bg_ref[...])


def _s5_kernel(u_ref, wb_ref, wc_ref, coef_ref, pw_ref, dbl_ref, d_ref, wg_ref, bg_ref, s0_ref,
               y_ref, sfin_ref, s_scr, carry_scr):
    j = pl.program_id(1)
    t = u_ref.shape[0]

    @pl.when(j == 0)
    def _():
        carry_scr[...] = s0_ref[0]

    u = u_ref[...]
    bu = _bdot(u, wb_ref[...])
    b_re, b_im = bu[:, :S_LANES], bu[:, S_LANES:]
    c_re, c_im = coef_ref[0:1, :], coef_ref[1:2, :]
    s_scr[0] = c_re * b_re - c_im * b_im
    s_scr[1] = c_re * b_im + c_im * b_re

    def body(g, carry):
        cr, ci = carry
        off = pl.multiple_of(g * SUBLANES, SUBLANES)
        sr = s_scr[0, pl.ds(off, SUBLANES), :]
        si = s_scr[1, pl.ds(off, SUBLANES), :]
        for n, d in enumerate((1, 2, 4)):
            ar, ai = dbl_ref[0, n], dbl_ref[1, n]
            pr, pi = pltpu.roll(sr, d, 0), pltpu.roll(si, d, 0)
            sr, si = sr + ar * pr - ai * pi, si + ar * pi + ai * pr
        p_re, p_im = pw_ref[0], pw_ref[1]
        sr, si = sr + p_re * cr - p_im * ci, si + p_re * ci + p_im * cr
        s_scr[0, pl.ds(off, SUBLANES), :] = sr
        s_scr[1, pl.ds(off, SUBLANES), :] = si
        return sr[SUBLANES - 1:SUBLANES, :], si[SUBLANES - 1:SUBLANES, :]

    cr, ci = lax.fori_loop(0, t // SUBLANES, body, (carry_scr[0:1, :], carry_scr[1:2, :]), unroll=2)
    carry_scr[0:1, :] = cr
    carry_scr[1:2, :] = ci
    y_ref[...] = _s5_glu(s_scr[0], s_scr[1], u, wc_ref, d_ref, wg_ref, bg_ref)

    @pl.when(j == pl.num_programs(1) - 1)
    def _():
        sfin_ref[0] = carry_scr[...]


def _s5_prompt(u, sp, d, w_glu, b_glu, s0, nb, seq, t):
    nt = seq // t
    return pl.pallas_call(
        _s5_kernel,
        grid=(nb, nt),
        in_specs=[pl.BlockSpec((t, S_WIDTH), lambda b, j: (b * nt + j, 0)),
                  _full((S_WIDTH, 2 * S_LANES)), _full((2 * S_LANES, S_WIDTH)), _full((2, S_LANES)),
                  _full((2, SUBLANES, S_LANES)), _full((2, 3, SUBLANES, S_LANES)), _full((1, S_WIDTH)),
                  _full((S_WIDTH, S_WIDTH)), _full((1, S_WIDTH)),
                  pl.BlockSpec((1, 2, S_LANES), lambda b, j: (b, 0, 0))],
        out_specs=[pl.BlockSpec((t, S_WIDTH), lambda b, j: (b * nt + j, 0)),
                   pl.BlockSpec((1, 2, S_LANES), lambda b, j: (b, 0, 0))],
        out_shape=[jax.ShapeDtypeStruct((nb * seq, S_WIDTH), F32),
                   jax.ShapeDtypeStruct((nb, 2, S_LANES), F32)],
        scratch_shapes=[pltpu.VMEM((2, t, S_LANES), F32), pltpu.VMEM((2, S_LANES), F32)],
        compiler_params=_cparams("parallel", "arbitrary"),
        name="s5_scan",
    )(u, sp["wb"], sp["wc"], sp["coef"], sp["pw"], sp["dbl"], d, w_glu, b_glu, s0)


def _s5_step_kernel(u_ref, wb_ref, wc_ref, coef_ref, abar_ref, d_ref, wg_ref, bg_ref, s0_ref,
                    y_ref, s_ref):
    u = u_ref[...]
    bu = _bdot(u, wb_ref[...])
    b_re, b_im = bu[:, :S_LANES], bu[:, S_LANES:]
    c_re, c_im = coef_ref[0:1, :], coef_ref[1:2, :]
    a_re, a_im = abar_ref[0:1, :], abar_ref[1:2, :]
    s0_re, s0_im = s0_ref[:, :S_LANES], s0_ref[:, S_LANES:]
    s_re = c_re * b_re - c_im * b_im + (a_re * s0_re - a_im * s0_im)
    s_im = c_re * b_im + c_im * b_re + (a_re * s0_im + a_im * s0_re)
    s_ref[:, :S_LANES] = s_re
    s_ref[:, S_LANES:] = s_im
    y_ref[...] = _s5_glu(s_re, s_im, u, wc_ref, d_ref, wg_ref, bg_ref)


def _s5_step(u, sp, d, w_glu, b_glu, s0):
    n = u.shape[0]
    return pl.pallas_call(
        _s5_step_kernel,
        out_shape=[jax.ShapeDtypeStruct((n, S_WIDTH), F32), jax.ShapeDtypeStruct((n, 2 * S_LANES), F32)],
        compiler_params=_cparams(),
        name="s5_step",
    )(u, sp["wb"], sp["wc"], sp["coef"], sp["abar"], d, w_glu, b_glu, s0)


def _rwkv_pre_kernel(decode, c_ref, prev_ref, sh0_ref, mu_ref, w0_ref, w2_ref, a0_ref, a2_ref, g2_ref,
                     kk_ref, ka_ref, rk_ref,
                     r_o, lw_o, k_o, v_o, a_o, b_o, g_o, bonus_o):
    c = c_ref[...]
    if decode:
        prev = sh0_ref[...]
    else:
        j = pl.program_id(1)
        last = jnp.where(j == 0, sh0_ref[0], prev_ref[SUBLANES - 1:SUBLANES, :])
        row = lax.broadcasted_iota(jnp.int32, c.shape, 0)
        prev = jnp.where(row == 0, last, pltpu.roll(c, 1, 0))
    cf = c + (prev - c) * mu_ref[...]
    r = cf[:, :R_WIDTH]
    k = cf[:, R_WIDTH:2 * R_WIDTH]
    v = cf[:, 2 * R_WIDTH:3 * R_WIDTH]
    lora = cf[:, 3 * R_WIDTH:]
    w_log = -jax.nn.softplus(-(w0_ref[...] + _bdot(jnp.tanh(lora), w2_ref[...]))) - 0.5
    a = jax.nn.sigmoid(a0_ref[...] + _bdot(lora, a2_ref[...]))
    g = _bdot(jax.nn.sigmoid(lora), g2_ref[...])
    kk = k * kk_ref[...]
    kk = kk * lax.rsqrt(jnp.maximum(_head_sum(kk * kk, R_HEAD), 1e-24))
    k = k * (1.0 + (a - 1.0) * ka_ref[...])
    r_o[...] = r
    lw_o[...] = -jnp.exp(w_log)
    k_o[...] = k
    v_o[...] = v
    a_o[...] = -kk
    b_o[...] = kk * a
    g_o[...] = g
    bonus_o[...] = _head_sum(r * k * rk_ref[...], R_HEAD) * v


def _rwkv_weights(mu, w0, w2, a0, a2, g2, k_k, k_a, r_k):
    z = lambda n: jnp.zeros((n, R_WIDTH), F32)
    w2p = jnp.concatenate([w2, z(A_LORA + G_LORA)], 0).astype(BF16)
    a2p = jnp.concatenate([z(W_LORA), a2, z(G_LORA)], 0).astype(BF16)
    g2p = jnp.concatenate([z(W_LORA + A_LORA), g2], 0).astype(BF16)
    row = lambda x: x.reshape(1, -1)
    return [row(mu), row(w0), w2p, row(a0), a2p, g2p, row(k_k), row(k_a), row(r_k)]


def _rwkv_pre(c, sh0, wts, nb, seq, t, decode):
    m = c.shape[0]
    outs = [jax.ShapeDtypeStruct((m, R_WIDTH), F32)] * 8
    wspecs = [_full(w.shape) for w in wts]
    if decode:
        return pl.pallas_call(
            functools.partial(_rwkv_pre_kernel, True),
            grid=(1,),
            in_specs=[_full((m, R_IN)), _full((SUBLANES, R_IN)), _full((m, R_IN))] + wspecs,
            out_specs=[_full((m, R_WIDTH))] * 8,
            out_shape=outs,
            compiler_params=_cparams("arbitrary"),
            name="rwkv_pre_step",
        )(c, c[:SUBLANES], sh0, *wts)
    nt = seq // t
    per8 = t // SUBLANES
    return pl.pallas_call(
        functools.partial(_rwkv_pre_kernel, False),
        grid=(nb, nt),
        in_specs=[pl.BlockSpec((t, R_IN), lambda b, j: (b * nt + j, 0)),
                  pl.BlockSpec((SUBLANES, R_IN), lambda b, j: (jnp.maximum((b * nt + j) * per8 - 1, 0), 0)),
                  pl.BlockSpec((1, 1, R_IN), lambda b, j: (b, 0, 0))] + wspecs,
        out_specs=[pl.BlockSpec((t, R_WIDTH), lambda b, j: (b * nt + j, 0))] * 8,
        out_shape=outs,
        compiler_params=_cparams("parallel", "parallel"),
        name="rwkv_pre",
    )(c, c, sh0.reshape(nb, 1, R_IN), *wts)


def _rwkv_chunk_kernel(r_ref, lw_ref, k_ref, v_ref, a_ref, b_ref, y1_ref, y2_ref, g_ref, f_ref):
    t = RWKV_CHUNK
    n_chunks = r_ref.shape[0] // t
    row = lax.broadcasted_iota(jnp.int32, (t, R_WIDTH), 0)
    ri = lax.broadcasted_iota(jnp.int32, (t, t), 0)
    ci = lax.broadcasted_iota(jnp.int32, (t, t), 1)
    hi = lax.broadcasted_iota(jnp.int32, (R_HEAD, R_HEAD), 0)
    hj = lax.broadcasted_iota(jnp.int32, (R_HEAD, R_HEAD), 1)
    per_head = lambda x: [x[:, h * R_HEAD:(h + 1) * R_HEAD] for h in range(R_HEADS)]
    ah, rh, bh, kh, vh, be_t, ke_t, ee = ([] for _ in range(8))
    for c in range(n_chunks):
        rows = slice(c * t, (c + 1) * t)
        lw = lw_ref[rows, :]
        cw = lw
        d = 1
        while d < t:
            cw = cw + jnp.where(row >= d, pltpu.roll(cw, d, 0), 0.0)
            d *= 2
        cw_end = cw[t - 1:t, :]
        e_neg = jnp.exp(-cw)
        e_rem = jnp.exp(cw_end - cw)
        ah += per_head(a_ref[rows, :] * jnp.exp(cw - lw))
        rh += per_head(r_ref[rows, :] * jnp.exp(cw))
        bh += per_head(b_ref[rows, :] * e_neg)
        kh += per_head(k_ref[rows, :] * e_neg)
        vh += per_head(v_ref[rows, :])
        be_t += [x.T for x in per_head(b_ref[rows, :] * e_rem)]
        ke_t += [x.T for x in per_head(k_ref[rows, :] * e_rem)]
        ee += per_head(jnp.exp(cw_end))
    units = range(n_chunks * R_HEADS)
    pr = lax.broadcasted_iota(jnp.int32, (2 * t, 2 * t), 0)
    pc = lax.broadcasted_iota(jnp.int32, (2 * t, 2 * t), 1)
    keep = ((pr < t) & (pr > pc % t)) | ((pr >= t) & (pr - t >= pc % t))
    quad = [jnp.where(keep, _dot3(jnp.concatenate([ah[u], rh[u]], 0),
                                  jnp.concatenate([kh[u], bh[u]], 0), _NT), 0.0) for u in units]
    kv = [_dot3(quad[u][:, :t], vh[u]) for u in units]
    left = lax.broadcasted_iota(jnp.int32, (t, 2 * t), 1) < t
    eye2 = jnp.where(lax.broadcasted_iota(jnp.int32, (t, 2 * t), 0)
                     == lax.broadcasted_iota(jnp.int32, (t, 2 * t), 1), 1.0, 0.0)
    ip = [jnp.where(left, eye2, quad[u][:t, :]) for u in units]
    n = 1
    while n < t:
        ip = [jnp.where(left, ip[u], 0.0) + _dot3(ip[u][:, t:], ip[u]) for u in units]
        n *= 2
    w = [_dot3(ip[u][:, :t], jnp.concatenate([ah[u], kv[u][:t, :]], 1)) for u in units]
    yy = [_dot3(quad[u][t:, t:], w[u]) for u in units]
    gf = [_dot3(be_t[u], w[u]) for u in units]
    kev = [_dot3(ke_t[u], vh[u]) for u in units]
    for u in units:
        c, h = divmod(u, R_HEADS)
        y1_ref[c, h] = rh[u] + yy[u][:, :R_HEAD]
        y2_ref[c, h] = yy[u][:, R_HEAD:] + kv[u][t:, :]
        g_ref[c, h] = jnp.where(hi == hj, ee[u], 0.0) + gf[u][:, :R_HEAD]
        f_ref[c, h] = gf[u][:, R_HEAD:] + kev[u]


RWKV_CHUNKS_PER_STEP = 4


def _rwkv_chunks(r, lw, k, v, a, b, t):
    m = r.shape[0]
    per = RWKV_CHUNKS_PER_STEP if (m // t) % RWKV_CHUNKS_PER_STEP == 0 else 1
    nc = m // t
    spec = pl.BlockSpec((per * t, R_WIDTH), lambda i: (i, 0))
    yspec = pl.BlockSpec((per, R_HEADS, t, R_HEAD), lambda i: (i, 0, 0, 0))
    gspec = pl.BlockSpec((per, R_HEADS, R_HEAD, R_HEAD), lambda i: (i, 0, 0, 0))
    return pl.pallas_call(
        _rwkv_chunk_kernel,
        grid=(nc // per,),
        in_specs=[spec] * 6,
        out_specs=[yspec, yspec, gspec, gspec],
        out_shape=[jax.ShapeDtypeStruct((nc, R_HEADS, t, R_HEAD), F32)] * 2
        + [jax.ShapeDtypeStruct((nc, R_HEADS, R_HEAD, R_HEAD), F32)] * 2,
        compiler_params=_cparams("parallel"),
        name="rwkv_chunk",
    )(r, lw, k, v, a, b)


def _rwkv_post(y, bonus, g, lng, lnb):
    m = _head_sum(y, R_HEAD) * (1.0 / R_HEAD)
    yc = y - m
    var = _head_sum(yc * yc, R_HEAD) * (1.0 / R_HEAD)
    return (yc * lax.rsqrt(var + GN_EPS) * lng + lnb + bonus) * g


def _rwkv_scan_kernel(y1_ref, y2_ref, g_ref, f_ref, bonus_ref, gate_ref, lng_ref, lnb_ref, h0_ref,
                      y_ref, hfin_ref, h_scr, y_scr):
    j = pl.program_id(1)

    @pl.when(j == 0)
    def _():
        h_scr[...] = h0_ref[0]

    t = RWKV_CHUNK
    heads = range(R_HEADS)
    states = [h_scr[h] for h in heads]
    for c in range(y1_ref.shape[0]):
        for h in heads:
            y_scr[c * t:(c + 1) * t, h * R_HEAD:(h + 1) * R_HEAD] = (
                _dot3(y1_ref[c, h], states[h]) + y2_ref[c, h])
        states = [_dot3(g_ref[c, h], states[h]) + f_ref[c, h] for h in heads]
    for h in heads:
        h_scr[h] = states[h]
    y_ref[...] = _rwkv_post(y_scr[...], bonus_ref[...], gate_ref[...], lng_ref[...], lnb_ref[...])

    @pl.when(j == pl.num_programs(1) - 1)
    def _():
        hfin_ref[0] = h_scr[...]


RWKV_SCAN_CHUNKS = 4


def _rwkv_scan(y1, y2, g, f, bonus, gate, lng, lnb, h0, nb, seq, t):
    per = RWKV_SCAN_CHUNKS if (seq // t) % RWKV_SCAN_CHUNKS == 0 else 1
    nc = seq // (t * per)
    yspec = pl.BlockSpec((per, R_HEADS, t, R_HEAD), lambda b, j: (b * nc + j, 0, 0, 0))
    gspec = pl.BlockSpec((per, R_HEADS, R_HEAD, R_HEAD), lambda b, j: (b * nc + j, 0, 0, 0))
    row = pl.BlockSpec((per * t, R_WIDTH), lambda b, j: (b * nc + j, 0))
    hspec = pl.BlockSpec((1, R_HEADS, R_HEAD, R_HEAD), lambda b, j: (b, 0, 0, 0))
    return pl.pallas_call(
        _rwkv_scan_kernel,
        grid=(nb, nc),
        in_specs=[yspec, yspec, gspec, gspec, row, row, _full((1, R_WIDTH)), _full((1, R_WIDTH)), hspec],
        out_specs=[row, hspec],
        out_shape=[jax.ShapeDtypeStruct((nb * seq, R_WIDTH), F32),
                   jax.ShapeDtypeStruct((nb, R_HEADS, R_HEAD, R_HEAD), F32)],
        scratch_shapes=[pltpu.VMEM((R_HEADS, R_HEAD, R_HEAD), F32), pltpu.VMEM((per * t, R_WIDTH), F32)],
        compiler_params=_cparams("parallel", "arbitrary"),
        name="rwkv_scan",
    )(y1, y2, g, f, bonus, gate, lng, lnb, h0)


RWKV_STEP_ROWS = 8


def _rwkv_step_kernel(r_ref, lw_ref, k_ref, v_ref, a_ref, b_ref, s_ref, y_ref, so_ref):
    ii = lax.broadcasted_iota(jnp.int32, (R_HEAD, R_HEAD), 0)
    jj = lax.broadcasted_iota(jnp.int32, (R_HEAD, R_HEAD), 1)
    eye = ii == jj
    for n in range(RWKV_STEP_ROWS):
        one = lambda ref: ref[n:n + 1, :]
        s = s_ref[n]
        sa = jnp.sum(s * one(a_ref), -1, keepdims=True)
        v_col = jnp.sum(jnp.where(eye, one(v_ref), 0.0), -1, keepdims=True)
        s_new = s * jnp.exp(one(lw_ref)) + sa * one(b_ref) + v_col * one(k_ref)
        y_col = jnp.sum(s_new * one(r_ref), -1, keepdims=True)
        so_ref[n] = s_new
        y_ref[n:n + 1, :] = jnp.sum(jnp.where(eye, y_col, 0.0), 0, keepdims=True)


def _rwkv_step(vecs, s0):
    n = s0.shape[0]
    vspec = pl.BlockSpec((RWKV_STEP_ROWS, R_HEAD), lambda i: (i, 0))
    sspec = pl.BlockSpec((RWKV_STEP_ROWS, R_HEAD, R_HEAD), lambda i: (i, 0, 0))
    return pl.pallas_call(
        _rwkv_step_kernel,
        grid=(n // RWKV_STEP_ROWS,),
        in_specs=[vspec] * 6 + [sspec],
        out_specs=[vspec, sspec],
        out_shape=[jax.ShapeDtypeStruct((n, R_HEAD), F32), jax.ShapeDtypeStruct((n, R_HEAD, R_HEAD), F32)],
        compiler_params=_cparams("parallel"),
        name="rwkv_step",
    )(*vecs, s0)


def _rwkv_post_kernel(y_ref, bonus_ref, gate_ref, lng_ref, lnb_ref, o_ref):
    o_ref[...] = _rwkv_post(y_ref[...], bonus_ref[...], gate_ref[...], lng_ref[...], lnb_ref[...])


def _rwkv_post_call(y, bonus, gate, lng, lnb):
    return pl.pallas_call(
        _rwkv_post_kernel,
        out_shape=jax.ShapeDtypeStruct(y.shape, F32),
        compiler_params=_cparams(),
        name="rwkv_post_step",
    )(y, bonus, gate, lng, lnb)


def _moba_kernel(qt_ref, k_ref, vt_ref, o_ref, km_scr, bias_scr, s_own, s_even, s_odd, p_odd):
    qi = pl.program_id(2)
    nblk = km_scr.shape[0]
    cols = A_GROUP * MOBA_BLOCK

    @pl.when(qi == 0)
    def _():
        km_scr[...] = jnp.mean(k_ref[0].reshape(nblk, MOBA_BLOCK, A_HEAD), axis=1)

    qt = jnp.concatenate([qt_ref[g * A_HEAD:(g + 1) * A_HEAD, :] for g in range(A_GROUP)], axis=1)
    gate_q = (qt * ATT_SCALE).astype(BF16)
    qb = (qt * (ATT_SCALE * LOG2_E)).astype(BF16)

    def scores(n):
        off = pl.multiple_of(n * MOBA_BLOCK, MOBA_BLOCK)
        return _bdot(k_ref[0, pl.ds(off, MOBA_BLOCK), :], qb)

    s_own[...] = scores(qi)
    s_even[...] = scores(0)
    blk = lax.broadcasted_iota(jnp.int32, (nblk, cols), 0)
    gate = jnp.where(blk < qi, _bdot(km_scr[...], gate_q), NEG_INF)
    sel = jnp.zeros(gate.shape, jnp.bool_)
    for _ in range(MOBA_TOPK):
        m = jnp.max(gate, 0, keepdims=True)
        idx = jnp.min(jnp.where(gate == m, blk, nblk), 0, keepdims=True)
        pick = (blk == idx) & (m > NEG_INF)
        sel = sel | pick
        gate = jnp.where(pick, NEG_INF, gate)
    bias_scr[...] = jnp.where(sel, 0.0, NEG_INF)

    ones_rows = jnp.ones((MOBA_ROWS, MOBA_BLOCK), BF16)

    def values_t(n):
        off = pl.multiple_of(n * MOBA_BLOCK, MOBA_BLOCK)
        return jnp.concatenate([vt_ref[:, pl.ds(off, MOBA_BLOCK)].astype(BF16), ones_rows], axis=0)

    groups = MOBA_BLOCK // MOBA_ROWS

    def softmax_step(s, m, elem_bias=None, col_bias=None):
        def group(r):
            sr = s[r * MOBA_ROWS:(r + 1) * MOBA_ROWS, :]
            return sr if elem_bias is None else sr + elem_bias(r)

        peak = group(0)
        for r in range(1, groups):
            peak = jnp.maximum(peak, group(r))
        top = jnp.max(peak, 0, keepdims=True)
        if col_bias is not None:
            top = top + col_bias
        m_new = jnp.maximum(m, top)
        alpha = jnp.exp2(m - m_new)
        shift = m_new if col_bias is None else m_new - col_bias
        p = jnp.concatenate([jnp.exp2(group(r) - shift).astype(BF16) for r in range(groups)], axis=0)
        return m_new, alpha, p

    p_odd[...] = jnp.zeros(p_odd.shape, BF16)

    def body(j, carry):
        m, acc = carry
        n0 = 2 * j
        pv_prev = _bdot(values_t(jnp.maximum(n0 - 1, 0)), p_odd[...])
        s_odd[...] = scores(n0 + 1)
        m, alpha, p_even = softmax_step(s_even, m, col_bias=bias_scr[pl.ds(n0, 1), :])
        acc = alpha * (acc + pv_prev) + _bdot(values_t(n0), p_even)
        s_even[...] = scores(jnp.minimum(n0 + 2, nblk - 1))
        m, alpha, p = softmax_step(s_odd, m, col_bias=bias_scr[pl.ds(n0 + 1, 1), :])
        p_odd[...] = p
        return m, alpha * acc

    trips = lax.shift_right_logical(qi + 1, 1)
    init = (jnp.full((1, cols), MAX_FLOOR, F32), jnp.zeros((A_HEAD + MOBA_ROWS, cols), F32))
    m, acc = lax.fori_loop(0, trips, body, init)
    pv_prev = _bdot(values_t(jnp.maximum(2 * trips - 1, 0)), p_odd[...])
    key = lax.broadcasted_iota(jnp.int32, (MOBA_ROWS, cols), 0)
    pos = lax.broadcasted_iota(jnp.int32, (MOBA_ROWS, cols), 1) % MOBA_BLOCK
    causal = lambda r: jnp.where(key + r * MOBA_ROWS <= pos, 0.0, NEG_INF)
    _, alpha, p = softmax_step(s_own, m, elem_bias=causal)
    acc = alpha * (acc + pv_prev) + _bdot(values_t(qi), p)
    o = acc[:A_HEAD, :] / acc[A_HEAD:A_HEAD + 1, :]
    for g in range(A_GROUP):
        o_ref[g * A_HEAD:(g + 1) * A_HEAD, :] = o[:, g * MOBA_BLOCK:(g + 1) * MOBA_BLOCK]


def _moba_prompt(qt, kh, vt, nb, seq):
    nblk = seq // MOBA_BLOCK
    qspec = pl.BlockSpec((A_GROUP * A_HEAD, MOBA_BLOCK), lambda b, h, i: (h, b * nblk + i))
    return pl.pallas_call(
        _moba_kernel,
        grid=(nb, A_KV_HEADS, nblk),
        in_specs=[qspec,
                  pl.BlockSpec((1, seq, A_HEAD), lambda b, h, i: (h, b, 0)),
                  pl.BlockSpec((A_HEAD, seq), lambda b, h, i: (h, b))],
        out_specs=qspec,
        out_shape=jax.ShapeDtypeStruct(qt.shape, F32),
        scratch_shapes=[pltpu.VMEM((nblk, A_HEAD), F32), pltpu.VMEM((nblk, A_GROUP * MOBA_BLOCK), F32),
                        pltpu.VMEM((MOBA_BLOCK, A_GROUP * MOBA_BLOCK), F32),
                        pltpu.VMEM((MOBA_BLOCK, A_GROUP * MOBA_BLOCK), F32),
                        pltpu.VMEM((MOBA_BLOCK, A_GROUP * MOBA_BLOCK), F32),
                        pltpu.VMEM((MOBA_BLOCK, A_GROUP * MOBA_BLOCK), BF16)],
        compiler_params=_cparams("parallel", "parallel", "arbitrary"),
        name="moba_prompt",
    )(qt, kh, vt)


def _pages_t(cache):
    return cache.transpose(0, 1, 3, 4, 2).reshape(-1, A_KVW, PAGE_SIZE)


def _kmeans_kernel(n_in, pt_ref, *refs):
    pages, o_ref = refs[:n_in], refs[n_in]
    j = pl.program_id(1)
    per_blk = MOBA_BLOCK // PAGE_SIZE
    blocks = n_in // per_blk

    @pl.when(j == 0)
    def _():
        o_ref[...] = jnp.zeros(o_ref.shape, F32)

    out = o_ref[0]
    lane = lax.broadcasted_iota(jnp.int32, out.shape, 1)
    for i in range(blocks):
        tot = sum(pages[per_blk * i + p][0] for p in range(per_blk))
        mean = jnp.sum(tot, axis=1, keepdims=True) * (1.0 / MOBA_BLOCK)
        out = jnp.where(lane == j * blocks + i, mean, out)
    o_ref[0] = out


DEC_KMEANS_PAGES = 32


def _decode_kmeans(pages, page_table, layer, n_pool):
    nb, n_pages = page_table.shape
    n_in = min(DEC_KMEANS_PAGES, n_pages)
    steps = n_pages // n_in
    per_blk = MOBA_BLOCK // PAGE_SIZE

    def page_spec(i):
        return pl.BlockSpec((1, A_KVW, PAGE_SIZE),
                            lambda b, j, pt: (layer * n_pool + pt[b * n_pages + j * n_in + i], 0, 0))

    return pl.pallas_call(
        functools.partial(_kmeans_kernel, n_in),
        grid_spec=pltpu.PrefetchScalarGridSpec(
            num_scalar_prefetch=1,
            grid=(nb, steps),
            in_specs=[page_spec(i) for i in range(n_in)],
            out_specs=pl.BlockSpec((1, A_KVW, n_pages // per_blk), lambda b, j, pt: (b, 0, 0)),
        ),
        out_shape=jax.ShapeDtypeStruct((nb, A_KVW, n_pages // per_blk), F32),
        compiler_params=_cparams("parallel", "arbitrary"),
        name="decode_kmeans",
    )(page_table.reshape(-1), *([pages] * n_in))


def _decode_select_kernel(q_ref, km_ref, o_ref):
    nblk = km_ref.shape[2]
    gate = _bdot(q_ref[0], km_ref[0])
    col = lax.broadcasted_iota(jnp.int32, gate.shape, 1)
    lane = lax.broadcasted_iota(jnp.int32, (A_HEADS, LANES), 1)
    out = jnp.zeros((A_HEADS, LANES), jnp.int32)
    for r in range(MOBA_TOPK):
        m = jnp.max(gate, -1, keepdims=True)
        idx = jnp.min(jnp.where(gate == m, col, nblk), -1, keepdims=True)
        out = jnp.where(lane == r, idx, out)
        gate = jnp.where(col == idx, NEG_INF, gate)
    o_ref[0] = out


def _decode_select(q_exp, kmeans):
    nb, _, nblk = kmeans.shape
    return pl.pallas_call(
        _decode_select_kernel,
        grid=(nb,),
        in_specs=[pl.BlockSpec((1, A_HEADS, A_KVW), lambda b: (b, 0, 0)),
                  pl.BlockSpec((1, A_KVW, nblk), lambda b: (b, 0, 0))],
        out_specs=pl.BlockSpec((1, A_HEADS, LANES), lambda b: (b, 0, 0)),
        out_shape=jax.ShapeDtypeStruct((nb, A_HEADS, LANES), jnp.int32),
        compiler_params=_cparams("parallel"),
        name="decode_select",
    )(q_exp, kmeans)


DEC_PAGES = MOBA_TOPK * (MOBA_BLOCK // PAGE_SIZE)


def _decode_attn_kernel(pg_ref, q_ref, kn_ref, vn_ref, *refs):
    k_pages, v_pages, o_ref = refs[:DEC_PAGES], refs[DEC_PAGES:2 * DEC_PAGES], refs[2 * DEC_PAGES]
    h = pl.program_id(1)
    q = q_ref[0, pl.ds(h, 1), :].astype(BF16)
    rb = lambda x: x.astype(BF16).astype(F32)
    s_self = jnp.sum(rb(q) * rb(kn_ref[0]), -1, keepdims=True) * ATT_SCALE
    scores = [_bdot(q, kp[0]) * ATT_SCALE for kp in k_pages]
    m = s_self
    for s in scores:
        m = jnp.maximum(m, jnp.max(s, -1, keepdims=True))
    p_self = jnp.exp(s_self - m)
    l = p_self
    acc = rb(p_self) * rb(vn_ref[0])
    for s, vp in zip(scores, v_pages):
        p = jnp.exp(s - m)
        l = l + jnp.sum(p, -1, keepdims=True)
        acc = acc + _bdot(p, vp[0], _NT)
    o_ref[0, pl.ds(h, 1), :] = acc / l


def _decode_attn(q_exp, k_new, v_new, kp, vp, page_ids, layer, n_pool):
    nb = q_exp.shape[0]

    def page_spec(i):
        return pl.BlockSpec(
            (1, A_KVW, PAGE_SIZE),
            lambda b, h, pg: (layer * n_pool + pg[(b * A_HEADS + h) * DEC_PAGES + i], 0, 0))

    bspec = pl.BlockSpec((1, A_HEADS, A_KVW), lambda b, h, pg: (b, 0, 0))
    nspec = pl.BlockSpec((1, 1, A_KVW), lambda b, h, pg: (b, 0, 0))
    return pl.pallas_call(
        _decode_attn_kernel,
        grid_spec=pltpu.PrefetchScalarGridSpec(
            num_scalar_prefetch=1,
            grid=(nb, A_HEADS),
            in_specs=[bspec, nspec, nspec] + [page_spec(i) for i in range(DEC_PAGES)] * 2,
            out_specs=bspec,
        ),
        out_shape=jax.ShapeDtypeStruct((nb, A_HEADS, A_KVW), F32),
        compiler_params=_cparams("parallel", "arbitrary"),
        name="decode_attn",
    )(page_ids.reshape(-1), q_exp, k_new.reshape(nb, 1, A_KVW), v_new.reshape(nb, 1, A_KVW),
      *([kp] * DEC_PAGES), *([vp] * DEC_PAGES))


def _moba_decode(q, k_new, v_new, k_pages, v_pages, n_pool, page_table, layer):
    nb = q.shape[0]
    onehot = (jnp.arange(A_KV_HEADS)[None, :] == (jnp.arange(A_HEADS) // A_GROUP)[:, None]).astype(F32)
    q_exp = (q.reshape(nb, A_HEADS, 1, A_HEAD) * onehot[None, :, :, None]).reshape(nb, A_HEADS, A_KVW)
    kmeans = _decode_kmeans(k_pages, page_table, layer, n_pool)
    blk = _decode_select(q_exp, kmeans)[:, :, :MOBA_TOPK]
    per_blk = MOBA_BLOCK // PAGE_SIZE
    pages_of = (blk[..., None] * per_blk + jnp.arange(per_blk)).reshape(nb, A_HEADS * DEC_PAGES)
    page_ids = jnp.take_along_axis(page_table, pages_of, axis=1)
    o = _decode_attn(q_exp, k_new, v_new, k_pages, v_pages, page_ids, layer, n_pool)
    o = o.reshape(nb, A_HEADS, A_KV_HEADS, A_HEAD)
    return jnp.einsum("bhkd,hk->bhd", o, onehot).reshape(nb, A_QW)


def _merge_kernel(ya_transposed, x_ref, ys_ref, yr_ref, ya_ref, wg_ref, ps_ref, pr_ref, pa_ref, wo_ref,
                  g_ref, b_ref, o_ref):
    x = x_ref[...]
    xb = x.astype(BF16)
    ya = ya_ref[...].T if ya_transposed else ya_ref[...]
    merged = None
    for i, (y, p_ref) in enumerate(((ys_ref[...], ps_ref), (yr_ref[...], pr_ref), (ya, pa_ref))):
        gate = jax.nn.sigmoid(_bdot(xb, wg_ref[:, i * D_MODEL:(i + 1) * D_MODEL]))
        term = gate * _bdot(y, p_ref[...])
        merged = term if merged is None else merged + term
    o_ref[...] = _ln(ALPHA * x + _bdot(merged, wo_ref[...]), g_ref[...], b_ref[...])


def _merge(x, ys, yr, ya, wg, ps, pr, pa, wo, g, b, tm, ya_transposed):
    m = x.shape[0]
    row = lambda width: pl.BlockSpec((tm, width), lambda i: (i, 0))
    ya_spec = pl.BlockSpec((A_QW, tm), lambda i: (0, i)) if ya_transposed else row(A_QW)
    return pl.pallas_call(
        functools.partial(_merge_kernel, ya_transposed),
        grid=(m // tm,),
        in_specs=[row(D_MODEL), row(S_WIDTH), row(R_WIDTH), ya_spec,
                  _resident(wg.shape), _resident(ps.shape), _resident(pr.shape), _resident(pa.shape),
                  _resident(wo.shape),
                  _full((1, D_MODEL)), _full((1, D_MODEL))],
        out_specs=row(D_MODEL),
        out_shape=jax.ShapeDtypeStruct((m, D_MODEL), F32),
        compiler_params=_cparams("parallel"),
        name="merge",
    )(x, ys, yr, ya, wg, ps, pr, pa, wo, g, b)


FFN_COLS = D_FF // 2


def _ffn_kernel(tiles_per_seq, x_ref, halo_ref, wup_ref, cw_ref, cb_ref, wdn_ref, g_ref, b_ref, o_ref):
    i = pl.program_id(0)
    x = x_ref[...]
    tm = x.shape[0]
    xe = jnp.concatenate([halo_ref[...], x], axis=0).astype(BF16)
    row = lax.broadcasted_iota(jnp.int32, (tm + SUBLANES, 1), 0)
    keep = jnp.logical_or(row >= SUBLANES, i % tiles_per_seq != 0)
    acc = jnp.zeros((tm, D_MODEL), F32)
    for c in range(D_FF // FFN_COLS):
        halves = []
        for off in (c * FFN_COLS, D_FF + c * FFN_COLS):
            up = jnp.where(keep, _bdot(xe, wup_ref[:, off:off + FFN_COLS]), 0.0)
            cv = cb_ref[:, off:off + FFN_COLS]
            for j in range(CONV_W):
                lo = SUBLANES - (CONV_W - 1) + j
                cv = cv + cw_ref[j:j + 1, off:off + FFN_COLS] * up[lo:lo + tm, :]
            halves.append(cv)
        hmid = jax.nn.gelu(halves[0]) * halves[1]
        acc = acc + _bdot(hmid, wdn_ref[c * FFN_COLS:(c + 1) * FFN_COLS, :])
    o_ref[...] = _ln(ALPHA * x + acc, g_ref[...], b_ref[...])


def _ffn(x, wup, cw, cb, wdn, g, b, seq, tm):
    m = x.shape[0]
    per8 = tm // SUBLANES
    return pl.pallas_call(
        functools.partial(_ffn_kernel, seq // tm),
        grid=(m // tm,),
        in_specs=[pl.BlockSpec((tm, D_MODEL), lambda i: (i, 0)),
                  pl.BlockSpec((SUBLANES, D_MODEL), lambda i: (jnp.maximum(i * per8 - 1, 0), 0)),
                  _resident(wup.shape), _full(cw.shape), _full(cb.shape), _resident(wdn.shape),
                  _full((1, D_MODEL)), _full((1, D_MODEL))],
        out_specs=pl.BlockSpec((tm, D_MODEL), lambda i: (i, 0)),
        out_shape=jax.ShapeDtypeStruct((m, D_MODEL), F32),
        compiler_params=_cparams("parallel"),
        name="ffn",
    )(x, x, wup, cw, cb, wdn, g, b)


def _ffn_step_kernel(x_ref, c0a_ref, c0b_ref, wup_ref, cw_ref, cb_ref, wdn_ref, g_ref, b_ref, o_ref, up_ref):
    x = x_ref[...]
    up = _bdot(x, wup_ref[...])
    up_ref[...] = up
    cv = cb_ref[...] + cw_ref[0:1, :] * c0a_ref[...] + cw_ref[1:2, :] * c0b_ref[...] + cw_ref[2:3, :] * up
    hmid = jax.nn.gelu(cv[:, :D_FF]) * cv[:, D_FF:]
    o_ref[...] = _ln(ALPHA * x + _bdot(hmid, wdn_ref[...]), g_ref[...], b_ref[...])


def _ffn_step(x, c0a, c0b, wup, cw, cb, wdn, g, b):
    n = x.shape[0]
    return pl.pallas_call(
        _ffn_step_kernel,
        out_shape=[jax.ShapeDtypeStruct((n, D_MODEL), F32), jax.ShapeDtypeStruct((n, 2 * D_FF), F32)],
        compiler_params=_cparams(),
        name="ffn_step",
    )(x, c0a, c0b, wup, cw, cb, wdn, g, b)


def _up_rows_kernel(x_ref, w_ref, o_ref):
    o_ref[...] = _bdot(x_ref[...], w_ref[...])


def _up_rows(x, w):
    return pl.pallas_call(
        _up_rows_kernel,
        out_shape=jax.ShapeDtypeStruct((x.shape[0], w.shape[1]), F32),
        compiler_params=_cparams(),
        name="ffn_up_tail",
    )(x, w)


def _layer_params(l, P):
    bf = lambda x: x.astype(BF16)
    row = lambda x: x.reshape(1, -1)
    w_in = P["w_in"][l]
    return dict(
        w_gate=bf(w_in[:, :O_SSM]), w_rest=bf(w_in[:, O_SSM:]),
        s5=_s5_params(P["ssm_a_re"][l], P["ssm_a_im"][l], P["ssm_log_dt"][l], P["ssm_b_re"][l],
                      P["ssm_b_im"][l], P["ssm_c_re"][l], P["ssm_c_im"][l]),
        ssm_d=row(P["ssm_d"][l]), w_glu=bf(P["ssm_w_glu"][l]), b_glu=row(P["ssm_b_glu"][l]),
        rwkv=_rwkv_weights(P["rwkv_mu"][l], P["rwkv_w0"][l], P["rwkv_w2"][l], P["rwkv_a0"][l],
                           P["rwkv_a2"][l], P["rwkv_g2"][l], P["rwkv_k_k"][l], P["rwkv_k_a"][l],
                           P["rwkv_r_k"][l]),
        lnx_g=row(P["rwkv_lnx_g"][l]), lnx_b=row(P["rwkv_lnx_b"][l]),
        proj_ssm=bf(P["proj_ssm"][l]), proj_rwkv=bf(P["proj_rwkv"][l]), proj_attn=bf(P["proj_attn"][l]),
        w_o=bf(P["w_o"][l]), ln1_g=row(P["ln1_g"][l]), ln1_b=row(P["ln1_b"][l]),
        w_up=bf(P["ffn_w_up"][l]), conv_w=P["ffn_conv_w"][l], conv_b=row(P["ffn_conv_b"][l]),
        w_down=bf(P["ffn_w_down"][l]), ln2_g=row(P["ln2_g"][l]), ln2_b=row(P["ln2_b"][l]),
    )


def _pick_tile(n, pref):
    t = min(pref, n)
    while n % t:
        t //= 2
    return t


def _prompt_trunk(x_prompt, P, layers):
    nb, seq, _ = x_prompt.shape
    m = nb * seq
    x = x_prompt.reshape(m, D_MODEL)
    tm = _pick_tile(seq, 512)
    rope = _rope_tables(jnp.arange(seq, dtype=jnp.int32))
    ln_g, ln_b = P["ln_in_g"].reshape(1, -1), P["ln_in_b"].reshape(1, -1)
    outs = []
    for l, lp in enumerate(layers):
        res = _in_proj(x, ln_g, ln_b, lp["w_rest"], rope, tm, seq // tm, pre_ln=(l == 0), attn_layouts=True)
        u, c, k, v, qt, kh, vt = res[:7]
        if l == 0:
            x = res[7]
        y_s, s_fin = _s5_prompt(u, lp["s5"], lp["ssm_d"], lp["w_glu"], lp["b_glu"],
                                jnp.zeros((nb, 2, S_LANES), F32), nb, seq, _pick_tile(seq, 256))
        r, lw, k2, vv, a, b, gate, bonus = _rwkv_pre(c, jnp.zeros((nb, R_IN), F32), lp["rwkv"], nb, seq,
                                                     _pick_tile(seq, 256), decode=False)
        y1, y2, g, f = _rwkv_chunks(r, lw, k2, vv, a, b, RWKV_CHUNK)
        y_r, h_fin = _rwkv_scan(y1, y2, g, f, bonus, gate, lp["lnx_g"], lp["lnx_b"],
                                jnp.zeros((nb, R_HEADS, R_HEAD, R_HEAD), F32), nb, seq, RWKV_CHUNK)
        y_at = _moba_prompt(qt, kh, vt, nb, seq)
        x = _merge(x, y_s, y_r, y_at, lp["w_gate"], lp["proj_ssm"], lp["proj_rwkv"], lp["proj_attn"],
                   lp["w_o"], lp["ln1_g"], lp["ln1_b"], _pick_tile(seq, 512), ya_transposed=True)
        tail = x.reshape(nb, seq, D_MODEL)[:, seq - (CONV_W - 1):].reshape(nb * (CONV_W - 1), D_MODEL)
        conv = _up_rows(tail, lp["w_up"]).reshape(nb, CONV_W - 1, 2 * D_FF)
        x = _ffn(x, lp["w_up"], lp["conv_w"], lp["conv_b"], lp["w_down"], lp["ln2_g"], lp["ln2_b"],
                 seq, _pick_tile(seq, 512))
        outs.append(dict(
            k=k.reshape(nb, seq, A_KV_HEADS, A_HEAD), v=v.reshape(nb, seq, A_KV_HEADS, A_HEAD),
            s_re=s_fin[:, 0].reshape(nb, S_GROUPS, S_STATE), s_im=s_fin[:, 1].reshape(nb, S_GROUPS, S_STATE),
            rwkv=jnp.swapaxes(h_fin, -1, -2), shift=c.reshape(nb, seq, R_IN)[:, -1], conv=conv))
    return x.reshape(nb, seq, D_MODEL), outs


def _sample_trunk(x_sample, P, layers, cache_k, cache_v, page_table, st_re, st_im, st_rwkv, st_shift, st_conv):
    nb = x_sample.shape[0]
    x = x_sample.reshape(nb, D_MODEL)
    past_len = page_table.shape[1] * PAGE_SIZE
    rope = _rope_tables(jnp.full((nb,), past_len, jnp.int32))
    ln_g, ln_b = P["ln_in_g"].reshape(1, -1), P["ln_in_b"].reshape(1, -1)
    k_pages, v_pages = _pages_t(cache_k), _pages_t(cache_v)
    outs = []
    for l, lp in enumerate(layers):
        res = _in_proj(x, ln_g, ln_b, lp["w_rest"], rope, nb, 1, pre_ln=(l == 0), attn_layouts=False)
        u, c, k, v, q = res[:5]
        if l == 0:
            x = res[5]
        s0 = jnp.concatenate([st_re[l].reshape(nb, S_LANES), st_im[l].reshape(nb, S_LANES)], axis=1)
        y_s, s_new = _s5_step(u, lp["s5"], lp["ssm_d"], lp["w_glu"], lp["b_glu"], s0)
        r, lw, k2, vv, a, b, gate, bonus = _rwkv_pre(c, st_shift[l], lp["rwkv"], nb, 1, nb, decode=True)
        per_head = lambda t: t.reshape(nb * R_HEADS, R_HEAD)
        y_rows, s_rwkv = _rwkv_step([per_head(t) for t in (r, lw, k2, vv, a, b)],
                                    st_rwkv[l].reshape(nb * R_HEADS, R_HEAD, R_HEAD))
        y_r = _rwkv_post_call(y_rows.reshape(nb, R_WIDTH), bonus, gate, lp["lnx_g"], lp["lnx_b"])
        y_a = _moba_decode(q, k, v, k_pages, v_pages, cache_k.shape[1], page_table, l)
        x = _merge(x, y_s, y_r, y_a, lp["w_gate"], lp["proj_ssm"], lp["proj_rwkv"], lp["proj_attn"],
                   lp["w_o"], lp["ln1_g"], lp["ln1_b"], nb, ya_transposed=False)
        x, up = _ffn_step(x, st_conv[l][:, 0], st_conv[l][:, 1], lp["w_up"], lp["conv_w"], lp["conv_b"],
                          lp["w_down"], lp["ln2_g"], lp["ln2_b"])
        outs.append(dict(
            k=k.reshape(nb, 1, A_KV_HEADS, A_HEAD), v=v.reshape(nb, 1, A_KV_HEADS, A_HEAD),
            s_re=s_new[:, :S_LANES].reshape(nb, S_GROUPS, S_STATE),
            s_im=s_new[:, S_LANES:].reshape(nb, S_GROUPS, S_STATE),
            rwkv=s_rwkv.reshape(nb, R_HEADS, R_HEAD, R_HEAD), shift=c,
            conv=jnp.stack([st_conv[l][:, 1], up], axis=1)))
    return x.reshape(nb, 1, D_MODEL), outs


def kernel(x_prompt, x_sample, cache_k, cache_v, page_table, state_ssm_re, state_ssm_im, state_rwkv, state_rwkv_shift, state_conv, ln_in_g, ln_in_b, w_in, ssm_a_re, ssm_a_im, ssm_log_dt, ssm_b_re, ssm_b_im, ssm_c_re, ssm_c_im, ssm_d, ssm_w_glu, ssm_b_glu, rwkv_mu, rwkv_w0, rwkv_w2, rwkv_a0, rwkv_a2, rwkv_g2, rwkv_k_k, rwkv_k_a, rwkv_r_k, rwkv_lnx_g, rwkv_lnx_b, proj_ssm, proj_rwkv, proj_attn, w_o, ln1_g, ln1_b, ffn_w_up, ffn_conv_w, ffn_conv_b, ffn_w_down, ln2_g, ln2_b):
    P = dict(ln_in_g=ln_in_g, ln_in_b=ln_in_b, w_in=w_in,
             ssm_a_re=ssm_a_re, ssm_a_im=ssm_a_im, ssm_log_dt=ssm_log_dt,
             ssm_b_re=ssm_b_re, ssm_b_im=ssm_b_im, ssm_c_re=ssm_c_re, ssm_c_im=ssm_c_im,
             ssm_d=ssm_d, ssm_w_glu=ssm_w_glu, ssm_b_glu=ssm_b_glu,
             rwkv_mu=rwkv_mu, rwkv_w0=rwkv_w0, rwkv_w2=rwkv_w2, rwkv_a0=rwkv_a0, rwkv_a2=rwkv_a2,
             rwkv_g2=rwkv_g2, rwkv_k_k=rwkv_k_k, rwkv_k_a=rwkv_k_a, rwkv_r_k=rwkv_r_k,
             rwkv_lnx_g=rwkv_lnx_g, rwkv_lnx_b=rwkv_lnx_b,
             proj_ssm=proj_ssm, proj_rwkv=proj_rwkv, proj_attn=proj_attn, w_o=w_o,
             ln1_g=ln1_g, ln1_b=ln1_b, ffn_w_up=ffn_w_up, ffn_conv_w=ffn_conv_w,
             ffn_conv_b=ffn_conv_b, ffn_w_down=ffn_w_down, ln2_g=ln2_g, ln2_b=ln2_b)
    layers = [_layer_params(l, P) for l in range(w_in.shape[0])]
    y_p, op = _prompt_trunk(x_prompt, P, layers)
    y_s, os_ = _sample_trunk(x_sample, P, layers, cache_k, cache_v, page_table, state_ssm_re, state_ssm_im,
                             state_rwkv, state_rwkv_shift, state_conv)
    st = lambda outs, key: jnp.stack([o[key] for o in outs])
    return (y_p, y_s, st(op, "k"), st(op, "v"), st(os_, "k"), st(os_, "v"),
            st(op, "s_re"), st(op, "s_im"), st(os_, "s_re"), st(os_, "s_im"),
            st(op, "rwkv"), st(os_, "rwkv"), st(op, "shift"), st(os_, "shift"),
            st(op, "conv"), st(os_, "conv"))
```

```python
import functools
import math

import jax
import jax.numpy as jnp
from jax import lax
from jax.experimental import pallas as pl
from jax.experimental.pallas import tpu as pltpu

F32 = jnp.float32
BF16 = jnp.bfloat16

D_MODEL = 1024
PAGE_SIZE = 128
S_GROUP = 16
S_GROUPS = 16
S_STATE = 64
S_WIDTH = S_GROUPS * S_GROUP
S_LANES = S_GROUPS * S_STATE
R_HEAD = 64
R_HEADS = 4
R_WIDTH = R_HEADS * R_HEAD
W_LORA = 32
A_LORA = 32
G_LORA = 64
N_LORA = W_LORA + A_LORA + G_LORA
R_IN = 3 * R_WIDTH + N_LORA
GN_EPS = 64e-5
A_HEAD = 64
A_HEADS = 8
A_KV_HEADS = 4
A_GROUP = A_HEADS // A_KV_HEADS
A_QW = A_HEADS * A_HEAD
A_KVW = A_KV_HEADS * A_HEAD
ROT_DIM = A_HEAD // 4
ROPE_THETA = 500000.0
MOBA_BLOCK = 256
MOBA_TOPK = 3
N_BRANCH = 3
O_SSM = N_BRANCH * D_MODEL
N_REST = S_WIDTH + R_IN + A_QW + 2 * A_KVW
D_FF = 2816
CONV_W = 3
DEPTH = 2
ALPHA = (2 * DEPTH) ** 0.25
LN_EPS = 1e-5
ATT_SCALE = A_HEAD ** -0.5
LOG2_E = math.log2(math.e)

VMEM_LIMIT_BYTES = 56 * 1024 * 1024
LANES = 128
SUBLANES = 8

MOBA_ROWS = 16
RWKV_CHUNK = 64
NEG_INF = float("-inf")
MAX_FLOOR = -1e30

_NN = (((1,), (0,)), ((), ()))
_NT = (((1,), (1,)), ((), ()))
_TN = (((0,), (0,)), ((), ()))


def _cparams(*sem):
    return pltpu.CompilerParams(dimension_semantics=sem or None, vmem_limit_bytes=VMEM_LIMIT_BYTES)


def _bdot(a, b, dims=_NN):
    return lax.dot_general(a.astype(BF16), b.astype(BF16), dims, preferred_element_type=F32)


def _dot3(a, b, dims=_NN):
    ah = a.astype(BF16)
    al = (a - ah.astype(F32)).astype(BF16)
    bh = b.astype(BF16)
    bl = (b - bh.astype(F32)).astype(BF16)
    f = lambda x, y: lax.dot_general(x, y, dims, preferred_element_type=F32)
    return f(ah, bh) + f(ah, bl) + f(al, bh)


def _ln(x, g, b):
    mu = jnp.mean(x, -1, keepdims=True)
    xc = x - mu
    var = jnp.mean(xc * xc, -1, keepdims=True)
    return xc * lax.rsqrt(var + LN_EPS) * g + b


def _head_sum(x, head):
    lane_head = lax.broadcasted_iota(jnp.int32, (1, x.shape[1]), 1) // head
    out = jnp.zeros_like(x)
    for h in range(x.shape[1] // head):
        m = lane_head == h
        s = jnp.sum(jnp.where(m, x, 0.0), -1, keepdims=True)
        out = jnp.where(m, s, out)
    return out


def _full(shape):
    n = len(shape)
    return pl.BlockSpec(shape, lambda *_: (0,) * n)


def _resident(shape):
    n = len(shape)
    return pl.BlockSpec(shape, lambda *_: (0,) * n, pipeline_mode=pl.Buffered(1))


def _in_proj_kernel(pre_ln, attn_layouts, x_ref, g_ref, b_ref, w_ref, cos_ref, sa_ref, sb_ref,
                    u_ref, c_ref, k_ref, v_ref, *rest):
    x = x_ref[...]
    if pre_ln:
        x = _ln(x, g_ref[...], b_ref[...])
        rest[-1][...] = x
    h = _bdot(x, w_ref[...])
    u_ref[...] = h[:, :S_WIDTH]
    c_ref[...] = h[:, S_WIDTH:S_WIDTH + R_IN]
    cos, sa, sb = cos_ref[...], sa_ref[...], sb_ref[...]
    base = S_WIDTH + R_IN
    nq = A_QW // LANES
    per = LANES // A_HEAD
    for j in range((A_QW + A_KVW) // LANES):
        ch = h[:, base + LANES * j: base + LANES * (j + 1)]
        rot = (ch * cos + pltpu.roll(ch, LANES - ROT_DIM // 2, 1) * sa
               + pltpu.roll(ch, ROT_DIM // 2, 1) * sb)
        if j < nq:
            if attn_layouts:
                rest[0][LANES * j:LANES * (j + 1), :] = rot.T
            else:
                rest[0][:, LANES * j:LANES * (j + 1)] = rot
        else:
            k_ref[:, LANES * (j - nq):LANES * (j - nq + 1)] = rot
            if attn_layouts:
                for i in range(per):
                    rest[1][per * (j - nq) + i] = rot[:, A_HEAD * i:A_HEAD * (i + 1)]
    v = h[:, base + A_QW + A_KVW:]
    v_ref[...] = v
    if attn_layouts:
        for j in range(A_KVW // LANES):
            rest[2][LANES * j:LANES * (j + 1), :] = v[:, LANES * j:LANES * (j + 1)].T


def _in_proj(x, ln_g, ln_b, w, rope, tm, pos_tiles, pre_ln, attn_layouts):
    m = x.shape[0]
    row = lambda width: pl.BlockSpec((tm, width), lambda i: (i, 0))
    col = lambda width: pl.BlockSpec((width, tm), lambda i: (0, i))
    tab = pl.BlockSpec((tm, LANES), lambda i: (i % pos_tiles, 0))
    f32 = lambda *shape: jax.ShapeDtypeStruct(shape, F32)
    specs = [row(S_WIDTH), row(R_IN), row(A_KVW), row(A_KVW)]
    shapes = [f32(m, S_WIDTH), f32(m, R_IN), f32(m, A_KVW), f32(m, A_KVW)]
    if attn_layouts:
        specs += [col(A_QW), pl.BlockSpec((A_KV_HEADS, tm, A_HEAD), lambda i: (0, i, 0)), col(A_KVW)]
        shapes += [f32(A_QW, m), f32(A_KV_HEADS, m, A_HEAD), f32(A_KVW, m)]
    else:
        specs += [row(A_QW)]
        shapes += [f32(m, A_QW)]
    if pre_ln:
        specs += [row(D_MODEL)]
        shapes += [f32(m, D_MODEL)]
    return pl.pallas_call(
        functools.partial(_in_proj_kernel, pre_ln, attn_layouts),
        grid=(m // tm,),
        in_specs=[row(D_MODEL), _full((1, D_MODEL)), _full((1, D_MODEL)), _resident((D_MODEL, N_REST)),
                  tab, tab, tab],
        out_specs=specs,
        out_shape=shapes,
        compiler_params=_cparams("parallel"),
        name="in_proj",
    )(x, ln_g, ln_b, w, *rope)


def _rope_tables(pos):
    half = ROT_DIM // 2
    inv = ROPE_THETA ** (-jnp.arange(half, dtype=F32) / half)
    ang = pos.astype(F32)[:, None] * inv[None, :]
    cos, sin = jnp.cos(ang), jnp.sin(ang)
    n = pos.shape[0]
    pad = jnp.zeros((n, A_HEAD - ROT_DIM), F32)
    zero = jnp.zeros((n, half), F32)
    cos_h = jnp.concatenate([cos, cos, pad + 1.0], -1)
    sa_h = jnp.concatenate([-sin, zero, pad], -1)
    sb_h = jnp.concatenate([zero, sin, pad], -1)
    rep = LANES // A_HEAD
    return tuple(jnp.tile(t, (1, rep)) for t in (cos_h, sa_h, sb_h))


def _s5_params(a_re, a_im, log_dt, b_re, b_im, c_re, c_im):
    dt = jnp.exp(log_dt)[:, None]
    lam_re, lam_im = a_re * dt, a_im * dt

    def power(k):
        mag = jnp.exp(lam_re * k)
        return (mag * jnp.cos(lam_im * k)).reshape(-1), (mag * jnp.sin(lam_im * k)).reshape(-1)

    abar_re, abar_im = power(1.0)
    den = (a_re * a_re + a_im * a_im).reshape(-1)
    ar, ai = a_re.reshape(-1), a_im.reshape(-1)
    em_re = abar_re - 1.0
    coef = jnp.stack([(em_re * ar + abar_im * ai) / den, (abar_im * ar - em_re * ai) / den])
    pw = jnp.stack([jnp.stack(x) for x in zip(*[power(float(k)) for k in range(1, SUBLANES + 1)])])
    rows = jnp.arange(SUBLANES)[:, None]
    dbl = jnp.stack([jnp.stack([jnp.where(rows >= d, part[None, :], 0.0) for part in power(float(d))])
                     for d in (1, 2, 4)], axis=1)
    eye = jnp.eye(S_GROUPS, dtype=F32)
    wb = jnp.concatenate([jnp.einsum("gpc,gh->gchp", b, eye).reshape(S_WIDTH, S_LANES)
                          for b in (b_re, b_im)], axis=1)
    wc = jnp.concatenate([jnp.einsum("gcp,gh->gphc", c, eye).reshape(S_LANES, S_WIDTH)
                          for c in (c_re, -c_im)], axis=0)
    return dict(coef=coef, pw=pw, dbl=dbl, abar=jnp.stack([abar_re, abar_im]),
                wb=wb.astype(BF16), wc=wc.astype(BF16))


def _s5_glu(s_re, s_im, u, wc_ref, d_ref, wg_ref, bg_ref):
    s = jnp.concatenate([s_re, s_im], axis=1)
    y = _bdot(s, wc_ref[...]) + d_ref[...] * u
    z = jax.nn.gelu(y)
    return z * jax.nn.sigmoid(_bdot(z, wg_ref[...]) + bg_ref[...])


def _s5_kernel(u_ref, wb_ref, wc_ref, coef_ref, pw_ref, dbl_ref, d_ref, wg_ref, bg_ref, s0_ref,
               y_ref, sfin_ref, s_scr, carry_scr):
    j = pl.program_id(1)
    t = u_ref.shape[0]

    @pl.when(j == 0)
    def _():
        carry_scr[...] = s0_ref[0]

    u = u_ref[...]
    bu = _bdot(u, wb_ref[...])
    b_re, b_im = bu[:, :S_LANES], bu[:, S_LANES:]
    c_re, c_im = coef_ref[0:1, :], coef_ref[1:2, :]
    s_scr[0] = c_re * b_re - c_im * b_im
    s_scr[1] = c_re * b_im + c_im * b_re

    def body(g, carry):
        cr, ci = carry
        off = pl.multiple_of(g * SUBLANES, SUBLANES)
        sr = s_scr[0, pl.ds(off, SUBLANES), :]
        si = s_scr[1, pl.ds(off, SUBLANES), :]
        for n, d in enumerate((1, 2, 4)):
            ar, ai = dbl_ref[0, n], dbl_ref[1, n]
            pr, pi = pltpu.roll(sr, d, 0), pltpu.roll(si, d, 0)
            sr, si = sr + ar * pr - ai * pi, si + ar * pi + ai * pr
        p_re, p_im = pw_ref[0], pw_ref[1]
        sr, si = sr + p_re * cr - p_im * ci, si + p_re * ci + p_im * cr
        s_scr[0, pl.ds(off, SUBLANES), :] = sr
        s_scr[1, pl.ds(off, SUBLANES), :] = si
        return sr[SUBLANES - 1:SUBLANES, :], si[SUBLANES - 1:SUBLANES, :]

    cr, ci = lax.fori_loop(0, t // SUBLANES, body, (carry_scr[0:1, :], carry_scr[1:2, :]), unroll=2)
    carry_scr[0:1, :] = cr
    carry_scr[1:2, :] = ci
    y_ref[...] = _s5_glu(s_scr[0], s_scr[1], u, wc_ref, d_ref, wg_ref, bg_ref)

    @pl.when(j == pl.num_programs(1) - 1)
    def _():
        sfin_ref[0] = carry_scr[...]


def _s5_prompt(u, sp, d, w_glu, b_glu, s0, nb, seq, t):
    nt = seq // t
    return pl.pallas_call(
        _s5_kernel,
        grid=(nb, nt),
        in_specs=[pl.BlockSpec((t, S_WIDTH), lambda b, j: (b * nt + j, 0)),
                  _full((S_WIDTH, 2 * S_LANES)), _full((2 * S_LANES, S_WIDTH)), _full((2, S_LANES)),
                  _full((2, SUBLANES, S_LANES)), _full((2, 3, SUBLANES, S_LANES)), _full((1, S_WIDTH)),
                  _full((S_WIDTH, S_WIDTH)), _full((1, S_WIDTH)),
                  pl.BlockSpec((1, 2, S_LANES), lambda b, j: (b, 0, 0))],
        out_specs=[pl.BlockSpec((t, S_WIDTH), lambda b, j: (b * nt + j, 0)),
                   pl.BlockSpec((1, 2, S_LANES), lambda b, j: (b, 0, 0))],
        out_shape=[jax.ShapeDtypeStruct((nb * seq, S_WIDTH), F32),
                   jax.ShapeDtypeStruct((nb, 2, S_LANES), F32)],
        scratch_shapes=[pltpu.VMEM((2, t, S_LANES), F32), pltpu.VMEM((2, S_LANES), F32)],
        compiler_params=_cparams("parallel", "arbitrary"),
        name="s5_scan",
    )(u, sp["wb"], sp["wc"], sp["coef"], sp["pw"], sp["dbl"], d, w_glu, b_glu, s0)


def _s5_step_kernel(u_ref, wb_ref, wc_ref, coef_ref, abar_ref, d_ref, wg_ref, bg_ref, s0_ref,
                    y_ref, s_ref):
    u = u_ref[...]
    bu = _bdot(u, wb_ref[...])
    b_re, b_im = bu[:, :S_LANES], bu[:, S_LANES:]
    c_re, c_im = coef_ref[0:1, :], coef_ref[1:2, :]
    a_re, a_im = abar_ref[0:1, :], abar_ref[1:2, :]
    s0_re, s0_im = s0_ref[:, :S_LANES], s0_ref[:, S_LANES:]
    s_re = c_re * b_re - c_im * b_im + (a_re * s0_re - a_im * s0_im)
    s_im = c_re * b_im + c_im * b_re + (a_re * s0_im + a_im * s0_re)
    s_ref[:, :S_LANES] = s_re
    s_ref[:, S_LANES:] = s_im
    y_ref[...] = _s5_glu(s_re, s_im, u, wc_ref, d_ref, wg_ref, bg_ref)


def _s5_step(u, sp, d, w_glu, b_glu, s0):
    n = u.shape[0]
    return pl.pallas_call(
        _s5_step_kernel,
        out_shape=[jax.ShapeDtypeStruct((n, S_WIDTH), F32), jax.ShapeDtypeStruct((n, 2 * S_LANES), F32)],
        compiler_params=_cparams(),
        name="s5_step",
    )(u, sp["wb"], sp["wc"], sp["coef"], sp["abar"], d, w_glu, b_glu, s0)


def _rwkv_token_math(c, prev, mu_ref, w0_ref, w2_ref, a0_ref, a2_ref, g2_ref, kk_ref, ka_ref, rk_ref):
    cf = c + (prev - c) * mu_ref[...]
    r = cf[:, :R_WIDTH]
    k = cf[:, R_WIDTH:2 * R_WIDTH]
    v = cf[:, 2 * R_WIDTH:3 * R_WIDTH]
    lora = cf[:, 3 * R_WIDTH:]
    w_log = -jax.nn.softplus(-(w0_ref[...] + _bdot(jnp.tanh(lora), w2_ref[...]))) - 0.5
    a = jax.nn.sigmoid(a0_ref[...] + _bdot(lora, a2_ref[...]))
    g = _bdot(jax.nn.sigmoid(lora), g2_ref[...])
    kk = k * kk_ref[...]
    kk = kk * lax.rsqrt(jnp.maximum(_head_sum(kk * kk, R_HEAD), 1e-24))
    k = k * (1.0 + (a - 1.0) * ka_ref[...])
    bonus = _head_sum(r * k * rk_ref[...], R_HEAD) * v
    return r, -jnp.exp(w_log), k, v, -kk, kk * a, g, bonus


def _rwkv_pre_kernel(c_ref, sh0_ref, *refs):
    weights, outs = refs[:9], refs[9:]
    for o_ref, val in zip(outs, _rwkv_token_math(c_ref[...], sh0_ref[...], *weights)):
        o_ref[...] = val


def _rwkv_weights(mu, w0, w2, a0, a2, g2, k_k, k_a, r_k):
    z = lambda n: jnp.zeros((n, R_WIDTH), F32)
    w2p = jnp.concatenate([w2, z(A_LORA + G_LORA)], 0).astype(BF16)
    a2p = jnp.concatenate([z(W_LORA), a2, z(G_LORA)], 0).astype(BF16)
    g2p = jnp.concatenate([z(W_LORA + A_LORA), g2], 0).astype(BF16)
    row = lambda x: x.reshape(1, -1)
    return [row(mu), row(w0), w2p, row(a0), a2p, g2p, row(k_k), row(k_a), row(r_k)]


def _rwkv_pre_step(c, sh0, wts):
    m = c.shape[0]
    return pl.pallas_call(
        _rwkv_pre_kernel,
        out_shape=[jax.ShapeDtypeStruct((m, R_WIDTH), F32)] * 8,
        compiler_params=_cparams(),
        name="rwkv_pre_step",
    )(c, sh0, *wts)


def _rwkv_chunk_kernel(c_ref, prev_ref, sh0_ref, *refs):
    weights = refs[:9]
    y1_ref, y2_ref, g_ref, f_ref, gate_ref, bonus_ref = refs[9:]
    c = c_ref[...]
    last = jnp.where(pl.program_id(1) == 0, sh0_ref[0], prev_ref[SUBLANES - 1:SUBLANES, :])
    crow = lax.broadcasted_iota(jnp.int32, c.shape, 0)
    prev = jnp.where(crow == 0, last, pltpu.roll(c, 1, 0))
    r_all, lw_all, k_all, v_all, a_all, b_all, gate, bonus = _rwkv_token_math(c, prev, *weights)
    gate_ref[...] = gate
    bonus_ref[...] = bonus
    t = RWKV_CHUNK
    n_chunks = c.shape[0] // t
    row = lax.broadcasted_iota(jnp.int32, (t, R_WIDTH), 0)
    ri = lax.broadcasted_iota(jnp.int32, (t, t), 0)
    ci = lax.broadcasted_iota(jnp.int32, (t, t), 1)
    hi = lax.broadcasted_iota(jnp.int32, (R_HEAD, R_HEAD), 0)
    hj = lax.broadcasted_iota(jnp.int32, (R_HEAD, R_HEAD), 1)
    per_head = lambda x: [x[:, h * R_HEAD:(h + 1) * R_HEAD] for h in range(R_HEADS)]
    ah, rh, bh, kh, vh, be_t, ke_t, ee = ([] for _ in range(8))
    for i in range(n_chunks):
        rows = slice(i * t, (i + 1) * t)
        lw = lw_all[rows, :]
        cw = lw
        d = 1
        while d < t:
            cw = cw + jnp.where(row >= d, pltpu.roll(cw, d, 0), 0.0)
            d *= 2
        cw_end = cw[t - 1:t, :]
        e_neg = jnp.exp(-cw)
        e_rem = jnp.exp(cw_end - cw)
        ah += per_head(a_all[rows, :] * jnp.exp(cw - lw))
        rh += per_head(r_all[rows, :] * jnp.exp(cw))
        bh += per_head(b_all[rows, :] * e_neg)
        kh += per_head(k_all[rows, :] * e_neg)
        vh += per_head(v_all[rows, :])
        be_t += [x.T for x in per_head(b_all[rows, :] * e_rem)]
        ke_t += [x.T for x in per_head(k_all[rows, :] * e_rem)]
        ee += per_head(jnp.exp(cw_end))
    units = range(n_chunks * R_HEADS)
    pr = lax.broadcasted_iota(jnp.int32, (2 * t, 2 * t), 0)
    pc = lax.broadcasted_iota(jnp.int32, (2 * t, 2 * t), 1)
    keep = ((pr < t) & (pr > pc % t)) | ((pr >= t) & (pr - t >= pc % t))
    quad = [jnp.where(keep, _dot3(jnp.concatenate([ah[u], rh[u]], 0),
                                  jnp.concatenate([kh[u], bh[u]], 0), _NT), 0.0) for u in units]
    kv = [_dot3(quad[u][:, :t], vh[u]) for u in units]
    left = lax.broadcasted_iota(jnp.int32, (t, 2 * t), 1) < t
    eye2 = jnp.where(lax.broadcasted_iota(jnp.int32, (t, 2 * t), 0)
                     == lax.broadcasted_iota(jnp.int32, (t, 2 * t), 1), 1.0, 0.0)
    ip = [jnp.where(left, eye2, quad[u][:t, :]) for u in units]
    n = 1
    while n < t:
        ip = [jnp.where(left, ip[u], 0.0) + _dot3(ip[u][:, t:], ip[u]) for u in units]
        n *= 2
    w = [_dot3(ip[u][:, :t], jnp.concatenate([ah[u], kv[u][:t, :]], 1)) for u in units]
    yy = [_dot3(quad[u][t:, t:], w[u]) for u in units]
    gf = [_dot3(be_t[u], w[u]) for u in units]
    kev = [_dot3(ke_t[u], vh[u]) for u in units]
    for u in units:
        i, h = divmod(u, R_HEADS)
        y1_ref[i, h] = rh[u] + yy[u][:, :R_HEAD]
        y2_ref[i, h] = yy[u][:, R_HEAD:] + kv[u][t:, :]
        g_ref[i, h] = jnp.where(hi == hj, ee[u], 0.0) + gf[u][:, :R_HEAD]
        f_ref[i, h] = gf[u][:, R_HEAD:] + kev[u]


RWKV_CHUNKS_PER_STEP = 4


def _rwkv_chunks(c, sh0, wts, nb, seq):
    t = RWKV_CHUNK
    per = RWKV_CHUNKS_PER_STEP if (seq // t) % RWKV_CHUNKS_PER_STEP == 0 else 1
    rows = per * t
    nt = seq // rows
    nc = nb * seq // t
    per8 = rows // SUBLANES
    tile = lambda b, j: (b * nt + j, 0)
    yspec = pl.BlockSpec((per, R_HEADS, t, R_HEAD), lambda b, j: (b * nt + j, 0, 0, 0))
    gspec = pl.BlockSpec((per, R_HEADS, R_HEAD, R_HEAD), lambda b, j: (b * nt + j, 0, 0, 0))
    row = pl.BlockSpec((rows, R_WIDTH), tile)
    f32 = lambda *shape: jax.ShapeDtypeStruct(shape, F32)
    return pl.pallas_call(
        _rwkv_chunk_kernel,
        grid=(nb, nt),
        in_specs=[pl.BlockSpec((rows, R_IN), tile),
                  pl.BlockSpec((SUBLANES, R_IN), lambda b, j: (jnp.maximum((b * nt + j) * per8 - 1, 0), 0)),
                  pl.BlockSpec((1, 1, R_IN), lambda b, j: (b, 0, 0))] + [_full(w.shape) for w in wts],
        out_specs=[yspec, yspec, gspec, gspec, row, row],
        out_shape=[f32(nc, R_HEADS, t, R_HEAD)] * 2 + [f32(nc, R_HEADS, R_HEAD, R_HEAD)] * 2
        + [f32(nb * seq, R_WIDTH)] * 2,
        compiler_params=_cparams("parallel", "parallel"),
        name="rwkv_chunk",
    )(c, c, sh0.reshape(nb, 1, R_IN), *wts)


def _rwkv_post(y, bonus, g, lng, lnb):
    m = _head_sum(y, R_HEAD) * (1.0 / R_HEAD)
    yc = y - m
    var = _head_sum(yc * yc, R_HEAD) * (1.0 / R_HEAD)
    return (yc * lax.rsqrt(var + GN_EPS) * lng + lnb + bonus) * g


def _rwkv_scan_kernel(y1_ref, y2_ref, g_ref, f_ref, bonus_ref, gate_ref, lng_ref, lnb_ref, h0_ref,
                      y_ref, hfin_ref, h_scr, y_scr):
    j = pl.program_id(1)

    @pl.when(j == 0)
    def _():
        h_scr[...] = h0_ref[0]

    t = RWKV_CHUNK
    heads = range(R_HEADS)
    states = [h_scr[h] for h in heads]
    for c in range(y1_ref.shape[0]):
        for h in heads:
            y_scr[c * t:(c + 1) * t, h * R_HEAD:(h + 1) * R_HEAD] = (
                _dot3(y1_ref[c, h], states[h]) + y2_ref[c, h])
        states = [_dot3(g_ref[c, h], states[h]) + f_ref[c, h] for h in heads]
    for h in heads:
        h_scr[h] = states[h]
    y_ref[...] = _rwkv_post(y_scr[...], bonus_ref[...], gate_ref[...], lng_ref[...], lnb_ref[...])

    @pl.when(j == pl.num_programs(1) - 1)
    def _():
        hfin_ref[0] = h_scr[...]


RWKV_SCAN_CHUNKS = 4


def _rwkv_scan(y1, y2, g, f, bonus, gate, lng, lnb, h0, nb, seq, t):
    per = RWKV_SCAN_CHUNKS if (seq // t) % RWKV_SCAN_CHUNKS == 0 else 1
    nc = seq // (t * per)
    yspec = pl.BlockSpec((per, R_HEADS, t, R_HEAD), lambda b, j: (b * nc + j, 0, 0, 0))
    gspec = pl.BlockSpec((per, R_HEADS, R_HEAD, R_HEAD), lambda b, j: (b * nc + j, 0, 0, 0))
    row = pl.BlockSpec((per * t, R_WIDTH), lambda b, j: (b * nc + j, 0))
    hspec = pl.BlockSpec((1, R_HEADS, R_HEAD, R_HEAD), lambda b, j: (b, 0, 0, 0))
    return pl.pallas_call(
        _rwkv_scan_kernel,
        grid=(nb, nc),
        in_specs=[yspec, yspec, gspec, gspec, row, row, _full((1, R_WIDTH)), _full((1, R_WIDTH)), hspec],
        out_specs=[row, hspec],
        out_shape=[jax.ShapeDtypeStruct((nb * seq, R_WIDTH), F32),
                   jax.ShapeDtypeStruct((nb, R_HEADS, R_HEAD, R_HEAD), F32)],
        scratch_shapes=[pltpu.VMEM((R_HEADS, R_HEAD, R_HEAD), F32), pltpu.VMEM((per * t, R_WIDTH), F32)],
        compiler_params=_cparams("parallel", "arbitrary"),
        name="rwkv_scan",
    )(y1, y2, g, f, bonus, gate, lng, lnb, h0)


RWKV_STEP_ROWS = 8


def _rwkv_step_kernel(r_ref, lw_ref, k_ref, v_ref, a_ref, b_ref, s_ref, y_ref, so_ref):
    ii = lax.broadcasted_iota(jnp.int32, (R_HEAD, R_HEAD), 0)
    jj = lax.broadcasted_iota(jnp.int32, (R_HEAD, R_HEAD), 1)
    eye = ii == jj
    for n in range(RWKV_STEP_ROWS):
        one = lambda ref: ref[n:n + 1, :]
        s = s_ref[n]
        sa = jnp.sum(s * one(a_ref), -1, keepdims=True)
        v_col = jnp.sum(jnp.where(eye, one(v_ref), 0.0), -1, keepdims=True)
        s_new = s * jnp.exp(one(lw_ref)) + sa * one(b_ref) + v_col * one(k_ref)
        y_col = jnp.sum(s_new * one(r_ref), -1, keepdims=True)
        so_ref[n] = s_new
        y_ref[n:n + 1, :] = jnp.sum(jnp.where(eye, y_col, 0.0), 0, keepdims=True)


def _rwkv_step(vecs, s0):
    n = s0.shape[0]
    vspec = pl.BlockSpec((RWKV_STEP_ROWS, R_HEAD), lambda i: (i, 0))
    sspec = pl.BlockSpec((RWKV_STEP_ROWS, R_HEAD, R_HEAD), lambda i: (i, 0, 0))
    return pl.pallas_call(
        _rwkv_step_kernel,
        grid=(n // RWKV_STEP_ROWS,),
        in_specs=[vspec] * 6 + [sspec],
        out_specs=[vspec, sspec],
        out_shape=[jax.ShapeDtypeStruct((n, R_HEAD), F32), jax.ShapeDtypeStruct((n, R_HEAD, R_HEAD), F32)],
        compiler_params=_cparams("parallel"),
        name="rwkv_step",
    )(*vecs, s0)


def _rwkv_post_kernel(y_ref, bonus_ref, gate_ref, lng_ref, lnb_ref, o_ref):
    o_ref[...] = _rwkv_post(y_ref[...], bonus_ref[...], gate_ref[...], lng_ref[...], lnb_ref[...])


def _rwkv_post_call(y, bonus, gate, lng, lnb):
    return pl.pallas_call(
        _rwkv_post_kernel,
        out_shape=jax.ShapeDtypeStruct(y.shape, F32),
        compiler_params=_cparams(),
        name="rwkv_post_step",
    )(y, bonus, gate, lng, lnb)


def _moba_kernel(qt_ref, k_ref, vt_ref, o_ref, km_scr, bias_scr, s_own, s_even, s_odd, p_odd):
    qi = pl.program_id(2)
    nblk = km_scr.shape[0]
    cols = A_GROUP * MOBA_BLOCK

    @pl.when(qi == 0)
    def _():
        km_scr[...] = jnp.mean(k_ref[0].reshape(nblk, MOBA_BLOCK, A_HEAD), axis=1)

    qt = jnp.concatenate([qt_ref[g * A_HEAD:(g + 1) * A_HEAD, :] for g in range(A_GROUP)], axis=1)
    gate_q = (qt * ATT_SCALE).astype(BF16)
    qb = (qt * (ATT_SCALE * LOG2_E)).astype(BF16)

    def scores(n):
        off = pl.multiple_of(n * MOBA_BLOCK, MOBA_BLOCK)
        return _bdot(k_ref[0, pl.ds(off, MOBA_BLOCK), :], qb)

    s_own[...] = scores(qi)
    s_even[...] = scores(0)
    blk = lax.broadcasted_iota(jnp.int32, (nblk, cols), 0)
    gate = jnp.where(blk < qi, _bdot(km_scr[...], gate_q), NEG_INF)
    sel = jnp.zeros(gate.shape, jnp.bool_)
    for _ in range(MOBA_TOPK):
        m = jnp.max(gate, 0, keepdims=True)
        idx = jnp.min(jnp.where(gate == m, blk, nblk), 0, keepdims=True)
        pick = (blk == idx) & (m > NEG_INF)
        sel = sel | pick
        gate = jnp.where(pick, NEG_INF, gate)
    bias_scr[...] = jnp.where(sel, 0.0, NEG_INF)

    ones_rows = jnp.ones((MOBA_ROWS, MOBA_BLOCK), BF16)

    def values_t(n):
        off = pl.multiple_of(n * MOBA_BLOCK, MOBA_BLOCK)
        return jnp.concatenate([vt_ref[:, pl.ds(off, MOBA_BLOCK)].astype(BF16), ones_rows], axis=0)

    groups = MOBA_BLOCK // MOBA_ROWS

    def softmax_step(s, m, elem_bias=None, col_bias=None):
        def group(r):
            sr = s[r * MOBA_ROWS:(r + 1) * MOBA_ROWS, :]
            return sr if elem_bias is None else sr + elem_bias(r)

        peak = group(0)
        for r in range(1, groups):
            peak = jnp.maximum(peak, group(r))
        top = jnp.max(peak, 0, keepdims=True)
        if col_bias is not None:
            top = top + col_bias
        m_new = jnp.maximum(m, top)
        alpha = jnp.exp2(m - m_new)
        shift = m_new if col_bias is None else m_new - col_bias
        p = jnp.concatenate([jnp.exp2(group(r) - shift).astype(BF16) for r in range(groups)], axis=0)
        return m_new, alpha, p

    p_odd[...] = jnp.zeros(p_odd.shape, BF16)

    def body(j, carry):
        m, acc = carry
        n0 = 2 * j
        pv_prev = _bdot(values_t(jnp.maximum(n0 - 1, 0)), p_odd[...])
        s_odd[...] = scores(n0 + 1)
        m, alpha, p_even = softmax_step(s_even, m, col_bias=bias_scr[pl.ds(n0, 1), :])
        acc = alpha * (acc + pv_prev) + _bdot(values_t(n0), p_even)
        s_even[...] = scores(jnp.minimum(n0 + 2, nblk - 1))
        m, alpha, p = softmax_step(s_odd, m, col_bias=bias_scr[pl.ds(n0 + 1, 1), :])
        p_odd[...] = p
        return m, alpha * acc

    trips = lax.shift_right_logical(qi + 1, 1)
    init = (jnp.full((1, cols), MAX_FLOOR, F32), jnp.zeros((A_HEAD + MOBA_ROWS, cols), F32))
    m, acc = lax.fori_loop(0, trips, body, init)
    pv_prev = _bdot(values_t(jnp.maximum(2 * trips - 1, 0)), p_odd[...])
    key = lax.broadcasted_iota(jnp.int32, (MOBA_ROWS, cols), 0)
    pos = lax.broadcasted_iota(jnp.int32, (MOBA_ROWS, cols), 1) % MOBA_BLOCK
    causal = lambda r: jnp.where(key + r * MOBA_ROWS <= pos, 0.0, NEG_INF)
    _, alpha, p = softmax_step(s_own, m, elem_bias=causal)
    acc = alpha * (acc + pv_prev) + _bdot(values_t(qi), p)
    o = acc[:A_HEAD, :] / acc[A_HEAD:A_HEAD + 1, :]
    for g in range(A_GROUP):
        o_ref[g * A_HEAD:(g + 1) * A_HEAD, :] = o[:, g * MOBA_BLOCK:(g + 1) * MOBA_BLOCK]


def _moba_prompt(qt, kh, vt, nb, seq):
    nblk = seq // MOBA_BLOCK
    qspec = pl.BlockSpec((A_GROUP * A_HEAD, MOBA_BLOCK), lambda b, h, i: (h, b * nblk + i))
    return pl.pallas_call(
        _moba_kernel,
        grid=(nb, A_KV_HEADS, nblk),
        in_specs=[qspec,
                  pl.BlockSpec((1, seq, A_HEAD), lambda b, h, i: (h, b, 0)),
                  pl.BlockSpec((A_HEAD, seq), lambda b, h, i: (h, b))],
        out_specs=qspec,
        out_shape=jax.ShapeDtypeStruct(qt.shape, F32),
        scratch_shapes=[pltpu.VMEM((nblk, A_HEAD), F32), pltpu.VMEM((nblk, A_GROUP * MOBA_BLOCK), F32),
                        pltpu.VMEM((MOBA_BLOCK, A_GROUP * MOBA_BLOCK), F32),
                        pltpu.VMEM((MOBA_BLOCK, A_GROUP * MOBA_BLOCK), F32),
                        pltpu.VMEM((MOBA_BLOCK, A_GROUP * MOBA_BLOCK), F32),
                        pltpu.VMEM((MOBA_BLOCK, A_GROUP * MOBA_BLOCK), BF16)],
        compiler_params=_cparams("parallel", "parallel", "arbitrary"),
        name="moba_prompt",
    )(qt, kh, vt)


def _pages_t(cache):
    return cache.transpose(0, 1, 3, 4, 2).reshape(-1, A_KVW, PAGE_SIZE)


def _kmeans_kernel(n_in, pt_ref, *refs):
    pages, o_ref = refs[:n_in], refs[n_in]
    j = pl.program_id(1)
    per_blk = MOBA_BLOCK // PAGE_SIZE
    blocks = n_in // per_blk

    @pl.when(j == 0)
    def _():
        o_ref[...] = jnp.zeros(o_ref.shape, F32)

    out = o_ref[0]
    lane = lax.broadcasted_iota(jnp.int32, out.shape, 1)
    for i in range(blocks):
        tot = sum(pages[per_blk * i + p][0] for p in range(per_blk))
        mean = jnp.sum(tot, axis=1, keepdims=True) * (1.0 / MOBA_BLOCK)
        out = jnp.where(lane == j * blocks + i, mean, out)
    o_ref[0] = out


DEC_KMEANS_PAGES = 32


def _decode_kmeans(pages, page_table, layer, n_pool):
    nb, n_pages = page_table.shape
    n_in = min(DEC_KMEANS_PAGES, n_pages)
    steps = n_pages // n_in
    per_blk = MOBA_BLOCK // PAGE_SIZE

    def page_spec(i):
        return pl.BlockSpec((1, A_KVW, PAGE_SIZE),
                            lambda b, j, pt: (layer * n_pool + pt[b * n_pages + j * n_in + i], 0, 0))

    return pl.pallas_call(
        functools.partial(_kmeans_kernel, n_in),
        grid_spec=pltpu.PrefetchScalarGridSpec(
            num_scalar_prefetch=1,
            grid=(nb, steps),
            in_specs=[page_spec(i) for i in range(n_in)],
            out_specs=pl.BlockSpec((1, A_KVW, n_pages // per_blk), lambda b, j, pt: (b, 0, 0)),
        ),
        out_shape=jax.ShapeDtypeStruct((nb, A_KVW, n_pages // per_blk), F32),
        compiler_params=_cparams("parallel", "arbitrary"),
        name="decode_kmeans",
    )(page_table.reshape(-1), *([pages] * n_in))


def _decode_select_kernel(q_ref, km_ref, o_ref):
    nblk = km_ref.shape[2]
    gate = _bdot(q_ref[0], km_ref[0])
    col = lax.broadcasted_iota(jnp.int32, gate.shape, 1)
    lane = lax.broadcasted_iota(jnp.int32, (A_HEADS, LANES), 1)
    out = jnp.zeros((A_HEADS, LANES), jnp.int32)
    for r in range(MOBA_TOPK):
        m = jnp.max(gate, -1, keepdims=True)
        idx = jnp.min(jnp.where(gate == m, col, nblk), -1, keepdims=True)
        out = jnp.where(lane == r, idx, out)
        gate = jnp.where(col == idx, NEG_INF, gate)
    o_ref[0] = out


def _decode_select(q_exp, kmeans):
    nb, _, nblk = kmeans.shape
    return pl.pallas_call(
        _decode_select_kernel,
        grid=(nb,),
        in_specs=[pl.BlockSpec((1, A_HEADS, A_KVW), lambda b: (b, 0, 0)),
                  pl.BlockSpec((1, A_KVW, nblk), lambda b: (b, 0, 0))],
        out_specs=pl.BlockSpec((1, A_HEADS, LANES), lambda b: (b, 0, 0)),
        out_shape=jax.ShapeDtypeStruct((nb, A_HEADS, LANES), jnp.int32),
        compiler_params=_cparams("parallel"),
        name="decode_select",
    )(q_exp, kmeans)


DEC_PAGES = MOBA_TOPK * (MOBA_BLOCK // PAGE_SIZE)


def _decode_attn_kernel(pg_ref, q_ref, kn_ref, vn_ref, *refs):
    k_pages, v_pages, o_ref = refs[:DEC_PAGES], refs[DEC_PAGES:2 * DEC_PAGES], refs[2 * DEC_PAGES]
    h = pl.program_id(1)
    q = q_ref[0, pl.ds(h, 1), :].astype(BF16)
    rb = lambda x: x.astype(BF16).astype(F32)
    s_self = jnp.sum(rb(q) * rb(kn_ref[0]), -1, keepdims=True) * ATT_SCALE
    scores = [_bdot(q, kp[0]) * ATT_SCALE for kp in k_pages]
    m = s_self
    for s in scores:
        m = jnp.maximum(m, jnp.max(s, -1, keepdims=True))
    p_self = jnp.exp(s_self - m)
    l = p_self
    acc = rb(p_self) * rb(vn_ref[0])
    for s, vp in zip(scores, v_pages):
        p = jnp.exp(s - m)
        l = l + jnp.sum(p, -1, keepdims=True)
        acc = acc + _bdot(p, vp[0], _NT)
    o_ref[0, pl.ds(h, 1), :] = acc / l


def _decode_attn(q_exp, k_new, v_new, kp, vp, page_ids, layer, n_pool):
    nb = q_exp.shape[0]

    def page_spec(i):
        return pl.BlockSpec(
            (1, A_KVW, PAGE_SIZE),
            lambda b, h, pg: (layer * n_pool + pg[(b * A_HEADS + h) * DEC_PAGES + i], 0, 0))

    bspec = pl.BlockSpec((1, A_HEADS, A_KVW), lambda b, h, pg: (b, 0, 0))
    nspec = pl.BlockSpec((1, 1, A_KVW), lambda b, h, pg: (b, 0, 0))
    return pl.pallas_call(
        _decode_attn_kernel,
        grid_spec=pltpu.PrefetchScalarGridSpec(
            num_scalar_prefetch=1,
            grid=(nb, A_HEADS),
            in_specs=[bspec, nspec, nspec] + [page_spec(i) for i in range(DEC_PAGES)] * 2,
            out_specs=bspec,
        ),
        out_shape=jax.ShapeDtypeStruct((nb, A_HEADS, A_KVW), F32),
        compiler_params=_cparams("parallel", "arbitrary"),
        name="decode_attn",
    )(page_ids.reshape(-1), q_exp, k_new.reshape(nb, 1, A_KVW), v_new.reshape(nb, 1, A_KVW),
      *([kp] * DEC_PAGES), *([vp] * DEC_PAGES))


def _moba_decode(q, k_new, v_new, k_pages, v_pages, n_pool, page_table, layer):
    nb = q.shape[0]
    onehot = (jnp.arange(A_KV_HEADS)[None, :] == (jnp.arange(A_HEADS) // A_GROUP)[:, None]).astype(F32)
    q_exp = (q.reshape(nb, A_HEADS, 1, A_HEAD) * onehot[None, :, :, None]).reshape(nb, A_HEADS, A_KVW)
    kmeans = _decode_kmeans(k_pages, page_table, layer, n_pool)
    blk = _decode_select(q_exp, kmeans)[:, :, :MOBA_TOPK]
    per_blk = MOBA_BLOCK // PAGE_SIZE
    pages_of = (blk[..., None] * per_blk + jnp.arange(per_blk)).reshape(nb, A_HEADS * DEC_PAGES)
    page_ids = jnp.take_along_axis(page_table, pages_of, axis=1)
    o = _decode_attn(q_exp, k_new, v_new, k_pages, v_pages, page_ids, layer, n_pool)
    o = o.reshape(nb, A_HEADS, A_KV_HEADS, A_HEAD)
    return jnp.einsum("bhkd,hk->bhd", o, onehot).reshape(nb, A_QW)


def _merge_kernel(ya_transposed, x_ref, ys_ref, yr_ref, ya_ref, wg_ref, ps_ref, pr_ref, pa_ref, wo_ref,
                  g_ref, b_ref, o_ref):
    x = x_ref[...]
    xb = x.astype(BF16)
    ya = ya_ref[...].T if ya_transposed else ya_ref[...]
    merged = None
    for i, (y, p_ref) in enumerate(((ys_ref[...], ps_ref), (yr_ref[...], pr_ref), (ya, pa_ref))):
        gate = jax.nn.sigmoid(_bdot(xb, wg_ref[:, i * D_MODEL:(i + 1) * D_MODEL]))
        term = gate * _bdot(y, p_ref[...])
        merged = term if merged is None else merged + term
    o_ref[...] = _ln(ALPHA * x + _bdot(merged, wo_ref[...]), g_ref[...], b_ref[...])


def _merge(x, ys, yr, ya, wg, ps, pr, pa, wo, g, b, tm, ya_transposed):
    m = x.shape[0]
    row = lambda width: pl.BlockSpec((tm, width), lambda i: (i, 0))
    ya_spec = pl.BlockSpec((A_QW, tm), lambda i: (0, i)) if ya_transposed else row(A_QW)
    return pl.pallas_call(
        functools.partial(_merge_kernel, ya_transposed),
        grid=(m // tm,),
        in_specs=[row(D_MODEL), row(S_WIDTH), row(R_WIDTH), ya_spec,
                  _resident(wg.shape), _resident(ps.shape), _resident(pr.shape), _resident(pa.shape),
                  _resident(wo.shape),
                  _full((1, D_MODEL)), _full((1, D_MODEL))],
        out_specs=row(D_MODEL),
        out_shape=jax.ShapeDtypeStruct((m, D_MODEL), F32),
        compiler_params=_cparams("parallel"),
        name="merge",
    )(x, ys, yr, ya, wg, ps, pr, pa, wo, g, b)


FFN_COLS = D_FF // 2


def _ffn_kernel(tiles_per_seq, x_ref, halo_ref, wup_ref, cw_ref, cb_ref, wdn_ref, g_ref, b_ref, o_ref):
    i = pl.program_id(0)
    x = x_ref[...]
    tm = x.shape[0]
    xe = jnp.concatenate([halo_ref[...], x], axis=0).astype(BF16)
    row = lax.broadcasted_iota(jnp.int32, (tm + SUBLANES, 1), 0)
    keep = jnp.logical_or(row >= SUBLANES, i % tiles_per_seq != 0)
    acc = jnp.zeros((tm, D_MODEL), F32)
    for c in range(D_FF // FFN_COLS):
        halves = []
        for off in (c * FFN_COLS, D_FF + c * FFN_COLS):
            up = jnp.where(keep, _bdot(xe, wup_ref[:, off:off + FFN_COLS]), 0.0)
            cv = cb_ref[:, off:off + FFN_COLS]
            for j in range(CONV_W):
                lo = SUBLANES - (CONV_W - 1) + j
                cv = cv + cw_ref[j:j + 1, off:off + FFN_COLS] * up[lo:lo + tm, :]
            halves.append(cv)
        hmid = jax.nn.gelu(halves[0]) * halves[1]
        acc = acc + _bdot(hmid, wdn_ref[c * FFN_COLS:(c + 1) * FFN_COLS, :])
    o_ref[...] = _ln(ALPHA * x + acc, g_ref[...], b_ref[...])


def _ffn(x, wup, cw, cb, wdn, g, b, seq, tm):
    m = x.shape[0]
    per8 = tm // SUBLANES
    return pl.pallas_call(
        functools.partial(_ffn_kernel, seq // tm),
        grid=(m // tm,),
        in_specs=[pl.BlockSpec((tm, D_MODEL), lambda i: (i, 0)),
                  pl.BlockSpec((SUBLANES, D_MODEL), lambda i: (jnp.maximum(i * per8 - 1, 0), 0)),
                  _resident(wup.shape), _full(cw.shape), _full(cb.shape), _resident(wdn.shape),
                  _full((1, D_MODEL)), _full((1, D_MODEL))],
        out_specs=pl.BlockSpec((tm, D_MODEL), lambda i: (i, 0)),
        out_shape=jax.ShapeDtypeStruct((m, D_MODEL), F32),
        compiler_params=_cparams("parallel"),
        name="ffn",
    )(x, x, wup, cw, cb, wdn, g, b)


def _ffn_step_kernel(x_ref, c0a_ref, c0b_ref, wup_ref, cw_ref, cb_ref, wdn_ref, g_ref, b_ref, o_ref, up_ref):
    x = x_ref[...]
    up = _bdot(x, wup_ref[...])
    up_ref[...] = up
    cv = cb_ref[...] + cw_ref[0:1, :] * c0a_ref[...] + cw_ref[1:2, :] * c0b_ref[...] + cw_ref[2:3, :] * up
    hmid = jax.nn.gelu(cv[:, :D_FF]) * cv[:, D_FF:]
    o_ref[...] = _ln(ALPHA * x + _bdot(hmid, wdn_ref[...]), g_ref[...], b_ref[...])


def _ffn_step(x, c0a, c0b, wup, cw, cb, wdn, g, b):
    n = x.shape[0]
    return pl.pallas_call(
        _ffn_step_kernel,
        out_shape=[jax.ShapeDtypeStruct((n, D_MODEL), F32), jax.ShapeDtypeStruct((n, 2 * D_FF), F32)],
        compiler_params=_cparams(),
        name="ffn_step",
    )(x, c0a, c0b, wup, cw, cb, wdn, g, b)


def _up_rows_kernel(x_ref, w_ref, o_ref):
    o_ref[...] = _bdot(x_ref[...], w_ref[...])


def _up_rows(x, w):
    return pl.pallas_call(
        _up_rows_kernel,
        out_shape=jax.ShapeDtypeStruct((x.shape[0], w.shape[1]), F32),
        compiler_params=_cparams(),
        name="ffn_up_tail",
    )(x, w)


def _layer_params(l, P):
    bf = lambda x: x.astype(BF16)
    row = lambda x: x.reshape(1, -1)
    w_in = P["w_in"][l]
    return dict(
        w_gate=bf(w_in[:, :O_SSM]), w_rest=bf(w_in[:, O_SSM:]),
        s5=_s5_params(P["ssm_a_re"][l], P["ssm_a_im"][l], P["ssm_log_dt"][l], P["ssm_b_re"][l],
                      P["ssm_b_im"][l], P["ssm_c_re"][l], P["ssm_c_im"][l]),
        ssm_d=row(P["ssm_d"][l]), w_glu=bf(P["ssm_w_glu"][l]), b_glu=row(P["ssm_b_glu"][l]),
        rwkv=_rwkv_weights(P["rwkv_mu"][l], P["rwkv_w0"][l], P["rwkv_w2"][l], P["rwkv_a0"][l],
                           P["rwkv_a2"][l], P["rwkv_g2"][l], P["rwkv_k_k"][l], P["rwkv_k_a"][l],
                           P["rwkv_r_k"][l]),
        lnx_g=row(P["rwkv_lnx_g"][l]), lnx_b=row(P["rwkv_lnx_b"][l]),
        proj_ssm=bf(P["proj_ssm"][l]), proj_rwkv=bf(P["proj_rwkv"][l]), proj_attn=bf(P["proj_attn"][l]),
        w_o=bf(P["w_o"][l]), ln1_g=row(P["ln1_g"][l]), ln1_b=row(P["ln1_b"][l]),
        w_up=bf(P["ffn_w_up"][l]), conv_w=P["ffn_conv_w"][l], conv_b=row(P["ffn_conv_b"][l]),
        w_down=bf(P["ffn_w_down"][l]), ln2_g=row(P["ln2_g"][l]), ln2_b=row(P["ln2_b"][l]),
    )


def _pick_tile(n, pref):
    t = min(pref, n)
    while n % t:
        t //= 2
    return t


def _prompt_trunk(x_prompt, P, layers):
    nb, seq, _ = x_prompt.shape
    m = nb * seq
    x = x_prompt.reshape(m, D_MODEL)
    tm = _pick_tile(seq, 512)
    rope = _rope_tables(jnp.arange(seq, dtype=jnp.int32))
    ln_g, ln_b = P["ln_in_g"].reshape(1, -1), P["ln_in_b"].reshape(1, -1)
    outs = []
    for l, lp in enumerate(layers):
        res = _in_proj(x, ln_g, ln_b, lp["w_rest"], rope, tm, seq // tm, pre_ln=(l == 0), attn_layouts=True)
        u, c, k, v, qt, kh, vt = res[:7]
        if l == 0:
            x = res[7]
        y_s, s_fin = _s5_prompt(u, lp["s5"], lp["ssm_d"], lp["w_glu"], lp["b_glu"],
                                jnp.zeros((nb, 2, S_LANES), F32), nb, seq, _pick_tile(seq, 256))
        y1, y2, g, f, gate, bonus = _rwkv_chunks(c, jnp.zeros((nb, R_IN), F32), lp["rwkv"], nb, seq)
        y_r, h_fin = _rwkv_scan(y1, y2, g, f, bonus, gate, lp["lnx_g"], lp["lnx_b"],
                                jnp.zeros((nb, R_HEADS, R_HEAD, R_HEAD), F32), nb, seq, RWKV_CHUNK)
        y_at = _moba_prompt(qt, kh, vt, nb, seq)
        x = _merge(x, y_s, y_r, y_at, lp["w_gate"], lp["proj_ssm"], lp["proj_rwkv"], lp["proj_attn"],
                   lp["w_o"], lp["ln1_g"], lp["ln1_b"], _pick_tile(seq, 512), ya_transposed=True)
        tail = x.reshape(nb, seq, D_MODEL)[:, seq - (CONV_W - 1):].reshape(nb * (CONV_W - 1), D_MODEL)
        conv = _up_rows(tail, lp["w_up"]).reshape(nb, CONV_W - 1, 2 * D_FF)
        x = _ffn(x, lp["w_up"], lp["conv_w"], lp["conv_b"], lp["w_down"], lp["ln2_g"], lp["ln2_b"],
                 seq, _pick_tile(seq, 512))
        outs.append(dict(
            k=k.reshape(nb, seq, A_KV_HEADS, A_HEAD), v=v.reshape(nb, seq, A_KV_HEADS, A_HEAD),
            s_re=s_fin[:, 0].reshape(nb, S_GROUPS, S_STATE), s_im=s_fin[:, 1].reshape(nb, S_GROUPS, S_STATE),
            rwkv=jnp.swapaxes(h_fin, -1, -2), shift=c.reshape(nb, seq, R_IN)[:, -1], conv=conv))
    return x.reshape(nb, seq, D_MODEL), outs


def _sample_trunk(x_sample, P, layers, cache_k, cache_v, page_table, st_re, st_im, st_rwkv, st_shift, st_conv):
    nb = x_sample.shape[0]
    x = x_sample.reshape(nb, D_MODEL)
    past_len = page_table.shape[1] * PAGE_SIZE
    rope = _rope_tables(jnp.full((nb,), past_len, jnp.int32))
    ln_g, ln_b = P["ln_in_g"].reshape(1, -1), P["ln_in_b"].reshape(1, -1)
    k_pages, v_pages = _pages_t(cache_k), _pages_t(cache_v)
    outs = []
    for l, lp in enumerate(layers):
        res = _in_proj(x, ln_g, ln_b, lp["w_rest"], rope, nb, 1, pre_ln=(l == 0), attn_layouts=False)
        u, c, k, v, q = res[:5]
        if l == 0:
            x = res[5]
        s0 = jnp.concatenate([st_re[l].reshape(nb, S_LANES), st_im[l].reshape(nb, S_LANES)], axis=1)
        y_s, s_new = _s5_step(u, lp["s5"], lp["ssm_d"], lp["w_glu"], lp["b_glu"], s0)
        r, lw, k2, vv, a, b, gate, bonus = _rwkv_pre_step(c, st_shift[l], lp["rwkv"])
        per_head = lambda t: t.reshape(nb * R_HEADS, R_HEAD)
        y_rows, s_rwkv = _rwkv_step([per_head(t) for t in (r, lw, k2, vv, a, b)],
                                    st_rwkv[l].reshape(nb * R_HEADS, R_HEAD, R_HEAD))
        y_r = _rwkv_post_call(y_rows.reshape(nb, R_WIDTH), bonus, gate, lp["lnx_g"], lp["lnx_b"])
        y_a = _moba_decode(q, k, v, k_pages, v_pages, cache_k.shape[1], page_table, l)
        x = _merge(x, y_s, y_r, y_a, lp["w_gate"], lp["proj_ssm"], lp["proj_rwkv"], lp["proj_attn"],
                   lp["w_o"], lp["ln1_g"], lp["ln1_b"], nb, ya_transposed=False)
        x, up = _ffn_step(x, st_conv[l][:, 0], st_conv[l][:, 1], lp["w_up"], lp["conv_w"], lp["conv_b"],
                          lp["w_down"], lp["ln2_g"], lp["ln2_b"])
        outs.append(dict(
            k=k.reshape(nb, 1, A_KV_HEADS, A_HEAD), v=v.reshape(nb, 1, A_KV_HEADS, A_HEAD),
            s_re=s_new[:, :S_LANES].reshape(nb, S_GROUPS, S_STATE),
            s_im=s_new[:, S_LANES:].reshape(nb, S_GROUPS, S_STATE),
            rwkv=s_rwkv.reshape(nb, R_HEADS, R_HEAD, R_HEAD), shift=c,
            conv=jnp.stack([st_conv[l][:, 1], up], axis=1)))
    return x.reshape(nb, 1, D_MODEL), outs


def kernel(x_prompt, x_sample, cache_k, cache_v, page_table, state_ssm_re, state_ssm_im, state_rwkv, state_rwkv_shift, state_conv, ln_in_g, ln_in_b, w_in, ssm_a_re, ssm_a_im, ssm_log_dt, ssm_b_re, ssm_b_im, ssm_c_re, ssm_c_im, ssm_d, ssm_w_glu, ssm_b_glu, rwkv_mu, rwkv_w0, rwkv_w2, rwkv_a0, rwkv_a2, rwkv_g2, rwkv_k_k, rwkv_k_a, rwkv_r_k, rwkv_lnx_g, rwkv_lnx_b, proj_ssm, proj_rwkv, proj_attn, w_o, ln1_g, ln1_b, ffn_w_up, ffn_conv_w, ffn_conv_b, ffn_w_down, ln2_g, ln2_b):
    P = dict(ln_in_g=ln_in_g, ln_in_b=ln_in_b, w_in=w_in,
             ssm_a_re=ssm_a_re, ssm_a_im=ssm_a_im, ssm_log_dt=ssm_log_dt,
             ssm_b_re=ssm_b_re, ssm_b_im=ssm_b_im, ssm_c_re=ssm_c_re, ssm_c_im=ssm_c_im,
             ssm_d=ssm_d, ssm_w_glu=ssm_w_glu, ssm_b_glu=ssm_b_glu,
             rwkv_mu=rwkv_mu, rwkv_w0=rwkv_w0, rwkv_w2=rwkv_w2, rwkv_a0=rwkv_a0, rwkv_a2=rwkv_a2,
             rwkv_g2=rwkv_g2, rwkv_k_k=rwkv_k_k, rwkv_k_a=rwkv_k_a, rwkv_r_k=rwkv_r_k,
             rwkv_lnx_g=rwkv_lnx_g, rwkv_lnx_b=rwkv_lnx_b,
             proj_ssm=proj_ssm, proj_rwkv=proj_rwkv, proj_attn=proj_attn, w_o=w_o,
             ln1_g=ln1_g, ln1_b=ln1_b, ffn_w_up=ffn_w_up, ffn_conv_w=ffn_conv_w,
             ffn_conv_b=ffn_conv_b, ffn_w_down=ffn_w_down, ln2_g=ln2_g, ln2_b=ln2_b)
    layers = [_layer_params(l, P) for l in range(w_in.shape[0])]
    y_p, op = _prompt_trunk(x_prompt, P, layers)
    y_s, os_ = _sample_trunk(x_sample, P, layers, cache_k, cache_v, page_table, state_ssm_re, state_ssm_im,
                             state_rwkv, state_rwkv_shift, state_conv)
    st = lambda outs, key: jnp.stack([o[key] for o in outs])
    return (y_p, y_s, st(op, "k"), st(op, "v"), st(os_, "k"), st(os_, "v"),
            st(op, "s_re"), st(op, "s_im"), st(os_, "s_re"), st(os_, "s_im"),
            st(op, "rwkv"), st(os_, "rwkv"), st(op, "shift"), st(os_, "shift"),
            st(op, "conv"), st(os_, "conv"))
```

```python
import functools
import math

import jax
import jax.numpy as jnp
from jax import lax
from jax.experimental import pallas as pl
from jax.experimental.pallas import tpu as pltpu

F32 = jnp.float32
BF16 = jnp.bfloat16

D_MODEL = 1024
PAGE_SIZE = 128
S_GROUP = 16
S_GROUPS = 16
S_STATE = 64
S_WIDTH = S_GROUPS * S_GROUP
S_LANES = S_GROUPS * S_STATE
R_HEAD = 64
R_HEADS = 4
R_WIDTH = R_HEADS * R_HEAD
W_LORA = 32
A_LORA = 32
G_LORA = 64
N_LORA = W_LORA + A_LORA + G_LORA
R_IN = 3 * R_WIDTH + N_LORA
GN_EPS = 64e-5
A_HEAD = 64
A_HEADS = 8
A_KV_HEADS = 4
A_GROUP = A_HEADS // A_KV_HEADS
A_QW = A_HEADS * A_HEAD
A_KVW = A_KV_HEADS * A_HEAD
ROT_DIM = A_HEAD // 4
ROPE_THETA = 500000.0
MOBA_BLOCK = 256
MOBA_TOPK = 3
N_BRANCH = 3
O_SSM = N_BRANCH * D_MODEL
N_REST = S_WIDTH + R_IN + A_QW + 2 * A_KVW
D_FF = 2816
CONV_W = 3
DEPTH = 2
ALPHA = (2 * DEPTH) ** 0.25
LN_EPS = 1e-5
ATT_SCALE = A_HEAD ** -0.5
LOG2_E = math.log2(math.e)

VMEM_LIMIT_BYTES = 56 * 1024 * 1024
LANES = 128
SUBLANES = 8

MOBA_ROWS = 16
RWKV_CHUNK = 64
NEG_INF = float("-inf")
MAX_FLOOR = -1e30

_NN = (((1,), (0,)), ((), ()))
_NT = (((1,), (1,)), ((), ()))
_TN = (((0,), (0,)), ((), ()))


def _cparams(*sem):
    return pltpu.CompilerParams(dimension_semantics=sem or None, vmem_limit_bytes=VMEM_LIMIT_BYTES)


def _bdot(a, b, dims=_NN):
    return lax.dot_general(a.astype(BF16), b.astype(BF16), dims, preferred_element_type=F32)


def _dot3(a, b, dims=_NN):
    ah = a.astype(BF16)
    al = (a - ah.astype(F32)).astype(BF16)
    bh = b.astype(BF16)
    bl = (b - bh.astype(F32)).astype(BF16)
    f = lambda x, y: lax.dot_general(x, y, dims, preferred_element_type=F32)
    return f(ah, bh) + f(ah, bl) + f(al, bh)


def _ln(x, g, b):
    mu = jnp.mean(x, -1, keepdims=True)
    xc = x - mu
    var = jnp.mean(xc * xc, -1, keepdims=True)
    return xc * lax.rsqrt(var + LN_EPS) * g + b


def _head_sum(x, head):
    lane_head = lax.broadcasted_iota(jnp.int32, (1, x.shape[1]), 1) // head
    out = jnp.zeros_like(x)
    for h in range(x.shape[1] // head):
        m = lane_head == h
        s = jnp.sum(jnp.where(m, x, 0.0), -1, keepdims=True)
        out = jnp.where(m, s, out)
    return out


def _full(shape):
    n = len(shape)
    return pl.BlockSpec(shape, lambda *_: (0,) * n)


def _resident(shape):
    n = len(shape)
    return pl.BlockSpec(shape, lambda *_: (0,) * n, pipeline_mode=pl.Buffered(1))


def _in_proj_kernel(pre_ln, attn_layouts, x_ref, g_ref, b_ref, w_ref, cos_ref, sa_ref, sb_ref,
                    u_ref, c_ref, k_ref, v_ref, *rest):
    x = x_ref[...]
    if pre_ln:
        x = _ln(x, g_ref[...], b_ref[...])
        rest[-1][...] = x
    h = _bdot(x, w_ref[...])
    u_ref[...] = h[:, :S_WIDTH]
    c_ref[...] = h[:, S_WIDTH:S_WIDTH + R_IN]
    cos, sa, sb = cos_ref[...], sa_ref[...], sb_ref[...]
    base = S_WIDTH + R_IN
    nq = A_QW // LANES
    per = LANES // A_HEAD
    for j in range((A_QW + A_KVW) // LANES):
        ch = h[:, base + LANES * j: base + LANES * (j + 1)]
        rot = (ch * cos + pltpu.roll(ch, LANES - ROT_DIM // 2, 1) * sa
               + pltpu.roll(ch, ROT_DIM // 2, 1) * sb)
        if j < nq:
            if attn_layouts:
                rest[0][LANES * j:LANES * (j + 1), :] = rot.T
            else:
                rest[0][:, LANES * j:LANES * (j + 1)] = rot
        else:
            k_ref[:, LANES * (j - nq):LANES * (j - nq + 1)] = rot
            if attn_layouts:
                for i in range(per):
                    rest[1][per * (j - nq) + i] = rot[:, A_HEAD * i:A_HEAD * (i + 1)]
    v = h[:, base + A_QW + A_KVW:]
    v_ref[...] = v
    if attn_layouts:
        for j in range(A_KVW // LANES):
            rest[2][LANES * j:LANES * (j + 1), :] = v[:, LANES * j:LANES * (j + 1)].T


def _in_proj(x, ln_g, ln_b, w, rope, tm, pos_tiles, pre_ln, attn_layouts):
    m = x.shape[0]
    row = lambda width: pl.BlockSpec((tm, width), lambda i: (i, 0))
    col = lambda width: pl.BlockSpec((width, tm), lambda i: (0, i))
    tab = pl.BlockSpec((tm, LANES), lambda i: (i % pos_tiles, 0))
    f32 = lambda *shape: jax.ShapeDtypeStruct(shape, F32)
    specs = [row(S_WIDTH), row(R_IN), row(A_KVW), row(A_KVW)]
    shapes = [f32(m, S_WIDTH), f32(m, R_IN), f32(m, A_KVW), f32(m, A_KVW)]
    if attn_layouts:
        specs += [col(A_QW), pl.BlockSpec((A_KV_HEADS, tm, A_HEAD), lambda i: (0, i, 0)), col(A_KVW)]
        shapes += [f32(A_QW, m), f32(A_KV_HEADS, m, A_HEAD), f32(A_KVW, m)]
    else:
        specs += [row(A_QW)]
        shapes += [f32(m, A_QW)]
    if pre_ln:
        specs += [row(D_MODEL)]
        shapes += [f32(m, D_MODEL)]
    return pl.pallas_call(
        functools.partial(_in_proj_kernel, pre_ln, attn_layouts),
        grid=(m // tm,),
        in_specs=[row(D_MODEL), _full((1, D_MODEL)), _full((1, D_MODEL)), _resident((D_MODEL, N_REST)),
                  tab, tab, tab],
        out_specs=specs,
        out_shape=shapes,
        compiler_params=_cparams("parallel"),
        name="in_proj",
    )(x, ln_g, ln_b, w, *rope)


def _rope_tables(pos):
    half = ROT_DIM // 2
    inv = ROPE_THETA ** (-jnp.arange(half, dtype=F32) / half)
    ang = pos.astype(F32)[:, None] * inv[None, :]
    cos, sin = jnp.cos(ang), jnp.sin(ang)
    n = pos.shape[0]
    pad = jnp.zeros((n, A_HEAD - ROT_DIM), F32)
    zero = jnp.zeros((n, half), F32)
    cos_h = jnp.concatenate([cos, cos, pad + 1.0], -1)
    sa_h = jnp.concatenate([-sin, zero, pad], -1)
    sb_h = jnp.concatenate([zero, sin, pad], -1)
    rep = LANES // A_HEAD
    return tuple(jnp.tile(t, (1, rep)) for t in (cos_h, sa_h, sb_h))


def _s5_params(a_re, a_im, log_dt, b_re, b_im, c_re, c_im):
    dt = jnp.exp(log_dt)[:, None]
    lam_re, lam_im = a_re * dt, a_im * dt

    def power(k):
        mag = jnp.exp(lam_re * k)
        return (mag * jnp.cos(lam_im * k)).reshape(-1), (mag * jnp.sin(lam_im * k)).reshape(-1)

    abar_re, abar_im = power(1.0)
    den = (a_re * a_re + a_im * a_im).reshape(-1)
    ar, ai = a_re.reshape(-1), a_im.reshape(-1)
    em_re = abar_re - 1.0
    coef = jnp.stack([(em_re * ar + abar_im * ai) / den, (abar_im * ar - em_re * ai) / den])
    pw = jnp.stack([jnp.stack(x) for x in zip(*[power(float(k)) for k in range(1, SUBLANES + 1)])])
    rows = jnp.arange(SUBLANES)[:, None]
    dbl = jnp.stack([jnp.stack([jnp.where(rows >= d, part[None, :], 0.0) for part in power(float(d))])
                     for d in (1, 2, 4)], axis=1)
    eye = jnp.eye(S_GROUPS, dtype=F32)
    wb = jnp.concatenate([jnp.einsum("gpc,gh->gchp", b, eye).reshape(S_WIDTH, S_LANES)
                          for b in (b_re, b_im)], axis=1)
    wc = jnp.concatenate([jnp.einsum("gcp,gh->gphc", c, eye).reshape(S_LANES, S_WIDTH)
                          for c in (c_re, -c_im)], axis=0)
    return dict(coef=coef, pw=pw, dbl=dbl, abar=jnp.stack([abar_re, abar_im]),
                wb=wb.astype(BF16), wc=wc.astype(BF16))


def _s5_glu(s_re, s_im, u, wc_ref, d_ref, wg_ref, bg_ref):
    s = jnp.concatenate([s_re, s_im], axis=1)
    y = _bdot(s, wc_ref[...]) + d_ref[...] * u
    z = jax.nn.gelu(y)
    return z * jax.nn.sigmoid(_bdot(z, wg_ref[...]) + bg_ref[...])


def _s5_kernel(u_ref, wb_ref, wc_ref, coef_ref, pw_ref, dbl_ref, d_ref, wg_ref, bg_ref, s0_ref,
               y_ref, sfin_ref, s_scr, carry_scr):
    j = pl.program_id(1)
    t = u_ref.shape[0]

    @pl.when(j == 0)
    def _():
        carry_scr[...] = s0_ref[0]

    u = u_ref[...]
    bu = _bdot(u, wb_ref[...])
    b_re, b_im = bu[:, :S_LANES], bu[:, S_LANES:]
    c_re, c_im = coef_ref[0:1, :], coef_ref[1:2, :]
    s_scr[0] = c_re * b_re - c_im * b_im
    s_scr[1] = c_re * b_im + c_im * b_re

    def body(g, carry):
        cr, ci = carry
        off = pl.multiple_of(g * SUBLANES, SUBLANES)
        sr = s_scr[0, pl.ds(off, SUBLANES), :]
        si = s_scr[1, pl.ds(off, SUBLANES), :]
        for n, d in enumerate((1, 2, 4)):
            ar, ai = dbl_ref[0, n], dbl_ref[1, n]
            pr, pi = pltpu.roll(sr, d, 0), pltpu.roll(si, d, 0)
            sr, si = sr + ar * pr - ai * pi, si + ar * pi + ai * pr
        p_re, p_im = pw_ref[0], pw_ref[1]
        sr, si = sr + p_re * cr - p_im * ci, si + p_re * ci + p_im * cr
        s_scr[0, pl.ds(off, SUBLANES), :] = sr
        s_scr[1, pl.ds(off, SUBLANES), :] = si
        return sr[SUBLANES - 1:SUBLANES, :], si[SUBLANES - 1:SUBLANES, :]

    cr, ci = lax.fori_loop(0, t // SUBLANES, body, (carry_scr[0:1, :], carry_scr[1:2, :]), unroll=2)
    carry_scr[0:1, :] = cr
    carry_scr[1:2, :] = ci
    y_ref[...] = _s5_glu(s_scr[0], s_scr[1], u, wc_ref, d_ref, wg_ref, bg_ref)

    @pl.when(j == pl.num_programs(1) - 1)
    def _():
        sfin_ref[0] = carry_scr[...]


def _s5_prompt(u, sp, d, w_glu, b_glu, s0, nb, seq, t):
    nt = seq // t
    return pl.pallas_call(
        _s5_kernel,
        grid=(nb, nt),
        in_specs=[pl.BlockSpec((t, S_WIDTH), lambda b, j: (b * nt + j, 0)),
                  _full((S_WIDTH, 2 * S_LANES)), _full((2 * S_LANES, S_WIDTH)), _full((2, S_LANES)),
                  _full((2, SUBLANES, S_LANES)), _full((2, 3, SUBLANES, S_LANES)), _full((1, S_WIDTH)),
                  _full((S_WIDTH, S_WIDTH)), _full((1, S_WIDTH)),
                  pl.BlockSpec((1, 2, S_LANES), lambda b, j: (b, 0, 0))],
        out_specs=[pl.BlockSpec((t, S_WIDTH), lambda b, j: (b * nt + j, 0)),
                   pl.BlockSpec((1, 2, S_LANES), lambda b, j: (b, 0, 0))],
        out_shape=[jax.ShapeDtypeStruct((nb * seq, S_WIDTH), F32),
                   jax.ShapeDtypeStruct((nb, 2, S_LANES), F32)],
        scratch_shapes=[pltpu.VMEM((2, t, S_LANES), F32), pltpu.VMEM((2, S_LANES), F32)],
        compiler_params=_cparams("parallel", "arbitrary"),
        name="s5_scan",
    )(u, sp["wb"], sp["wc"], sp["coef"], sp["pw"], sp["dbl"], d, w_glu, b_glu, s0)


def _s5_step_kernel(u_ref, wb_ref, wc_ref, coef_ref, abar_ref, d_ref, wg_ref, bg_ref, s0_ref,
                    y_ref, s_ref):
    u = u_ref[...]
    bu = _bdot(u, wb_ref[...])
    b_re, b_im = bu[:, :S_LANES], bu[:, S_LANES:]
    c_re, c_im = coef_ref[0:1, :], coef_ref[1:2, :]
    a_re, a_im = abar_ref[0:1, :], abar_ref[1:2, :]
    s0_re, s0_im = s0_ref[:, :S_LANES], s0_ref[:, S_LANES:]
    s_re = c_re * b_re - c_im * b_im + (a_re * s0_re - a_im * s0_im)
    s_im = c_re * b_im + c_im * b_re + (a_re * s0_im + a_im * s0_re)
    s_ref[:, :S_LANES] = s_re
    s_ref[:, S_LANES:] = s_im
    y_ref[...] = _s5_glu(s_re, s_im, u, wc_ref, d_ref, wg_ref, bg_ref)


def _s5_step(u, sp, d, w_glu, b_glu, s0):
    n = u.shape[0]
    return pl.pallas_call(
        _s5_step_kernel,
        out_shape=[jax.ShapeDtypeStruct((n, S_WIDTH), F32), jax.ShapeDtypeStruct((n, 2 * S_LANES), F32)],
        compiler_params=_cparams(),
        name="s5_step",
    )(u, sp["wb"], sp["wc"], sp["coef"], sp["abar"], d, w_glu, b_glu, s0)


def _rwkv_token_math(c, prev, mu_ref, w0_ref, w2_ref, a0_ref, a2_ref, g2_ref, kk_ref, ka_ref, rk_ref):
    cf = c + (prev - c) * mu_ref[...]
    r = cf[:, :R_WIDTH]
    k = cf[:, R_WIDTH:2 * R_WIDTH]
    v = cf[:, 2 * R_WIDTH:3 * R_WIDTH]
    lora = cf[:, 3 * R_WIDTH:]
    w_log = -jax.nn.softplus(-(w0_ref[...] + _bdot(jnp.tanh(lora), w2_ref[...]))) - 0.5
    a = jax.nn.sigmoid(a0_ref[...] + _bdot(lora, a2_ref[...]))
    g = _bdot(jax.nn.sigmoid(lora), g2_ref[...])
    kk = k * kk_ref[...]
    kk = kk * lax.rsqrt(jnp.maximum(_head_sum(kk * kk, R_HEAD), 1e-24))
    k = k * (1.0 + (a - 1.0) * ka_ref[...])
    bonus = _head_sum(r * k * rk_ref[...], R_HEAD) * v
    return r, -jnp.exp(w_log), k, v, -kk, kk * a, g, bonus


def _rwkv_pre_kernel(c_ref, sh0_ref, *refs):
    weights, outs = refs[:9], refs[9:]
    for o_ref, val in zip(outs, _rwkv_token_math(c_ref[...], sh0_ref[...], *weights)):
        o_ref[...] = val


def _rwkv_weights(mu, w0, w2, a0, a2, g2, k_k, k_a, r_k):
    z = lambda n: jnp.zeros((n, R_WIDTH), F32)
    w2p = jnp.concatenate([w2, z(A_LORA + G_LORA)], 0).astype(BF16)
    a2p = jnp.concatenate([z(W_LORA), a2, z(G_LORA)], 0).astype(BF16)
    g2p = jnp.concatenate([z(W_LORA + A_LORA), g2], 0).astype(BF16)
    row = lambda x: x.reshape(1, -1)
    return [row(mu), row(w0), w2p, row(a0), a2p, g2p, row(k_k), row(k_a), row(r_k)]


def _rwkv_pre_step(c, sh0, wts):
    m = c.shape[0]
    return pl.pallas_call(
        _rwkv_pre_kernel,
        out_shape=[jax.ShapeDtypeStruct((m, R_WIDTH), F32)] * 8,
        compiler_params=_cparams(),
        name="rwkv_pre_step",
    )(c, sh0, *wts)


def _rwkv_chunk_kernel(c_ref, prev_ref, sh0_ref, *refs):
    weights = refs[:9]
    y1_ref, y2_ref, g_ref, f_ref, gate_ref, bonus_ref = refs[9:]
    c = c_ref[...]
    last = jnp.where(pl.program_id(1) == 0, sh0_ref[0], prev_ref[SUBLANES - 1:SUBLANES, :])
    crow = lax.broadcasted_iota(jnp.int32, c.shape, 0)
    prev = jnp.where(crow == 0, last, pltpu.roll(c, 1, 0))
    r_all, lw_all, k_all, v_all, a_all, b_all, gate, bonus = _rwkv_token_math(c, prev, *weights)
    gate_ref[...] = gate
    bonus_ref[...] = bonus
    t = RWKV_CHUNK
    n_chunks = c.shape[0] // t
    row = lax.broadcasted_iota(jnp.int32, (t, R_WIDTH), 0)
    ri = lax.broadcasted_iota(jnp.int32, (t, t), 0)
    ci = lax.broadcasted_iota(jnp.int32, (t, t), 1)
    hi = lax.broadcasted_iota(jnp.int32, (R_HEAD, R_HEAD), 0)
    hj = lax.broadcasted_iota(jnp.int32, (R_HEAD, R_HEAD), 1)
    per_head = lambda x: [x[:, h * R_HEAD:(h + 1) * R_HEAD] for h in range(R_HEADS)]
    ah, rh, bh, kh, vh, be_t, ke_t, ee = ([] for _ in range(8))
    for i in range(n_chunks):
        rows = slice(i * t, (i + 1) * t)
        lw = lw_all[rows, :]
        cw = lw
        d = 1
        while d < t:
            cw = cw + jnp.where(row >= d, pltpu.roll(cw, d, 0), 0.0)
            d *= 2
        cw_end = cw[t - 1:t, :]
        e_neg = jnp.exp(-cw)
        e_rem = jnp.exp(cw_end - cw)
        ah += per_head(a_all[rows, :] * jnp.exp(cw - lw))
        rh += per_head(r_all[rows, :] * jnp.exp(cw))
        bh += per_head(b_all[rows, :] * e_neg)
        kh += per_head(k_all[rows, :] * e_neg)
        vh += per_head(v_all[rows, :])
        be_t += [x.T for x in per_head(b_all[rows, :] * e_rem)]
        ke_t += [x.T for x in per_head(k_all[rows, :] * e_rem)]
        ee += per_head(jnp.exp(cw_end))
    units = range(n_chunks * R_HEADS)
    pr = lax.broadcasted_iota(jnp.int32, (2 * t, 2 * t), 0)
    pc = lax.broadcasted_iota(jnp.int32, (2 * t, 2 * t), 1)
    keep = ((pr < t) & (pr > pc % t)) | ((pr >= t) & (pr - t >= pc % t))
    quad = [jnp.where(keep, _dot3(jnp.concatenate([ah[u], rh[u]], 0),
                                  jnp.concatenate([kh[u], bh[u]], 0), _NT), 0.0) for u in units]
    kv = [_dot3(quad[u][:, :t], vh[u]) for u in units]
    left = lax.broadcasted_iota(jnp.int32, (t, 2 * t), 1) < t
    eye2 = jnp.where(lax.broadcasted_iota(jnp.int32, (t, 2 * t), 0)
                     == lax.broadcasted_iota(jnp.int32, (t, 2 * t), 1), 1.0, 0.0)
    ip = [jnp.where(left, eye2, quad[u][:t, :]) for u in units]
    n = 1
    while n < t:
        ip = [jnp.where(left, ip[u], 0.0) + _dot3(ip[u][:, t:], ip[u]) for u in units]
        n *= 2
    w = [_dot3(ip[u][:, :t], jnp.concatenate([ah[u], kv[u][:t, :]], 1)) for u in units]
    yy = [_dot3(quad[u][t:, t:], w[u]) for u in units]
    gf = [_dot3(be_t[u], w[u]) for u in units]
    kev = [_dot3(ke_t[u], vh[u]) for u in units]
    for u in units:
        i, h = divmod(u, R_HEADS)
        y1_ref[i, h] = rh[u] + yy[u][:, :R_HEAD]
        y2_ref[i, h] = yy[u][:, R_HEAD:] + kv[u][t:, :]
        g_ref[i, h] = jnp.where(hi == hj, ee[u], 0.0) + gf[u][:, :R_HEAD]
        f_ref[i, h] = gf[u][:, R_HEAD:] + kev[u]


RWKV_CHUNKS_PER_STEP = 4


def _rwkv_chunks(c, sh0, wts, nb, seq):
    t = RWKV_CHUNK
    per = RWKV_CHUNKS_PER_STEP if (seq // t) % RWKV_CHUNKS_PER_STEP == 0 else 1
    rows = per * t
    nt = seq // rows
    nc = nb * seq // t
    per8 = rows // SUBLANES
    tile = lambda b, j: (b * nt + j, 0)
    yspec = pl.BlockSpec((per, R_HEADS, t, R_HEAD), lambda b, j: (b * nt + j, 0, 0, 0))
    gspec = pl.BlockSpec((per, R_HEADS, R_HEAD, R_HEAD), lambda b, j: (b * nt + j, 0, 0, 0))
    row = pl.BlockSpec((rows, R_WIDTH), tile)
    f32 = lambda *shape: jax.ShapeDtypeStruct(shape, F32)
    return pl.pallas_call(
        _rwkv_chunk_kernel,
        grid=(nb, nt),
        in_specs=[pl.BlockSpec((rows, R_IN), tile),
                  pl.BlockSpec((SUBLANES, R_IN), lambda b, j: (jnp.maximum((b * nt + j) * per8 - 1, 0), 0)),
                  pl.BlockSpec((1, 1, R_IN), lambda b, j: (b, 0, 0))] + [_full(w.shape) for w in wts],
        out_specs=[yspec, yspec, gspec, gspec, row, row],
        out_shape=[f32(nc, R_HEADS, t, R_HEAD)] * 2 + [f32(nc, R_HEADS, R_HEAD, R_HEAD)] * 2
        + [f32(nb * seq, R_WIDTH)] * 2,
        compiler_params=_cparams("parallel", "parallel"),
        name="rwkv_chunk",
    )(c, c, sh0.reshape(nb, 1, R_IN), *wts)


def _rwkv_post(y, bonus, g, lng, lnb):
    m = _head_sum(y, R_HEAD) * (1.0 / R_HEAD)
    yc = y - m
    var = _head_sum(yc * yc, R_HEAD) * (1.0 / R_HEAD)
    return (yc * lax.rsqrt(var + GN_EPS) * lng + lnb + bonus) * g


def _rwkv_scan_kernel(y1_ref, y2_ref, g_ref, f_ref, bonus_ref, gate_ref, lng_ref, lnb_ref, h0_ref,
                      y_ref, hfin_ref, h_scr, y_scr):
    j = pl.program_id(1)

    @pl.when(j == 0)
    def _():
        h_scr[...] = h0_ref[0]

    t = RWKV_CHUNK
    heads = range(R_HEADS)
    states = [h_scr[h] for h in heads]
    for c in range(y1_ref.shape[0]):
        for h in heads:
            y_scr[c * t:(c + 1) * t, h * R_HEAD:(h + 1) * R_HEAD] = (
                _dot3(y1_ref[c, h], states[h]) + y2_ref[c, h])
        states = [_dot3(g_ref[c, h], states[h]) + f_ref[c, h] for h in heads]
    for h in heads:
        h_scr[h] = states[h]
    y_ref[...] = _rwkv_post(y_scr[...], bonus_ref[...], gate_ref[...], lng_ref[...], lnb_ref[...])

    @pl.when(j == pl.num_programs(1) - 1)
    def _():
        hfin_ref[0] = h_scr[...]


RWKV_SCAN_CHUNKS = 8


def _rwkv_scan(y1, y2, g, f, bonus, gate, lng, lnb, h0, nb, seq, t):
    per = RWKV_SCAN_CHUNKS if (seq // t) % RWKV_SCAN_CHUNKS == 0 else 1
    nc = seq // (t * per)
    yspec = pl.BlockSpec((per, R_HEADS, t, R_HEAD), lambda b, j: (b * nc + j, 0, 0, 0))
    gspec = pl.BlockSpec((per, R_HEADS, R_HEAD, R_HEAD), lambda b, j: (b * nc + j, 0, 0, 0))
    row = pl.BlockSpec((per * t, R_WIDTH), lambda b, j: (b * nc + j, 0))
    hspec = pl.BlockSpec((1, R_HEADS, R_HEAD, R_HEAD), lambda b, j: (b, 0, 0, 0))
    return pl.pallas_call(
        _rwkv_scan_kernel,
        grid=(nb, nc),
        in_specs=[yspec, yspec, gspec, gspec, row, row, _full((1, R_WIDTH)), _full((1, R_WIDTH)), hspec],
        out_specs=[row, hspec],
        out_shape=[jax.ShapeDtypeStruct((nb * seq, R_WIDTH), F32),
                   jax.ShapeDtypeStruct((nb, R_HEADS, R_HEAD, R_HEAD), F32)],
        scratch_shapes=[pltpu.VMEM((R_HEADS, R_HEAD, R_HEAD), F32), pltpu.VMEM((per * t, R_WIDTH), F32)],
        compiler_params=_cparams("parallel", "arbitrary"),
        name="rwkv_scan",
    )(y1, y2, g, f, bonus, gate, lng, lnb, h0)


RWKV_STEP_ROWS = 8


def _rwkv_step_kernel(r_ref, lw_ref, k_ref, v_ref, a_ref, b_ref, s_ref, y_ref, so_ref):
    ii = lax.broadcasted_iota(jnp.int32, (R_HEAD, R_HEAD), 0)
    jj = lax.broadcasted_iota(jnp.int32, (R_HEAD, R_HEAD), 1)
    eye = ii == jj
    for n in range(RWKV_STEP_ROWS):
        one = lambda ref: ref[n:n + 1, :]
        s = s_ref[n]
        sa = jnp.sum(s * one(a_ref), -1, keepdims=True)
        v_col = jnp.sum(jnp.where(eye, one(v_ref), 0.0), -1, keepdims=True)
        s_new = s * jnp.exp(one(lw_ref)) + sa * one(b_ref) + v_col * one(k_ref)
        y_col = jnp.sum(s_new * one(r_ref), -1, keepdims=True)
        so_ref[n] = s_new
        y_ref[n:n + 1, :] = jnp.sum(jnp.where(eye, y_col, 0.0), 0, keepdims=True)


def _rwkv_step(vecs, s0):
    n = s0.shape[0]
    vspec = pl.BlockSpec((RWKV_STEP_ROWS, R_HEAD), lambda i: (i, 0))
    sspec = pl.BlockSpec((RWKV_STEP_ROWS, R_HEAD, R_HEAD), lambda i: (i, 0, 0))
    return pl.pallas_call(
        _rwkv_step_kernel,
        grid=(n // RWKV_STEP_ROWS,),
        in_specs=[vspec] * 6 + [sspec],
        out_specs=[vspec, sspec],
        out_shape=[jax.ShapeDtypeStruct((n, R_HEAD), F32), jax.ShapeDtypeStruct((n, R_HEAD, R_HEAD), F32)],
        compiler_params=_cparams("parallel"),
        name="rwkv_step",
    )(*vecs, s0)


def _rwkv_post_kernel(y_ref, bonus_ref, gate_ref, lng_ref, lnb_ref, o_ref):
    o_ref[...] = _rwkv_post(y_ref[...], bonus_ref[...], gate_ref[...], lng_ref[...], lnb_ref[...])


def _rwkv_post_call(y, bonus, gate, lng, lnb):
    return pl.pallas_call(
        _rwkv_post_kernel,
        out_shape=jax.ShapeDtypeStruct(y.shape, F32),
        compiler_params=_cparams(),
        name="rwkv_post_step",
    )(y, bonus, gate, lng, lnb)


def _moba_kernel(qt_ref, k_ref, vt_ref, o_ref, km_scr, bias_scr, s_own, s_even, s_odd, p_odd):
    qi = pl.program_id(2)
    nblk = km_scr.shape[0]
    cols = A_GROUP * MOBA_BLOCK

    @pl.when(qi == 0)
    def _():
        km_scr[...] = jnp.mean(k_ref[0].reshape(nblk, MOBA_BLOCK, A_HEAD), axis=1)

    qt = jnp.concatenate([qt_ref[g * A_HEAD:(g + 1) * A_HEAD, :] for g in range(A_GROUP)], axis=1)
    gate_q = (qt * ATT_SCALE).astype(BF16)
    qb = (qt * (ATT_SCALE * LOG2_E)).astype(BF16)

    def scores(n):
        off = pl.multiple_of(n * MOBA_BLOCK, MOBA_BLOCK)
        return _bdot(k_ref[0, pl.ds(off, MOBA_BLOCK), :], qb)

    s_own[...] = scores(qi)
    s_even[...] = scores(0)
    blk = lax.broadcasted_iota(jnp.int32, (nblk, cols), 0)
    gate = jnp.where(blk < qi, _bdot(km_scr[...], gate_q), NEG_INF)
    sel = jnp.zeros(gate.shape, jnp.bool_)
    for _ in range(MOBA_TOPK):
        m = jnp.max(gate, 0, keepdims=True)
        idx = jnp.min(jnp.where(gate == m, blk, nblk), 0, keepdims=True)
        pick = (blk == idx) & (m > NEG_INF)
        sel = sel | pick
        gate = jnp.where(pick, NEG_INF, gate)
    bias_scr[...] = jnp.where(sel, 0.0, NEG_INF)

    ones_rows = jnp.ones((MOBA_ROWS, MOBA_BLOCK), BF16)

    def values_t(n):
        off = pl.multiple_of(n * MOBA_BLOCK, MOBA_BLOCK)
        return jnp.concatenate([vt_ref[:, pl.ds(off, MOBA_BLOCK)].astype(BF16), ones_rows], axis=0)

    groups = MOBA_BLOCK // MOBA_ROWS

    def softmax_step(s, m, elem_bias=None, col_bias=None):
        def group(r):
            sr = s[r * MOBA_ROWS:(r + 1) * MOBA_ROWS, :]
            return sr if elem_bias is None else sr + elem_bias(r)

        peak = group(0)
        for r in range(1, groups):
            peak = jnp.maximum(peak, group(r))
        top = jnp.max(peak, 0, keepdims=True)
        if col_bias is not None:
            top = top + col_bias
        m_new = jnp.maximum(m, top)
        alpha = jnp.exp2(m - m_new)
        shift = m_new if col_bias is None else m_new - col_bias
        p = jnp.concatenate([jnp.exp2(group(r) - shift).astype(BF16) for r in range(groups)], axis=0)
        return m_new, alpha, p

    p_odd[...] = jnp.zeros(p_odd.shape, BF16)

    def body(j, carry):
        m, acc = carry
        n0 = 2 * j
        pv_prev = _bdot(values_t(jnp.maximum(n0 - 1, 0)), p_odd[...])
        s_odd[...] = scores(n0 + 1)
        m, alpha, p_even = softmax_step(s_even, m, col_bias=bias_scr[pl.ds(n0, 1), :])
        acc = alpha * (acc + pv_prev) + _bdot(values_t(n0), p_even)
        s_even[...] = scores(jnp.minimum(n0 + 2, nblk - 1))
        m, alpha, p = softmax_step(s_odd, m, col_bias=bias_scr[pl.ds(n0 + 1, 1), :])
        p_odd[...] = p
        return m, alpha * acc

    trips = lax.shift_right_logical(qi + 1, 1)
    init = (jnp.full((1, cols), MAX_FLOOR, F32), jnp.zeros((A_HEAD + MOBA_ROWS, cols), F32))
    m, acc = lax.fori_loop(0, trips, body, init)
    pv_prev = _bdot(values_t(jnp.maximum(2 * trips - 1, 0)), p_odd[...])
    key = lax.broadcasted_iota(jnp.int32, (MOBA_ROWS, cols), 0)
    pos = lax.broadcasted_iota(jnp.int32, (MOBA_ROWS, cols), 1) % MOBA_BLOCK
    causal = lambda r: jnp.where(key + r * MOBA_ROWS <= pos, 0.0, NEG_INF)
    _, alpha, p = softmax_step(s_own, m, elem_bias=causal)
    acc = alpha * (acc + pv_prev) + _bdot(values_t(qi), p)
    o = acc[:A_HEAD, :] / acc[A_HEAD:A_HEAD + 1, :]
    for g in range(A_GROUP):
        o_ref[g * A_HEAD:(g + 1) * A_HEAD, :] = o[:, g * MOBA_BLOCK:(g + 1) * MOBA_BLOCK]


def _moba_prompt(qt, kh, vt, nb, seq):
    nblk = seq // MOBA_BLOCK
    qspec = pl.BlockSpec((A_GROUP * A_HEAD, MOBA_BLOCK), lambda b, h, i: (h, b * nblk + i))
    return pl.pallas_call(
        _moba_kernel,
        grid=(nb, A_KV_HEADS, nblk),
        in_specs=[qspec,
                  pl.BlockSpec((1, seq, A_HEAD), lambda b, h, i: (h, b, 0)),
                  pl.BlockSpec((A_HEAD, seq), lambda b, h, i: (h, b))],
        out_specs=qspec,
        out_shape=jax.ShapeDtypeStruct(qt.shape, F32),
        scratch_shapes=[pltpu.VMEM((nblk, A_HEAD), F32), pltpu.VMEM((nblk, A_GROUP * MOBA_BLOCK), F32),
                        pltpu.VMEM((MOBA_BLOCK, A_GROUP * MOBA_BLOCK), F32),
                        pltpu.VMEM((MOBA_BLOCK, A_GROUP * MOBA_BLOCK), F32),
                        pltpu.VMEM((MOBA_BLOCK, A_GROUP * MOBA_BLOCK), F32),
                        pltpu.VMEM((MOBA_BLOCK, A_GROUP * MOBA_BLOCK), BF16)],
        compiler_params=_cparams("parallel", "parallel", "arbitrary"),
        name="moba_prompt",
    )(qt, kh, vt)


def _pages_t(cache):
    return cache.transpose(0, 1, 3, 4, 2).reshape(-1, A_KVW, PAGE_SIZE)


def _kmeans_kernel(n_in, pt_ref, *refs):
    pages, o_ref = refs[:n_in], refs[n_in]
    j = pl.program_id(1)
    per_blk = MOBA_BLOCK // PAGE_SIZE
    blocks = n_in // per_blk

    @pl.when(j == 0)
    def _():
        o_ref[...] = jnp.zeros(o_ref.shape, F32)

    out = o_ref[0]
    lane = lax.broadcasted_iota(jnp.int32, out.shape, 1)
    for i in range(blocks):
        tot = sum(pages[per_blk * i + p][0] for p in range(per_blk))
        mean = jnp.sum(tot, axis=1, keepdims=True) * (1.0 / MOBA_BLOCK)
        out = jnp.where(lane == j * blocks + i, mean, out)
    o_ref[0] = out


DEC_KMEANS_PAGES = 32


def _decode_kmeans(pages, page_table, layer, n_pool):
    nb, n_pages = page_table.shape
    n_in = min(DEC_KMEANS_PAGES, n_pages)
    steps = n_pages // n_in
    per_blk = MOBA_BLOCK // PAGE_SIZE

    def page_spec(i):
        return pl.BlockSpec((1, A_KVW, PAGE_SIZE),
                            lambda b, j, pt: (layer * n_pool + pt[b * n_pages + j * n_in + i], 0, 0))

    return pl.pallas_call(
        functools.partial(_kmeans_kernel, n_in),
        grid_spec=pltpu.PrefetchScalarGridSpec(
            num_scalar_prefetch=1,
            grid=(nb, steps),
            in_specs=[page_spec(i) for i in range(n_in)],
            out_specs=pl.BlockSpec((1, A_KVW, n_pages // per_blk), lambda b, j, pt: (b, 0, 0)),
        ),
        out_shape=jax.ShapeDtypeStruct((nb, A_KVW, n_pages // per_blk), F32),
        compiler_params=_cparams("parallel", "arbitrary"),
        name="decode_kmeans",
    )(page_table.reshape(-1), *([pages] * n_in))


def _decode_select_kernel(q_ref, km_ref, o_ref):
    nblk = km_ref.shape[2]
    gate = _bdot(q_ref[0], km_ref[0])
    col = lax.broadcasted_iota(jnp.int32, gate.shape, 1)
    lane = lax.broadcasted_iota(jnp.int32, (A_HEADS, LANES), 1)
    out = jnp.zeros((A_HEADS, LANES), jnp.int32)
    for r in range(MOBA_TOPK):
        m = jnp.max(gate, -1, keepdims=True)
        idx = jnp.min(jnp.where(gate == m, col, nblk), -1, keepdims=True)
        out = jnp.where(lane == r, idx, out)
        gate = jnp.where(col == idx, NEG_INF, gate)
    o_ref[0] = out


def _decode_select(q_exp, kmeans):
    nb, _, nblk = kmeans.shape
    return pl.pallas_call(
        _decode_select_kernel,
        grid=(nb,),
        in_specs=[pl.BlockSpec((1, A_HEADS, A_KVW), lambda b: (b, 0, 0)),
                  pl.BlockSpec((1, A_KVW, nblk), lambda b: (b, 0, 0))],
        out_specs=pl.BlockSpec((1, A_HEADS, LANES), lambda b: (b, 0, 0)),
        out_shape=jax.ShapeDtypeStruct((nb, A_HEADS, LANES), jnp.int32),
        compiler_params=_cparams("parallel"),
        name="decode_select",
    )(q_exp, kmeans)


DEC_PAGES = MOBA_TOPK * (MOBA_BLOCK // PAGE_SIZE)


DEC_HEADS = 4
DEC_STEP_PAGES = DEC_HEADS * DEC_PAGES


def _decode_attn_kernel(pg_ref, q_ref, kn_ref, vn_ref, *refs):
    k_pages, v_pages = refs[:DEC_STEP_PAGES], refs[DEC_STEP_PAGES:2 * DEC_STEP_PAGES]
    o_ref = refs[2 * DEC_STEP_PAGES]
    h0 = pl.program_id(1) * DEC_HEADS
    rb = lambda x: x.astype(BF16).astype(F32)
    heads = range(DEC_HEADS)
    pages = lambda lst, i: lst[i * DEC_PAGES:(i + 1) * DEC_PAGES]
    q = [q_ref[0, pl.ds(h0 + i, 1), :].astype(BF16) for i in heads]
    scores = [[_bdot(q[i], kp[0]) * ATT_SCALE for kp in pages(k_pages, i)] for i in heads]
    s_self = [jnp.sum(rb(q[i]) * rb(kn_ref[0]), -1, keepdims=True) * ATT_SCALE for i in heads]
    m = []
    for i in heads:
        top = s_self[i]
        for s in scores[i]:
            top = jnp.maximum(top, jnp.max(s, -1, keepdims=True))
        m.append(top)
    probs = [[jnp.exp(s - m[i]) for s in scores[i]] for i in heads]
    pv = [[_bdot(p, vp[0], _NT) for p, vp in zip(probs[i], pages(v_pages, i))] for i in heads]
    for i in heads:
        p_self = jnp.exp(s_self[i] - m[i])
        l = p_self
        acc = rb(p_self) * rb(vn_ref[0])
        for p, x in zip(probs[i], pv[i]):
            l = l + jnp.sum(p, -1, keepdims=True)
            acc = acc + x
        o_ref[0, pl.ds(h0 + i, 1), :] = acc / l


def _decode_attn(q_exp, k_new, v_new, kp, vp, page_ids, layer, n_pool):
    nb = q_exp.shape[0]

    def page_spec(i):
        return pl.BlockSpec(
            (1, A_KVW, PAGE_SIZE),
            lambda b, h, pg: (layer * n_pool + pg[(b * A_HEADS + h * DEC_HEADS) * DEC_PAGES + i], 0, 0))

    bspec = pl.BlockSpec((1, A_HEADS, A_KVW), lambda b, h, pg: (b, 0, 0))
    nspec = pl.BlockSpec((1, 1, A_KVW), lambda b, h, pg: (b, 0, 0))
    return pl.pallas_call(
        _decode_attn_kernel,
        grid_spec=pltpu.PrefetchScalarGridSpec(
            num_scalar_prefetch=1,
            grid=(nb, A_HEADS // DEC_HEADS),
            in_specs=[bspec, nspec, nspec] + [page_spec(i) for i in range(DEC_STEP_PAGES)] * 2,
            out_specs=bspec,
        ),
        out_shape=jax.ShapeDtypeStruct((nb, A_HEADS, A_KVW), F32),
        compiler_params=_cparams("parallel", "arbitrary"),
        name="decode_attn",
    )(page_ids.reshape(-1), q_exp, k_new.reshape(nb, 1, A_KVW), v_new.reshape(nb, 1, A_KVW),
      *([kp] * DEC_STEP_PAGES), *([vp] * DEC_STEP_PAGES))


def _moba_decode(q, k_new, v_new, k_pages, v_pages, n_pool, page_table, layer):
    nb = q.shape[0]
    onehot = (jnp.arange(A_KV_HEADS)[None, :] == (jnp.arange(A_HEADS) // A_GROUP)[:, None]).astype(F32)
    q_exp = (q.reshape(nb, A_HEADS, 1, A_HEAD) * onehot[None, :, :, None]).reshape(nb, A_HEADS, A_KVW)
    kmeans = _decode_kmeans(k_pages, page_table, layer, n_pool)
    blk = _decode_select(q_exp, kmeans)[:, :, :MOBA_TOPK]
    per_blk = MOBA_BLOCK // PAGE_SIZE
    pages_of = (blk[..., None] * per_blk + jnp.arange(per_blk)).reshape(nb, A_HEADS * DEC_PAGES)
    page_ids = jnp.take_along_axis(page_table, pages_of, axis=1)
    o = _decode_attn(q_exp, k_new, v_new, k_pages, v_pages, page_ids, layer, n_pool)
    o = o.reshape(nb, A_HEADS, A_KV_HEADS, A_HEAD)
    return jnp.einsum("bhkd,hk->bhd", o, onehot).reshape(nb, A_QW)


def _merge_kernel(ya_transposed, x_ref, ys_ref, yr_ref, ya_ref, wg_ref, ps_ref, pr_ref, pa_ref, wo_ref,
                  g_ref, b_ref, o_ref):
    x = x_ref[...]
    xb = x.astype(BF16)
    ya = ya_ref[...].T if ya_transposed else ya_ref[...]
    merged = None
    for i, (y, p_ref) in enumerate(((ys_ref[...], ps_ref), (yr_ref[...], pr_ref), (ya, pa_ref))):
        gate = jax.nn.sigmoid(_bdot(xb, wg_ref[:, i * D_MODEL:(i + 1) * D_MODEL]))
        term = gate * _bdot(y, p_ref[...])
        merged = term if merged is None else merged + term
    o_ref[...] = _ln(ALPHA * x + _bdot(merged, wo_ref[...]), g_ref[...], b_ref[...])


def _merge(x, ys, yr, ya, wg, ps, pr, pa, wo, g, b, tm, ya_transposed):
    m = x.shape[0]
    row = lambda width: pl.BlockSpec((tm, width), lambda i: (i, 0))
    ya_spec = pl.BlockSpec((A_QW, tm), lambda i: (0, i)) if ya_transposed else row(A_QW)
    return pl.pallas_call(
        functools.partial(_merge_kernel, ya_transposed),
        grid=(m // tm,),
        in_specs=[row(D_MODEL), row(S_WIDTH), row(R_WIDTH), ya_spec,
                  _resident(wg.shape), _resident(ps.shape), _resident(pr.shape), _resident(pa.shape),
                  _resident(wo.shape),
                  _full((1, D_MODEL)), _full((1, D_MODEL))],
        out_specs=row(D_MODEL),
        out_shape=jax.ShapeDtypeStruct((m, D_MODEL), F32),
        compiler_params=_cparams("parallel"),
        name="merge",
    )(x, ys, yr, ya, wg, ps, pr, pa, wo, g, b)


FFN_COLS = D_FF // 2


def _ffn_kernel(tiles_per_seq, x_ref, halo_ref, wup_ref, cw_ref, cb_ref, wdn_ref, g_ref, b_ref, o_ref):
    i = pl.program_id(0)
    x = x_ref[...]
    tm = x.shape[0]
    xe = jnp.concatenate([halo_ref[...], x], axis=0).astype(BF16)
    row = lax.broadcasted_iota(jnp.int32, (tm + SUBLANES, 1), 0)
    keep = jnp.logical_or(row >= SUBLANES, i % tiles_per_seq != 0)
    acc = jnp.zeros((tm, D_MODEL), F32)
    for c in range(D_FF // FFN_COLS):
        halves = []
        for off in (c * FFN_COLS, D_FF + c * FFN_COLS):
            up = jnp.where(keep, _bdot(xe, wup_ref[:, off:off + FFN_COLS]), 0.0)
            cv = cb_ref[:, off:off + FFN_COLS]
            for j in range(CONV_W):
                lo = SUBLANES - (CONV_W - 1) + j
                cv = cv + cw_ref[j:j + 1, off:off + FFN_COLS] * up[lo:lo + tm, :]
            halves.append(cv)
        hmid = jax.nn.gelu(halves[0]) * halves[1]
        acc = acc + _bdot(hmid, wdn_ref[c * FFN_COLS:(c + 1) * FFN_COLS, :])
    o_ref[...] = _ln(ALPHA * x + acc, g_ref[...], b_ref[...])


def _ffn(x, wup, cw, cb, wdn, g, b, seq, tm):
    m = x.shape[0]
    per8 = tm // SUBLANES
    return pl.pallas_call(
        functools.partial(_ffn_kernel, seq // tm),
        grid=(m // tm,),
        in_specs=[pl.BlockSpec((tm, D_MODEL), lambda i: (i, 0)),
                  pl.BlockSpec((SUBLANES, D_MODEL), lambda i: (jnp.maximum(i * per8 - 1, 0), 0)),
                  _resident(wup.shape), _full(cw.shape), _full(cb.shape), _resident(wdn.shape),
                  _full((1, D_MODEL)), _full((1, D_MODEL))],
        out_specs=pl.BlockSpec((tm, D_MODEL), lambda i: (i, 0)),
        out_shape=jax.ShapeDtypeStruct((m, D_MODEL), F32),
        compiler_params=_cparams("parallel"),
        name="ffn",
    )(x, x, wup, cw, cb, wdn, g, b)


def _ffn_step_kernel(x_ref, c0a_ref, c0b_ref, wup_ref, cw_ref, cb_ref, wdn_ref, g_ref, b_ref, o_ref, up_ref):
    x = x_ref[...]
    up = _bdot(x, wup_ref[...])
    up_ref[...] = up
    cv = cb_ref[...] + cw_ref[0:1, :] * c0a_ref[...] + cw_ref[1:2, :] * c0b_ref[...] + cw_ref[2:3, :] * up
    hmid = jax.nn.gelu(cv[:, :D_FF]) * cv[:, D_FF:]
    o_ref[...] = _ln(ALPHA * x + _bdot(hmid, wdn_ref[...]), g_ref[...], b_ref[...])


def _ffn_step(x, c0a, c0b, wup, cw, cb, wdn, g, b):
    n = x.shape[0]
    return pl.pallas_call(
        _ffn_step_kernel,
        out_shape=[jax.ShapeDtypeStruct((n, D_MODEL), F32), jax.ShapeDtypeStruct((n, 2 * D_FF), F32)],
        compiler_params=_cparams(),
        name="ffn_step",
    )(x, c0a, c0b, wup, cw, cb, wdn, g, b)


def _up_rows_kernel(x_ref, w_ref, o_ref):
    o_ref[...] = _bdot(x_ref[...], w_ref[...])


def _up_rows(x, w):
    return pl.pallas_call(
        _up_rows_kernel,
        out_shape=jax.ShapeDtypeStruct((x.shape[0], w.shape[1]), F32),
        compiler_params=_cparams(),
        name="ffn_up_tail",
    )(x, w)


def _layer_params(l, P):
    bf = lambda x: x.astype(BF16)
    row = lambda x: x.reshape(1, -1)
    w_in = P["w_in"][l]
    return dict(
        w_gate=bf(w_in[:, :O_SSM]), w_rest=bf(w_in[:, O_SSM:]),
        s5=_s5_params(P["ssm_a_re"][l], P["ssm_a_im"][l], P["ssm_log_dt"][l], P["ssm_b_re"][l],
                      P["ssm_b_im"][l], P["ssm_c_re"][l], P["ssm_c_im"][l]),
        ssm_d=row(P["ssm_d"][l]), w_glu=bf(P["ssm_w_glu"][l]), b_glu=row(P["ssm_b_glu"][l]),
        rwkv=_rwkv_weights(P["rwkv_mu"][l], P["rwkv_w0"][l], P["rwkv_w2"][l], P["rwkv_a0"][l],
                           P["rwkv_a2"][l], P["rwkv_g2"][l], P["rwkv_k_k"][l], P["rwkv_k_a"][l],
                           P["rwkv_r_k"][l]),
        lnx_g=row(P["rwkv_lnx_g"][l]), lnx_b=row(P["rwkv_lnx_b"][l]),
        proj_ssm=bf(P["proj_ssm"][l]), proj_rwkv=bf(P["proj_rwkv"][l]), proj_attn=bf(P["proj_attn"][l]),
        w_o=bf(P["w_o"][l]), ln1_g=row(P["ln1_g"][l]), ln1_b=row(P["ln1_b"][l]),
        w_up=bf(P["ffn_w_up"][l]), conv_w=P["ffn_conv_w"][l], conv_b=row(P["ffn_conv_b"][l]),
        w_down=bf(P["ffn_w_down"][l]), ln2_g=row(P["ln2_g"][l]), ln2_b=row(P["ln2_b"][l]),
    )


def _pick_tile(n, pref):
    t = min(pref, n)
    while n % t:
        t //= 2
    return t


def _prompt_trunk(x_prompt, P, layers):
    nb, seq, _ = x_prompt.shape
    m = nb * seq
    x = x_prompt.reshape(m, D_MODEL)
    tm = _pick_tile(seq, 512)
    rope = _rope_tables(jnp.arange(seq, dtype=jnp.int32))
    ln_g, ln_b = P["ln_in_g"].reshape(1, -1), P["ln_in_b"].reshape(1, -1)
    outs = []
    for l, lp in enumerate(layers):
        res = _in_proj(x, ln_g, ln_b, lp["w_rest"], rope, tm, seq // tm, pre_ln=(l == 0), attn_layouts=True)
        u, c, k, v, qt, kh, vt = res[:7]
        if l == 0:
            x = res[7]
        y_s, s_fin = _s5_prompt(u, lp["s5"], lp["ssm_d"], lp["w_glu"], lp["b_glu"],
                                jnp.zeros((nb, 2, S_LANES), F32), nb, seq, _pick_tile(seq, 256))
        y1, y2, g, f, gate, bonus = _rwkv_chunks(c, jnp.zeros((nb, R_IN), F32), lp["rwkv"], nb, seq)
        y_r, h_fin = _rwkv_scan(y1, y2, g, f, bonus, gate, lp["lnx_g"], lp["lnx_b"],
                                jnp.zeros((nb, R_HEADS, R_HEAD, R_HEAD), F32), nb, seq, RWKV_CHUNK)
        y_at = _moba_prompt(qt, kh, vt, nb, seq)
        x = _merge(x, y_s, y_r, y_at, lp["w_gate"], lp["proj_ssm"], lp["proj_rwkv"], lp["proj_attn"],
                   lp["w_o"], lp["ln1_g"], lp["ln1_b"], _pick_tile(seq, 512), ya_transposed=True)
        tail = x.reshape(nb, seq, D_MODEL)[:, seq - (CONV_W - 1):].reshape(nb * (CONV_W - 1), D_MODEL)
        conv = _up_rows(tail, lp["w_up"]).reshape(nb, CONV_W - 1, 2 * D_FF)
        x = _ffn(x, lp["w_up"], lp["conv_w"], lp["conv_b"], lp["w_down"], lp["ln2_g"], lp["ln2_b"],
                 seq, _pick_tile(seq, 512))
        outs.append(dict(
            k=k.reshape(nb, seq, A_KV_HEADS, A_HEAD), v=v.reshape(nb, seq, A_KV_HEADS, A_HEAD),
            s_re=s_fin[:, 0].reshape(nb, S_GROUPS, S_STATE), s_im=s_fin[:, 1].reshape(nb, S_GROUPS, S_STATE),
            rwkv=jnp.swapaxes(h_fin, -1, -2), shift=c.reshape(nb, seq, R_IN)[:, -1], conv=conv))
    return x.reshape(nb, seq, D_MODEL), outs


def _sample_trunk(x_sample, P, layers, cache_k, cache_v, page_table, st_re, st_im, st_rwkv, st_shift, st_conv):
    nb = x_sample.shape[0]
    x = x_sample.reshape(nb, D_MODEL)
    past_len = page_table.shape[1] * PAGE_SIZE
    rope = _rope_tables(jnp.full((nb,), past_len, jnp.int32))
    ln_g, ln_b = P["ln_in_g"].reshape(1, -1), P["ln_in_b"].reshape(1, -1)
    k_pages, v_pages = _pages_t(cache_k), _pages_t(cache_v)
    outs = []
    for l, lp in enumerate(layers):
        res = _in_proj(x, ln_g, ln_b, lp["w_rest"], rope, nb, 1, pre_ln=(l == 0), attn_layouts=False)
        u, c, k, v, q = res[:5]
        if l == 0:
            x = res[5]
        s0 = jnp.concatenate([st_re[l].reshape(nb, S_LANES), st_im[l].reshape(nb, S_LANES)], axis=1)
        y_s, s_new = _s5_step(u, lp["s5"], lp["ssm_d"], lp["w_glu"], lp["b_glu"], s0)
        r, lw, k2, vv, a, b, gate, bonus = _rwkv_pre_step(c, st_shift[l], lp["rwkv"])
        per_head = lambda t: t.reshape(nb * R_HEADS, R_HEAD)
        y_rows, s_rwkv = _rwkv_step([per_head(t) for t in (r, lw, k2, vv, a, b)],
                                    st_rwkv[l].reshape(nb * R_HEADS, R_HEAD, R_HEAD))
        y_r = _rwkv_post_call(y_rows.reshape(nb, R_WIDTH), bonus, gate, lp["lnx_g"], lp["lnx_b"])
        y_a = _moba_decode(q, k, v, k_pages, v_pages, cache_k.shape[1], page_table, l)
        x = _merge(x, y_s, y_r, y_a, lp["w_gate"], lp["proj_ssm"], lp["proj_rwkv"], lp["proj_attn"],
                   lp["w_o"], lp["ln1_g"], lp["ln1_b"], nb, ya_transposed=False)
        x, up = _ffn_step(x, st_conv[l][:, 0], st_conv[l][:, 1], lp["w_up"], lp["conv_w"], lp["conv_b"],
                          lp["w_down"], lp["ln2_g"], lp["ln2_b"])
        outs.append(dict(
            k=k.reshape(nb, 1, A_KV_HEADS, A_HEAD), v=v.reshape(nb, 1, A_KV_HEADS, A_HEAD),
            s_re=s_new[:, :S_LANES].reshape(nb, S_GROUPS, S_STATE),
            s_im=s_new[:, S_LANES:].reshape(nb, S_GROUPS, S_STATE),
            rwkv=s_rwkv.reshape(nb, R_HEADS, R_HEAD, R_HEAD), shift=c,
            conv=jnp.stack([st_conv[l][:, 1], up], axis=1)))
    return x.reshape(nb, 1, D_MODEL), outs


def kernel(x_prompt, x_sample, cache_k, cache_v, page_table, state_ssm_re, state_ssm_im, state_rwkv, state_rwkv_shift, state_conv, ln_in_g, ln_in_b, w_in, ssm_a_re, ssm_a_im, ssm_log_dt, ssm_b_re, ssm_b_im, ssm_c_re, ssm_c_im, ssm_d, ssm_w_glu, ssm_b_glu, rwkv_mu, rwkv_w0, rwkv_w2, rwkv_a0, rwkv_a2, rwkv_g2, rwkv_k_k, rwkv_k_a, rwkv_r_k, rwkv_lnx_g, rwkv_lnx_b, proj_ssm, proj_rwkv, proj_attn, w_o, ln1_g, ln1_b, ffn_w_up, ffn_conv_w, ffn_conv_b, ffn_w_down, ln2_g, ln2_b):
    P = dict(ln_in_g=ln_in_g, ln_in_b=ln_in_b, w_in=w_in,
             ssm_a_re=ssm_a_re, ssm_a_im=ssm_a_im, ssm_log_dt=ssm_log_dt,
             ssm_b_re=ssm_b_re, ssm_b_im=ssm_b_im, ssm_c_re=ssm_c_re, ssm_c_im=ssm_c_im,
             ssm_d=ssm_d, ssm_w_glu=ssm_w_glu, ssm_b_glu=ssm_b_glu,
             rwkv_mu=rwkv_mu, rwkv_w0=rwkv_w0, rwkv_w2=rwkv_w2, rwkv_a0=rwkv_a0, rwkv_a2=rwkv_a2,
             rwkv_g2=rwkv_g2, rwkv_k_k=rwkv_k_k, rwkv_k_a=rwkv_k_a, rwkv_r_k=rwkv_r_k,
             rwkv_lnx_g=rwkv_lnx_g, rwkv_lnx_b=rwkv_lnx_b,
             proj_ssm=proj_ssm, proj_rwkv=proj_rwkv, proj_attn=proj_attn, w_o=w_o,
             ln1_g=ln1_g, ln1_b=ln1_b, ffn_w_up=ffn_w_up, ffn_conv_w=ffn_conv_w,
             ffn_conv_b=ffn_conv_b, ffn_w_down=ffn_w_down, ln2_g=ln2_g, ln2_b=ln2_b)
    layers = [_layer_params(l, P) for l in range(w_in.shape[0])]
    y_p, op = _prompt_trunk(x_prompt, P, layers)
    y_s, os_ = _sample_trunk(x_sample, P, layers, cache_k, cache_v, page_table, state_ssm_re, state_ssm_im,
                             state_rwkv, state_rwkv_shift, state_conv)
    st = lambda outs, key: jnp.stack([o[key] for o in outs])
    return (y_p, y_s, st(op, "k"), st(op, "v"), st(os_, "k"), st(os_, "v"),
            st(op, "s_re"), st(op, "s_im"), st(os_, "s_re"), st(os_, "s_im"),
            st(op, "rwkv"), st(os_, "rwkv"), st(op, "shift"), st(os_, "shift"),
            st(op, "conv"), st(os_, "conv"))
```
